```python
import math
import jax, jax.numpy as jnp
from jax import lax
import numpy as np

D_MODEL = 1024
BATCH = 8
SEQ = 2048
DEPTH = 1
DEC_BATCH = 128
DEC_SEQ = 8
PAST_LEN = 8192
PAGE_SIZE = 128

WINDOW = 128
A_HEADS = 8
A_KV_HEADS = 2
A_HEAD_DIM = 64
A_GROUP = A_HEADS // A_KV_HEADS
B_HEADS = 4
B_KEY_DIM = 128
B_VAL_DIM = 128
CONV_W = 4
DN_CHUNK = 64
CONV_DIM = B_HEADS * (2 * B_KEY_DIM + B_VAL_DIM)
D_FF = 2816
EPS = 1e-6
IN_SIZES = (A_HEADS * A_HEAD_DIM, A_KV_HEADS * A_HEAD_DIM, A_KV_HEADS * A_HEAD_DIM,
            CONV_DIM, B_HEADS * B_VAL_DIM, B_HEADS, B_HEADS, D_MODEL, D_MODEL)
IN_DIM = sum(IN_SIZES)

kernel_name = 'hybrid_swa_sink_gated_deltanet_macaron_step'


def rms_norm(x, w):
    xf = x.astype(jnp.float32)
    y = xf * lax.rsqrt(jnp.mean(xf * xf, axis=-1, keepdims=True) + EPS)
    return (y * w.astype(jnp.float32)).astype(x.dtype)


def l2_normalize(x):
    return x * lax.rsqrt(jnp.sum(x * x, axis=-1, keepdims=True) + EPS)


def swiglu_ffn(x, w_up, w_down):
    gate, up = jnp.split(x @ w_up, 2, axis=-1)
    return (jax.nn.silu(gate) * up) @ w_down


def split_columns(z):
    offsets = [int(o) for o in np.cumsum(IN_SIZES)[:-1]]
    return jnp.split(z, offsets, axis=-1)


def sink_attend(q, k, v, mask, sinks):
    s = jnp.einsum('...qkgd,...skd->...kgqs', q, k, preferred_element_type=jnp.float32) * (A_HEAD_DIM ** -0.5)
    s = jnp.where(mask, s, -jnp.inf)
    sink = sinks.astype(jnp.float32).reshape(A_KV_HEADS, A_GROUP, 1, 1)
    m = jnp.maximum(jnp.max(s, axis=-1, keepdims=True), sink)
    e = jnp.exp(s - m)
    p = e / (jnp.sum(e, axis=-1, keepdims=True) + jnp.exp(sink - m))
    return jnp.einsum('...kgqs,...skd->...qkgd', p.astype(v.dtype), v)


def swa_prompt(q, k, v, sinks):
    bsz, seq, _ = q.shape
    nb = seq // WINDOW
    q = q.reshape(bsz, nb, WINDOW, A_KV_HEADS, A_GROUP, A_HEAD_DIM)
    pad = jnp.zeros((bsz, WINDOW, A_KV_HEADS, A_HEAD_DIM), k.dtype)
    kp = jnp.concatenate([pad, k], axis=1).reshape(bsz, nb + 1, WINDOW, A_KV_HEADS, A_HEAD_DIM)
    vp = jnp.concatenate([pad, v], axis=1).reshape(bsz, nb + 1, WINDOW, A_KV_HEADS, A_HEAD_DIM)
    kband = jnp.concatenate([kp[:, :-1], kp[:, 1:]], axis=2)
    vband = jnp.concatenate([vp[:, :-1], vp[:, 1:]], axis=2)
    n = jnp.arange(nb)[:, None, None]
    i = jnp.arange(WINDOW)[None, :, None]
    j = jnp.arange(2 * WINDOW)[None, None, :]
    diff = WINDOW + i - j
    mask = (diff >= 0) & (diff < WINDOW) & (n * WINDOW - WINDOW + j >= 0)
    o = sink_attend(q, kband, vband, mask[:, None, None], sinks)
    return o.reshape(bsz, seq, A_HEADS * A_HEAD_DIM), k[:, -WINDOW:], v[:, -WINDOW:]


def swa_sample(q, k, v, buf_k, buf_v, sinks):
    bsz, t, _ = q.shape
    wb = buf_k.shape[1]
    q = q.reshape(bsz, t, A_KV_HEADS, A_GROUP, A_HEAD_DIM)
    kc = jnp.concatenate([buf_k, k], axis=1)
    vc = jnp.concatenate([buf_v, v], axis=1)
    diff = (wb + jnp.arange(t))[:, None] - jnp.arange(wb + t)[None, :]
    mask = (diff >= 0) & (diff < WINDOW)
    o = sink_attend(q, kc, vc, mask, sinks)
    return o.reshape(bsz, t, A_HEADS * A_HEAD_DIM), kc[:, -wb:], vc[:, -wb:]


def causal_short_conv(x, buf, w):
    t = x.shape[1]
    xc = jnp.concatenate([buf, x], axis=1)
    out = sum(xc[:, j:j + t] * w[j] for j in range(CONV_W))
    return jax.nn.silu(out), xc[:, t:]


def gated_delta_chunked(q, k, v, g, beta, s0):
    bsz, t, h, dk = q.shape
    dv = v.shape[-1]
    c = DN_CHUNK if t % DN_CHUNK == 0 else t
    nc = t // c

    def blocks(x):
        return jnp.moveaxis(x.reshape((bsz, nc, c, h) + x.shape[3:]), 3, 2)

    q, k, v, g, beta = blocks(q), blocks(k), blocks(v), blocks(g), blocks(beta)
    gc = jnp.cumsum(g, axis=-1)
    incl = jnp.tril(jnp.ones((c, c), bool))
    strict = jnp.tril(jnp.ones((c, c), bool), -1)
    decay = jnp.exp(jnp.where(incl, gc[..., :, None] - gc[..., None, :], -jnp.inf))
    k_beta = k * beta[..., None]
    m = jnp.where(strict, jnp.einsum('...id,...jd->...ij', k_beta, k) * decay, 0.0)
    a = m + jnp.eye(c, dtype=m.dtype)
    rhs = jnp.concatenate([k_beta * jnp.exp(gc)[..., None], v * beta[..., None]], axis=-1)
    sol = lax.linalg.triangular_solve(a, rhs, left_side=True, lower=True, unit_diagonal=True)
    k_cum, v_u = sol[..., :dk], sol[..., dk:]
    qk = jnp.einsum('...id,...jd->...ij', q, k) * decay
    q_dec = q * jnp.exp(gc)[..., None]
    k_dec = k * jnp.exp(gc[..., -1:] - gc)[..., None]
    g_tot = jnp.exp(gc[..., -1])

    def step(s, xs):
        k_cum_n, v_u_n, qk_n, q_dec_n, k_dec_n, g_tot_n = xs
        v_new = v_u_n - jnp.einsum('bhcd,bhde->bhce', k_cum_n, s)
        o = jnp.einsum('bhcd,bhde->bhce', q_dec_n, s) + jnp.einsum('bhij,bhje->bhie', qk_n, v_new)
        s = s * g_tot_n[..., None, None] + jnp.einsum('bhcd,bhce->bhde', k_dec_n, v_new)
        return s, o

    xs = (jnp.moveaxis(k_cum, 1, 0), jnp.moveaxis(v_u, 1, 0), jnp.moveaxis(qk, 1, 0),
          jnp.moveaxis(q_dec, 1, 0), jnp.moveaxis(k_dec, 1, 0), jnp.moveaxis(g_tot, 1, 0))
    s_final, o = lax.scan(step, s0, xs)
    o = jnp.transpose(o, (1, 0, 3, 2, 4)).reshape(bsz, t, h, dv)
    return o, s_final


def parallel_mixers(h, w_in, attn_sinks, conv_w, dn_a_log, dn_dt_bias, dn_out_norm,
                    w_branch_a, w_branch_b, w_out, swa_k_buf, swa_v_buf, conv_buf, delta_state):
    bsz, t, _ = h.shape
    qa, ka, va, qkvb, zb, ab, bb, ga, gb = split_columns(h @ w_in)
    ka = ka.reshape(bsz, t, A_KV_HEADS, A_HEAD_DIM)
    va = va.reshape(bsz, t, A_KV_HEADS, A_HEAD_DIM)
    if swa_k_buf is None:
        oa, new_k, new_v = swa_prompt(qa, ka, va, attn_sinks)
    else:
        oa, new_k, new_v = swa_sample(qa, ka, va, swa_k_buf, swa_v_buf, attn_sinks)
    if conv_buf is None:
        conv_buf = jnp.zeros((bsz, CONV_W - 1, CONV_DIM), h.dtype)
    xb, new_conv = causal_short_conv(qkvb, conv_buf, conv_w)
    qb, kb, vb = jnp.split(xb.astype(jnp.float32), [B_HEADS * B_KEY_DIM, 2 * B_HEADS * B_KEY_DIM], axis=-1)
    qb = l2_normalize(qb.reshape(bsz, t, B_HEADS, B_KEY_DIM)) * (B_KEY_DIM ** -0.5)
    kb = l2_normalize(kb.reshape(bsz, t, B_HEADS, B_KEY_DIM))
    vb = vb.reshape(bsz, t, B_HEADS, B_VAL_DIM)
    g = -jnp.exp(dn_a_log.astype(jnp.float32)) * jax.nn.softplus(ab.astype(jnp.float32) + dn_dt_bias.astype(jnp.float32))
    beta = jax.nn.sigmoid(bb.astype(jnp.float32))
    if delta_state is None:
        s0 = jnp.zeros((bsz, B_HEADS, B_KEY_DIM, B_VAL_DIM), jnp.float32)
    else:
        s0 = delta_state.astype(jnp.float32)
    ob, new_s = gated_delta_chunked(qb, kb, vb, g, beta, s0)
    ob = rms_norm(ob, dn_out_norm) * jax.nn.silu(zb.astype(jnp.float32).reshape(bsz, t, B_HEADS, B_VAL_DIM))
    ob = ob.reshape(bsz, t, B_HEADS * B_VAL_DIM).astype(h.dtype)
    merged = jax.nn.sigmoid(ga) * (oa @ w_branch_a) + jax.nn.sigmoid(gb) * (ob @ w_branch_b)
    return merged @ w_out, new_k, new_v, new_conv, new_s.astype(h.dtype)


def decoder_layer(x, swa_k_buf, swa_v_buf, conv_buf, delta_state,
                  ffn1_norm_pre, ffn1_norm_post, ffn1_w_up, ffn1_w_down,
                  mix_norm_pre, mix_norm_post, w_in, attn_sinks, conv_w, dn_a_log, dn_dt_bias,
                  dn_out_norm, w_branch_a, w_branch_b, w_out,
                  ffn2_norm_pre, ffn2_norm_post, ffn2_w_up, ffn2_w_down):
    x = x + 0.5 * rms_norm(swiglu_ffn(rms_norm(x, ffn1_norm_pre), ffn1_w_up, ffn1_w_down), ffn1_norm_post)
    m, nk, nv, nc, ns = parallel_mixers(rms_norm(x, mix_norm_pre), w_in, attn_sinks, conv_w, dn_a_log,
                                        dn_dt_bias, dn_out_norm, w_branch_a, w_branch_b, w_out,
                                        swa_k_buf, swa_v_buf, conv_buf, delta_state)
    x = x + rms_norm(m, mix_norm_post)
    x = x + 0.5 * rms_norm(swiglu_ffn(rms_norm(x, ffn2_norm_pre), ffn2_w_up, ffn2_w_down), ffn2_norm_post)
    return x, nk, nv, nc, ns


def setup_inputs(seed: int = 0) -> dict:
    key = jax.random.key(seed)
    ks = iter(jax.random.split(key, 32))

    def nrm(shape, scale=1.0):
        return scale * jax.random.normal(next(ks), shape, jnp.float32)

    def gain(n):
        return 1.0 + 0.02 * nrm((DEPTH, n))

    win_buf = min(WINDOW, PAST_LEN)
    dt = jnp.exp(jax.random.uniform(next(ks), (DEPTH, B_HEADS), jnp.float32, math.log(1e-3), math.log(1e-1)))
    return {
        'x_prompt': nrm((BATCH, SEQ, D_MODEL)),
        'x_sample': nrm((DEC_BATCH, DEC_SEQ, D_MODEL)),
        'cache_swa_k': nrm((DEPTH, DEC_BATCH, win_buf, A_KV_HEADS, A_HEAD_DIM)),
        'cache_swa_v': nrm((DEPTH, DEC_BATCH, win_buf, A_KV_HEADS, A_HEAD_DIM)),
        'state_conv': nrm((DEPTH, DEC_BATCH, CONV_W - 1, CONV_DIM)),
        'state_delta': nrm((DEPTH, DEC_BATCH, B_HEADS, B_KEY_DIM, B_VAL_DIM), 0.1),
        'ffn1_norm_pre': gain(D_MODEL),
        'ffn1_norm_post': gain(D_MODEL),
        'ffn1_w_up': nrm((DEPTH, D_MODEL, 2 * D_FF), D_MODEL ** -0.5),
        'ffn1_w_down': nrm((DEPTH, D_FF, D_MODEL), D_FF ** -0.5),
        'mix_norm_pre': gain(D_MODEL),
        'mix_norm_post': gain(D_MODEL),
        'w_in': nrm((DEPTH, D_MODEL, IN_DIM), D_MODEL ** -0.5),
        'attn_sinks': nrm((DEPTH, A_HEADS)),
        'conv_w': nrm((DEPTH, CONV_W, CONV_DIM), CONV_W ** -0.5),
        'dn_a_log': jnp.log(jax.random.uniform(next(ks), (DEPTH, B_HEADS), jnp.float32, 1.0, 16.0)),
        'dn_dt_bias': jnp.log(jnp.expm1(dt)),
        'dn_out_norm': gain(B_VAL_DIM),
        'w_branch_a': nrm((DEPTH, A_HEADS * A_HEAD_DIM, D_MODEL), (A_HEADS * A_HEAD_DIM) ** -0.5),
        'w_branch_b': nrm((DEPTH, B_HEADS * B_VAL_DIM, D_MODEL), (B_HEADS * B_VAL_DIM) ** -0.5),
        'w_out': nrm((DEPTH, D_MODEL, D_MODEL), D_MODEL ** -0.5),
        'ffn2_norm_pre': gain(D_MODEL),
        'ffn2_norm_post': gain(D_MODEL),
        'ffn2_w_up': nrm((DEPTH, D_MODEL, 2 * D_FF), D_MODEL ** -0.5),
        'ffn2_w_down': nrm((DEPTH, D_FF, D_MODEL), D_FF ** -0.5),
    }


def reference(x_prompt, x_sample, cache_swa_k, cache_swa_v, state_conv, state_delta,
              ffn1_norm_pre, ffn1_norm_post, ffn1_w_up, ffn1_w_down,
              mix_norm_pre, mix_norm_post, w_in, attn_sinks, conv_w, dn_a_log, dn_dt_bias,
              dn_out_norm, w_branch_a, w_branch_b, w_out,
              ffn2_norm_pre, ffn2_norm_post, ffn2_w_up, ffn2_w_down):
    yp, ys = x_prompt, x_sample
    pk, pv, pc, ps = [], [], [], []
    sk, sv, sc, ss = [], [], [], []
    for l in range(DEPTH):
        lw = (ffn1_norm_pre[l], ffn1_norm_post[l], ffn1_w_up[l], ffn1_w_down[l],
              mix_norm_pre[l], mix_norm_post[l], w_in[l], attn_sinks[l], conv_w[l], dn_a_log[l],
              dn_dt_bias[l], dn_out_norm[l], w_branch_a[l], w_branch_b[l], w_out[l],
              ffn2_norm_pre[l], ffn2_norm_post[l], ffn2_w_up[l], ffn2_w_down[l])
        yp, k1, v1, c1, s1 = decoder_layer(yp, None, None, None, None, *lw)
        ys, k2, v2, c2, s2 = decoder_layer(ys, cache_swa_k[l], cache_swa_v[l], state_conv[l], state_delta[l], *lw)
        pk.append(k1); pv.append(v1); pc.append(c1); ps.append(s1)
        sk.append(k2); sv.append(v2); sc.append(c2); ss.append(s2)
    return (yp, ys,
            jnp.stack(pk), jnp.stack(pv), jnp.stack(pc), jnp.stack(ps),
            jnp.stack(sk), jnp.stack(sv), jnp.stack(sc), jnp.stack(ss))
```

```python
import functools

import jax
import jax.numpy as jnp
from jax import lax
from jax.experimental import pallas as pl
from jax.experimental.pallas import tpu as pltpu

WINDOW = 128
DN_CHUNK = 64
EPS = 1e-6
F32 = jnp.float32
BF16 = jnp.bfloat16
NEG_BIG = -1e30
LANES = 128
SUBLANES = 8
VMEM_LIMIT_BYTES = 56 * 1024 * 1024
FF_CHUNK = 256
SAMPLE_GROUP = 16


def _rms(x, w):
    return x * lax.rsqrt(jnp.mean(x * x, axis=-1, keepdims=True) + EPS) * w


def _silu(x):
    return x * jax.nn.sigmoid(x)


def _softplus(x):
    return jnp.maximum(x, 0.0) + jnp.log1p(jnp.exp(-jnp.abs(x)))


def _dot(a, b):
    return jnp.dot(a, b, preferred_element_type=F32)


def _dot_nt(a, b):
    return lax.dot_general(a, b, (((1,), (1,)), ((), ())), preferred_element_type=F32)


def _dot_tn(a, b):
    return lax.dot_general(a, b, (((0,), (0,)), ((), ())), preferred_element_type=F32)


def _split3(b):
    hi = b.astype(BF16)
    r = b - hi.astype(F32)
    mid = r.astype(BF16)
    lo = (r - mid.astype(F32)).astype(BF16)
    return hi, mid, lo


def _dot_sel(sel_bf16, b):
    hi, mid, lo = _split3(b)
    return _dot(sel_bf16, hi) + _dot(sel_bf16, mid) + _dot(sel_bf16, lo)


def _dot_hi(a, b):
    ah = a.astype(BF16)
    al = (a - ah.astype(F32)).astype(BF16)
    bh = b.astype(BF16)
    bl = (b - bh.astype(F32)).astype(BF16)
    return _dot(ah, bh) + _dot(ah, bl) + _dot(al, bh)


def _ffn_kernel(x_ref, pre_ref, post_ref, wg_ref, wu_ref, wd_ref, o_ref):
    x = x_ref[...]
    xn = _rms(x, pre_ref[...]).astype(BF16)
    d_ff = wd_ref.shape[0]
    acc = jnp.zeros(x.shape, F32)
    for c in range(d_ff // FF_CHUNK):
        sl = slice(c * FF_CHUNK, (c + 1) * FF_CHUNK)
        g = _dot(xn, wg_ref[:, sl])
        u = _dot(xn, wu_ref[:, sl])
        acc = acc + _dot((_silu(g) * u).astype(BF16), wd_ref[sl, :])
    o_ref[...] = x + 0.5 * _rms(acc, post_ref[...])


def _const_spec(shape):
    nd = len(shape)
    return pl.BlockSpec(shape, lambda *_: (0,) * nd)


def _weight_spec(shape):
    nd = len(shape)
    return pl.BlockSpec(shape, lambda *_: (0,) * nd, pipeline_mode=pl.Buffered(1))


def _row_tile(t):
    for tm in (512, 256, 128, 64, 32, 16, 8):
        if t % tm == 0:
            return tm
    raise ValueError(f"token count {t} is not a multiple of 8")


def _params(sem):
    return pltpu.CompilerParams(dimension_semantics=sem, vmem_limit_bytes=VMEM_LIMIT_BYTES)


def _ffn(x, pre, post, wg, wu, wd):
    t, d = x.shape
    f = wd.shape[0]
    assert f % FF_CHUNK == 0
    tm = _row_tile(t)
    row = pl.BlockSpec((tm, d), lambda i: (i, 0))
    return pl.pallas_call(
        _ffn_kernel,
        grid=(t // tm,),
        in_specs=[row, _const_spec((1, d)), _const_spec((1, d)),
                  _weight_spec((d, f)), _weight_spec((d, f)), _weight_spec((f, d))],
        out_specs=row,
        out_shape=jax.ShapeDtypeStruct((t, d), F32),
        compiler_params=_params(("parallel",)),
        name="ffn",
    )(x, pre, post, wg, wu, wd)


def _inproj_kernel(x_ref, pre_ref, wq_ref, wkv_ref, wb_ref, wz_ref, wgb_ref, alog_ref, dtb_ref,
                   q_ref, kv_ref, b_ref, z_ref, gb_ref, *, q_scale, n_heads_b):
    h = _rms(x_ref[...], pre_ref[...]).astype(BF16)
    q_ref[...] = (_dot(h, wq_ref[...]) * q_scale).astype(BF16)
    kv_ref[...] = _dot(h, wkv_ref[...])
    b_ref[...] = _dot(h, wb_ref[...])
    z_ref[...] = _dot(h, wz_ref[...])
    ab = _dot(h, wgb_ref[...])
    g = -jnp.exp(alog_ref[...]) * _softplus(ab + dtb_ref[...])
    beta = jax.nn.sigmoid(ab)
    lane = lax.broadcasted_iota(jnp.int32, ab.shape, 1)
    gb_ref[...] = jnp.where(lane < n_heads_b, g, beta)


def _inproj(x, pre, wq, wkv, wb, wz, wgb, alog, dtb, *, q_scale, n_heads_b):
    t, d = x.shape
    tm = _row_tile(t)
    row = lambda n: pl.BlockSpec((tm, n), lambda i: (i, 0))
    outs = [(wq.shape[1], BF16), (wkv.shape[1], F32), (wb.shape[1], F32), (wz.shape[1], F32),
            (wgb.shape[1], F32)]
    return pl.pallas_call(
        functools.partial(_inproj_kernel, q_scale=q_scale, n_heads_b=n_heads_b),
        grid=(t // tm,),
        in_specs=[row(d), _const_spec((1, d)),
                  _weight_spec(wq.shape), _weight_spec(wkv.shape), _weight_spec(wb.shape),
                  _weight_spec(wz.shape), _weight_spec(wgb.shape),
                  _const_spec(alog.shape), _const_spec(dtb.shape)],
        out_specs=[row(n) for n, _ in outs],
        out_shape=[jax.ShapeDtypeStruct((t, n), dt) for n, dt in outs],
        compiler_params=_params(("parallel",)),
        name="inproj",
    )(x, pre, wq, wkv, wb, wz, wgb, alog, dtb)


def _merge_kernel(x_ref, oa_ref, ob_ref, pre_ref, post_ref, wga_ref, wgb_ref, wa_ref, wb_ref,
                  wo_ref, o_ref):
    x = x_ref[...]
    h = _rms(x, pre_ref[...]).astype(BF16)
    ga = jax.nn.sigmoid(_dot(h, wga_ref[...]))
    gb = jax.nn.sigmoid(_dot(h, wgb_ref[...]))
    merged = ga * _dot(oa_ref[...], wa_ref[...]) + gb * _dot(ob_ref[...], wb_ref[...])
    m = _dot(merged.astype(BF16), wo_ref[...])
    o_ref[...] = x + _rms(m, post_ref[...])


def _merge(x, oa, ob, pre, post, wga, wgb, wa, wb, wo):
    t, d = x.shape
    tm = _row_tile(t)
    row = lambda n: pl.BlockSpec((tm, n), lambda i: (i, 0))
    return pl.pallas_call(
        _merge_kernel,
        grid=(t // tm,),
        in_specs=[row(d), row(oa.shape[1]), row(ob.shape[1]), _const_spec((1, d)),
                  _const_spec((1, d)), _weight_spec(wga.shape), _weight_spec(wgb.shape),
                  _weight_spec(wa.shape), _weight_spec(wb.shape), _weight_spec(wo.shape)],
        out_specs=row(d),
        out_shape=jax.ShapeDtypeStruct((t, d), F32),
        compiler_params=_params(("parallel",)),
        name="merge",
    )(x, oa, ob, pre, post, wga, wgb, wa, wb, wo)


def _sink_softmax(s, mask, sink):
    s = jnp.where(mask, s, NEG_BIG)
    m = jnp.maximum(jnp.max(s, axis=-1, keepdims=True), sink)
    e = jnp.exp(s - m)
    return e / (jnp.sum(e, axis=-1, keepdims=True) + jnp.exp(sink - m))


def _swa_prompt_kernel(sink_ref, q_ref, kvp_ref, kvc_ref, o_ref, *, kv_heads, group, head_dim):
    n = pl.program_id(1)
    w = q_ref.shape[0]
    q = q_ref[...]
    kvp = kvp_ref[...].astype(BF16)
    kvc = kvc_ref[...].astype(BF16)
    assert w & (w - 1) == 0
    i = lax.broadcasted_iota(jnp.int32, (group * w, 2 * w), 0) & (w - 1)
    j = lax.broadcasted_iota(jnp.int32, (group * w, 2 * w), 1)
    diff = w + i - j
    first = jnp.where(n > 0, 0, w)
    mask = (diff >= 0) & (diff < WINDOW) & (j >= first)
    outs = []
    for kv in range(kv_heads):
        ks = slice(kv * head_dim, (kv + 1) * head_dim)
        vs = slice((kv_heads + kv) * head_dim, (kv_heads + kv + 1) * head_dim)
        kband = jnp.concatenate([kvp[:, ks], kvc[:, ks]], axis=0)
        vband = jnp.concatenate([kvp[:, vs], kvc[:, vs]], axis=0)
        heads = [kv * group + g for g in range(group)]
        qs = jnp.concatenate([q[:, h * head_dim:(h + 1) * head_dim] for h in heads], axis=0)
        sink = jnp.concatenate([jnp.full((w, 1), sink_ref[h], F32) for h in heads], axis=0)
        p = _sink_softmax(_dot_nt(qs, kband), mask, sink)
        o = _dot(p.astype(BF16), vband)
        outs.extend(o[g * w:(g + 1) * w, :] for g in range(group))
    o_ref[...] = jnp.concatenate(outs, axis=1).astype(o_ref.dtype)


def _swa_prompt(sinks, q, kv, *, batch, kv_heads, head_dim):
    t, dq = q.shape
    seq = t // batch
    assert seq % WINDOW == 0
    nb = seq // WINDOW
    group = dq // (kv_heads * head_dim)
    return pl.pallas_call(
        functools.partial(_swa_prompt_kernel, kv_heads=kv_heads, group=group, head_dim=head_dim),
        grid=(batch, nb),
        in_specs=[pl.BlockSpec(memory_space=pltpu.SMEM),
                  pl.BlockSpec((WINDOW, dq), lambda b, n: (b * nb + n, 0)),
                  pl.BlockSpec((WINDOW, kv.shape[1]), lambda b, n: (b * nb + jnp.maximum(n - 1, 0), 0)),
                  pl.BlockSpec((WINDOW, kv.shape[1]), lambda b, n: (b * nb + n, 0))],
        out_specs=pl.BlockSpec((WINDOW, dq), lambda b, n: (b * nb + n, 0)),
        out_shape=jax.ShapeDtypeStruct((t, dq), BF16),
        compiler_params=_params(("parallel", "parallel")),
        name="swa_prompt",
    )(sinks, q, kv, kv)


def _swa_sample_kernel(sink_ref, q_ref, kvn_ref, ck_ref, cv_ref, o_ref, nk_ref, nv_ref, *,
                       kv_heads, group, head_dim):
    bb, t, _ = q_ref.shape
    wb = ck_ref.shape[1]
    dkv = kv_heads * head_dim
    q = q_ref[...]
    kc = jnp.concatenate([ck_ref[...], kvn_ref[:, :, :dkv]], axis=1)
    vc = jnp.concatenate([cv_ref[...], kvn_ref[:, :, dkv:]], axis=1)
    nk_ref[...] = kc[:, t:, :]
    nv_ref[...] = vc[:, t:, :]
    kcb = kc.astype(BF16)
    vcb = vc.astype(BF16)
    assert t & (t - 1) == 0
    i = lax.broadcasted_iota(jnp.int32, (group * t, wb + t), 0) & (t - 1)
    j = lax.broadcasted_iota(jnp.int32, (group * t, wb + t), 1)
    diff = wb + i - j
    mask = ((diff >= 0) & (diff < WINDOW))[None]
    outs = []
    for kv in range(kv_heads):
        hs = slice(kv * head_dim, (kv + 1) * head_dim)
        heads = [kv * group + g for g in range(group)]
        qs = jnp.concatenate([q[:, :, h * head_dim:(h + 1) * head_dim] for h in heads], axis=1)
        sink = jnp.concatenate([jnp.full((1, t, 1), sink_ref[h], F32) for h in heads], axis=1)
        s = jnp.einsum('bqd,bkd->bqk', qs, kcb[:, :, hs], preferred_element_type=F32)
        p = _sink_softmax(s, mask, sink)
        o = jnp.einsum('bqk,bkd->bqd', p.astype(BF16), vcb[:, :, hs], preferred_element_type=F32)
        outs.extend(o[:, g * t:(g + 1) * t, :] for g in range(group))
    o_ref[...] = jnp.concatenate(outs, axis=2).astype(o_ref.dtype)


def _swa_sample(sinks, q, kvn, ck, cv, *, kv_heads, head_dim):
    nbatch, t, dq = q.shape
    wb = ck.shape[1]
    group = dq // (kv_heads * head_dim)
    bb = SUBLANES if nbatch % SUBLANES == 0 else 1
    blk = lambda a: pl.BlockSpec((bb,) + a.shape[1:], lambda i: (i, 0, 0))
    return pl.pallas_call(
        functools.partial(_swa_sample_kernel, kv_heads=kv_heads, group=group, head_dim=head_dim),
        grid=(nbatch // bb,),
        in_specs=[pl.BlockSpec(memory_space=pltpu.SMEM), blk(q), blk(kvn), blk(ck), blk(cv)],
        out_specs=[blk(q), blk(ck), blk(cv)],
        out_shape=[jax.ShapeDtypeStruct(q.shape, BF16), jax.ShapeDtypeStruct(ck.shape, F32),
                   jax.ShapeDtypeStruct(cv.shape, F32)],
        compiler_params=_params(("parallel",)),
        name="swa_sample",
    )(sinks, q, kvn, ck, cv)


def _delta_intra(q, k, v, g_b, beta_b, c):
    r, dk = q.shape
    q = q * lax.rsqrt(jnp.sum(q * q, axis=-1, keepdims=True) + EPS) * (dk ** -0.5)
    k = k * lax.rsqrt(jnp.sum(k * k, axis=-1, keepdims=True) + EPS)
    ri = lax.broadcasted_iota(jnp.int32, (r, r), 0)
    ci = lax.broadcasted_iota(jnp.int32, (r, r), 1)
    one = jnp.ones((r, r), BF16)
    sel = lambda mask: jnp.where(mask, 1.0, 0.0).astype(BF16)
    incl, strict, upper, same_sel = ci <= ri, ci < ri, ri <= ci, one
    if r != c:
        assert c & (c - 1) == 0
        shift = c.bit_length() - 1
        same = (ri >> shift) == (ci >> shift)
        incl, strict, upper = same & incl, same & strict, same & upper
        same_sel = sel(same)
    gc = _dot_sel(sel(incl), g_b)
    g_last = _dot_sel(same_sel, g_b)
    g_row = _dot_sel(one, jnp.where(upper, g_b[:, :r], 0.0))
    decay = jnp.where(incl, jnp.exp(gc[:, :r] - g_row), 0.0)
    egc = jnp.exp(gc)
    kb = k * beta_b
    kbf = k.astype(BF16)
    m = jnp.where(strict, _dot_nt(kb.astype(BF16), kbf) * decay, 0.0)
    qk = _dot_nt(q.astype(BF16), kbf) * decay
    p = -m
    tinv = jnp.where(ri == ci, 1.0, 0.0) + p
    squarings = max((c - 1).bit_length() - 1, 0)
    for _ in range(squarings):
        p = _dot_hi(p, p)
        tinv = tinv + _dot_hi(tinv, p)
    sol = _dot_hi(tinv, jnp.concatenate([kb * egc, v * beta_b], axis=1))
    k_cum, v_u = sol[:, :dk], sol[:, dk:]
    q_dec = q * egc
    k_dec = k * jnp.exp(g_last - gc)
    return k_cum, v_u, qk, q_dec, k_dec, jnp.exp(g_last)


def _gated_out_norm(o, w, z):
    return (_rms(o, w) * _silu(z)).astype(BF16)


def _delta_prompt_kernel(x_ref, z_ref, gb_ref, cw_ref, nw_ref, o_ref, s_out_ref, xc_ref, s_ref, *,
                         heads, dk, dv, conv_w):
    n = pl.program_id(1)
    c = x_ref.shape[0]
    pad = SUBLANES

    @pl.when(n == 0)
    def _():
        xc_ref[0:pad, :] = jnp.zeros((pad, xc_ref.shape[1]), F32)
        s_ref[...] = jnp.zeros(s_ref.shape, F32)

    xc_ref[pad:pad + c, :] = x_ref[...]
    acc = None
    for jj in range(conv_w):
        off = pad - (conv_w - 1) + jj
        term = xc_ref[off:off + c, :] * cw_ref[jj:jj + 1, :]
        acc = term if acc is None else acc + term
    xb = _silu(acc)
    xc_ref[0:pad, :] = xc_ref[c:c + pad, :]

    gb = gb_ref[...]
    for h in range(heads):
        q = xb[:, h * dk:(h + 1) * dk]
        k = xb[:, (heads + h) * dk:(heads + h + 1) * dk]
        v = xb[:, 2 * heads * dk + h * dv:2 * heads * dk + (h + 1) * dv]
        g_b = jnp.broadcast_to(gb[:, h:h + 1], (c, dk))
        beta_b = jnp.broadcast_to(gb[:, heads + h:heads + h + 1], (c, dk))
        k_cum, v_u, qk, q_dec, k_dec, g_tot = _delta_intra(q, k, v, g_b, beta_b, c)
        s = s_ref[h]
        res = _dot(jnp.concatenate([k_cum, q_dec], axis=0).astype(BF16), s.astype(BF16))
        v_new = v_u - res[:c]
        o = res[c:] + _dot(qk.astype(BF16), v_new.astype(BF16))
        s_ref[h] = s * g_tot[0:1, :] + _dot_tn(k_dec.astype(BF16), v_new.astype(BF16))
        o_ref[:, h * dv:(h + 1) * dv] = _gated_out_norm(o, nw_ref[...], z_ref[:, h * dv:(h + 1) * dv])

    @pl.when(n == pl.num_programs(1) - 1)
    def _():
        s_out_ref[0] = s_ref[...]


def _delta_prompt(x, z, gb, conv_w, norm_w, *, batch, heads, dk, dv):
    t, cdim = x.shape
    seq = t // batch
    c = DN_CHUNK if seq % DN_CHUNK == 0 else seq
    assert c % SUBLANES == 0 and c >= conv_w.shape[0] - 1 and dk == dv
    nc = seq // c
    blk = lambda n_: pl.BlockSpec((c, n_), lambda b, n: (b * nc + n, 0))
    return pl.pallas_call(
        functools.partial(_delta_prompt_kernel, heads=heads, dk=dk, dv=dv, conv_w=conv_w.shape[0]),
        grid=(batch, nc),
        in_specs=[blk(cdim), blk(z.shape[1]), blk(gb.shape[1]),
                  _const_spec(conv_w.shape), _const_spec(norm_w.shape)],
        out_specs=[blk(heads * dv), pl.BlockSpec((1, heads, dk, dv), lambda b, n: (b, 0, 0, 0))],
        out_shape=[jax.ShapeDtypeStruct((t, heads * dv), BF16),
                   jax.ShapeDtypeStruct((batch, heads, dk, dv), F32)],
        scratch_shapes=[pltpu.VMEM((c + SUBLANES, cdim), F32), pltpu.VMEM((heads, dk, dv), F32)],
        compiler_params=_params(("parallel", "arbitrary")),
        name="delta_prompt",
    )(x, z, gb, conv_w, norm_w)


def _delta_sample_kernel(x_ref, cbuf_ref, z_ref, gb_ref, s_in_ref, cw_ref, nw_ref, o_ref, s_out_ref,
                         xc_ref, *, heads, dk, dv, conv_w):
    nb, hist, cdim = cbuf_ref.shape
    r = x_ref.shape[0]
    t = r // nb
    xc_ref[:, SUBLANES:SUBLANES + t, :] = x_ref[...].reshape(nb, t, cdim)
    xc_ref[:, SUBLANES - hist:SUBLANES, :] = cbuf_ref[...]
    acc = None
    for jj in range(conv_w):
        off = SUBLANES - (conv_w - 1) + jj
        term = xc_ref[:, off:off + t, :] * cw_ref[jj:jj + 1, :][None]
        acc = term if acc is None else acc + term
    xb = _silu(acc).reshape(r, cdim)

    gb = gb_ref[...]
    for h in range(heads):
        q = xb[:, h * dk:(h + 1) * dk]
        k = xb[:, (heads + h) * dk:(heads + h + 1) * dk]
        v = xb[:, 2 * heads * dk + h * dv:2 * heads * dk + (h + 1) * dv]
        g_b = jnp.broadcast_to(gb[:, h:h + 1], (r, dk))
        beta_b = jnp.broadcast_to(gb[:, heads + h:heads + h + 1], (r, dk))
        k_cum, v_u, qk, q_dec, k_dec, g_tot = _delta_intra(q, k, v, g_b, beta_b, t)
        v_new, o_state = [], []
        for b in range(nb):
            rows = slice(b * t, (b + 1) * t)
            s = s_in_ref[b, h]
            res = _dot(jnp.concatenate([k_cum[rows], q_dec[rows]], axis=0).astype(BF16), s.astype(BF16))
            vn = v_u[rows] - res[:t]
            v_new.append(vn)
            o_state.append(res[t:])
            s_out_ref[b, h] = (s * g_tot[b * t:b * t + 1, :]
                               + _dot_tn(k_dec[rows].astype(BF16), vn.astype(BF16)))
        v_new = jnp.concatenate(v_new, axis=0)
        o = jnp.concatenate(o_state, axis=0) + _dot(qk.astype(BF16), v_new.astype(BF16))
        o_ref[:, h * dv:(h + 1) * dv] = _gated_out_norm(o, nw_ref[...], z_ref[:, h * dv:(h + 1) * dv])


def _delta_sample(x, cbuf, z, gb, s0, conv_w, norm_w, *, heads, dk, dv):
    rows, cdim = x.shape
    nbatch, hist, _ = cbuf.shape
    t = rows // nbatch
    assert t == SUBLANES and hist == conv_w.shape[0] - 1 and hist <= t and dk == dv
    nb = SAMPLE_GROUP if nbatch % SAMPLE_GROUP == 0 else 1
    r = nb * t
    blk = lambda n_: pl.BlockSpec((r, n_), lambda i: (i, 0))
    sblk = pl.BlockSpec((nb, heads, dk, dv), lambda i: (i, 0, 0, 0))
    return pl.pallas_call(
        functools.partial(_delta_sample_kernel, heads=heads, dk=dk, dv=dv, conv_w=conv_w.shape[0]),
        grid=(nbatch // nb,),
        in_specs=[blk(cdim), pl.BlockSpec((nb, hist, cdim), lambda i: (i, 0, 0)), blk(z.shape[1]),
                  blk(gb.shape[1]), sblk, _const_spec(conv_w.shape), _const_spec(norm_w.shape)],
        out_specs=[blk(heads * dv), sblk],
        out_shape=[jax.ShapeDtypeStruct((rows, heads * dv), BF16), jax.ShapeDtypeStruct(s0.shape, F32)],
        scratch_shapes=[pltpu.VMEM((nb, 2 * SUBLANES, cdim), F32)],
        compiler_params=_params(("parallel",)),
        name="delta_sample",
    )(x, cbuf, z, gb, s0, conv_w, norm_w)


def _layer(xp, xs, ck, cv, cconv, sdelta, lw, dims):
    (f1_pre, f1_post, f1_up, f1_down, mix_pre, mix_post, w_in, sinks, conv_w, a_log, dt_bias,
     out_norm, w_a, w_b, w_o, f2_pre, f2_post, f2_up, f2_down) = lw
    batch, seq, d = xp.shape
    dbatch, dseq, _ = xs.shape
    kvh, hd = dims["kv_heads"], dims["head_dim"]
    hb, dk, dv = dims["b_heads"], dims["dk"], dims["dv"]
    a_heads = w_a.shape[0] // hd
    cdim = conv_w.shape[1]
    d_ff = f1_down.shape[0]
    bf = lambda a: a.astype(BF16)
    row = lambda a: a.reshape(1, -1).astype(F32)

    sizes = (a_heads * hd, kvh * hd, kvh * hd, cdim, hb * dv, hb, hb, d, d)
    offs = [0]
    for s_ in sizes:
        offs.append(offs[-1] + s_)
    col = lambda i0, i1: w_in[:, offs[i0]:offs[i1]]
    wq, wkv, wqkvb, wz = bf(col(0, 1)), bf(col(1, 3)), bf(col(3, 4)), bf(col(4, 5))
    wgbeta = bf(jnp.pad(col(5, 7), ((0, 0), (0, LANES - 2 * hb))))
    wga, wgb = bf(col(7, 8)), bf(col(8, 9))
    alog = jnp.pad(row(a_log), ((0, 0), (0, LANES - hb)))
    dtb = jnp.pad(row(dt_bias), ((0, 0), (0, LANES - hb)))

    def ffn(x, pre, post, up, down):
        return _ffn(x, row(pre), row(post), bf(up[:, :d_ff]), bf(up[:, d_ff:]), bf(down))

    def inproj(x):
        return _inproj(x, row(mix_pre), wq, wkv, wqkvb, wz, wgbeta, alog, dtb,
                       q_scale=hd ** -0.5, n_heads_b=hb)

    def merge(x, oa, ob):
        return _merge(x, oa, ob, row(mix_pre), row(mix_post), wga, wgb, bf(w_a), bf(w_b), bf(w_o))

    sinks = sinks.astype(F32)
    norm_w = row(out_norm)
    conv_w = conv_w.astype(F32)
    hist = conv_w.shape[0] - 1

    x = ffn(xp.reshape(batch * seq, d), f1_pre, f1_post, f1_up, f1_down)
    q, kv, xb, z, gb = inproj(x)
    oa = _swa_prompt(sinks, q, kv, batch=batch, kv_heads=kvh, head_dim=hd)
    ob, p_state = _delta_prompt(xb, z, gb, conv_w, norm_w, batch=batch, heads=hb, dk=dk, dv=dv)
    x = merge(x, oa, ob)
    yp = ffn(x, f2_pre, f2_post, f2_up, f2_down).reshape(batch, seq, d)
    kv3 = kv.reshape(batch, seq, 2, kvh, hd)
    p_k, p_v = kv3[:, seq - WINDOW:, 0], kv3[:, seq - WINDOW:, 1]
    p_conv = xb.reshape(batch, seq, cdim)[:, seq - hist:]

    x = ffn(xs.reshape(dbatch * dseq, d), f1_pre, f1_post, f1_up, f1_down)
    q, kv, xb, z, gb = inproj(x)
    wb = ck.shape[1]
    oa, s_k, s_v = _swa_sample(sinks, q.reshape(dbatch, dseq, -1), kv.reshape(dbatch, dseq, -1),
                               ck.reshape(dbatch, wb, kvh * hd), cv.reshape(dbatch, wb, kvh * hd),
                               kv_heads=kvh, head_dim=hd)
    ob, s_state = _delta_sample(xb, cconv, z, gb, sdelta, conv_w, norm_w, heads=hb, dk=dk, dv=dv)
    x = merge(x, oa.reshape(dbatch * dseq, -1), ob)
    ys = ffn(x, f2_pre, f2_post, f2_up, f2_down).reshape(dbatch, dseq, d)
    s_k = s_k.reshape(dbatch, wb, kvh, hd)
    s_v = s_v.reshape(dbatch, wb, kvh, hd)
    s_conv = xb.reshape(dbatch, dseq, cdim)[:, dseq - hist:]
    return yp, ys, (p_k, p_v, p_conv, p_state), (s_k, s_v, s_conv, s_state)


def kernel(x_prompt, x_sample, cache_swa_k, cache_swa_v, state_conv, state_delta, ffn1_norm_pre, ffn1_norm_post, ffn1_w_up, ffn1_w_down, mix_norm_pre, mix_norm_post, w_in, attn_sinks, conv_w, dn_a_log, dn_dt_bias, dn_out_norm, w_branch_a, w_branch_b, w_out, ffn2_norm_pre, ffn2_norm_post, ffn2_w_up, ffn2_w_down):
    weights = (ffn1_norm_pre, ffn1_norm_post, ffn1_w_up, ffn1_w_down, mix_norm_pre, mix_norm_post,
               w_in, attn_sinks, conv_w, dn_a_log, dn_dt_bias, dn_out_norm, w_branch_a, w_branch_b,
               w_out, ffn2_norm_pre, ffn2_norm_post, ffn2_w_up, ffn2_w_down)
    depth = w_in.shape[0]
    dims = dict(kv_heads=cache_swa_k.shape[3], head_dim=cache_swa_k.shape[4],
                b_heads=state_delta.shape[2], dk=state_delta.shape[3], dv=state_delta.shape[4])
    yp, ys = x_prompt, x_sample
    p_out, s_out = [], []
    for l in range(depth):
        yp, ys, p_new, s_new = _layer(yp, ys, cache_swa_k[l], cache_swa_v[l], state_conv[l],
                                      state_delta[l], tuple(w[l] for w in weights), dims)
        p_out.append(p_new)
        s_out.append(s_new)
    stack = lambda outs, i: jnp.stack([o[i] for o in outs])
    return (yp, ys,
            stack(p_out, 0), stack(p_out, 1), stack(p_out, 2), stack(p_out, 3),
            stack(s_out, 0), stack(s_out, 1), stack(s_out, 2), stack(s_out, 3))
```

```python
import functools

import jax
import jax.numpy as jnp
from jax import lax
from jax.experimental import pallas as pl
from jax.experimental.pallas import tpu as pltpu

WINDOW = 128
DN_CHUNK = 64
EPS = 1e-6
F32 = jnp.float32
BF16 = jnp.bfloat16
NEG_BIG = -1e30
LANES = 128
SUBLANES = 8
VMEM_LIMIT_BYTES = 56 * 1024 * 1024
FF_CHUNK = 256
SAMPLE_GROUP = 16
SWA_BLOCKS_PER_STEP = 2
PROMPT_CHUNKS_PER_STEP = 2


def _rms(x, w):
    return x * lax.rsqrt(jnp.mean(x * x, axis=-1, keepdims=True) + EPS) * w


def _silu(x):
    return x * jax.nn.sigmoid(x)


def _softplus(x):
    return jnp.maximum(x, 0.0) + jnp.log1p(jnp.exp(-jnp.abs(x)))


def _dot(a, b):
    return jnp.dot(a, b, preferred_element_type=F32)


def _dot_nt(a, b):
    return lax.dot_general(a, b, (((1,), (1,)), ((), ())), preferred_element_type=F32)


def _dot_tn(a, b):
    return lax.dot_general(a, b, (((0,), (0,)), ((), ())), preferred_element_type=F32)


def _split3(b):
    hi = b.astype(BF16)
    r = b - hi.astype(F32)
    mid = r.astype(BF16)
    lo = (r - mid.astype(F32)).astype(BF16)
    return hi, mid, lo


def _dot_sel(sel_bf16, b):
    hi, mid, lo = _split3(b)
    return _dot(sel_bf16, hi) + _dot(sel_bf16, mid) + _dot(sel_bf16, lo)


def _dot_hi(a, b):
    ah = a.astype(BF16)
    al = (a - ah.astype(F32)).astype(BF16)
    bh = b.astype(BF16)
    bl = (b - bh.astype(F32)).astype(BF16)
    return _dot(ah, bh) + _dot(ah, bl) + _dot(al, bh)


def _ffn_kernel(x_ref, pre_ref, post_ref, wg_ref, wu_ref, wd_ref, o_ref):
    x = x_ref[...]
    xn = _rms(x, pre_ref[...]).astype(BF16)
    d_ff = wd_ref.shape[0]
    acc = jnp.zeros(x.shape, F32)
    for c in range(d_ff // FF_CHUNK):
        sl = slice(c * FF_CHUNK, (c + 1) * FF_CHUNK)
        g = _dot(xn, wg_ref[:, sl])
        u = _dot(xn, wu_ref[:, sl])
        acc = acc + _dot((_silu(g) * u).astype(BF16), wd_ref[sl, :])
    o_ref[...] = x + 0.5 * _rms(acc, post_ref[...])


def _const_spec(shape):
    nd = len(shape)
    return pl.BlockSpec(shape, lambda *_: (0,) * nd)


def _weight_spec(shape):
    nd = len(shape)
    return pl.BlockSpec(shape, lambda *_: (0,) * nd, pipeline_mode=pl.Buffered(1))


def _row_tile(t):
    for tm in (512, 256, 128, 64, 32, 16, 8):
        if t % tm == 0:
            return tm
    raise ValueError(f"token count {t} is not a multiple of 8")


def _params(sem):
    return pltpu.CompilerParams(dimension_semantics=sem, vmem_limit_bytes=VMEM_LIMIT_BYTES)


def _ffn(x, pre, post, wg, wu, wd):
    t, d = x.shape
    f = wd.shape[0]
    assert f % FF_CHUNK == 0
    tm = _row_tile(t)
    row = pl.BlockSpec((tm, d), lambda i: (i, 0))
    return pl.pallas_call(
        _ffn_kernel,
        grid=(t // tm,),
        in_specs=[row, _const_spec((1, d)), _const_spec((1, d)),
                  _weight_spec((d, f)), _weight_spec((d, f)), _weight_spec((f, d))],
        out_specs=row,
        out_shape=jax.ShapeDtypeStruct((t, d), F32),
        compiler_params=_params(("parallel",)),
        name="ffn",
    )(x, pre, post, wg, wu, wd)


def _inproj_kernel(x_ref, pre_ref, wq_ref, wkv_ref, wb_ref, wz_ref, wgb_ref, alog_ref, dtb_ref,
                   q_ref, kv_ref, b_ref, z_ref, gb_ref, *, q_scale, n_heads_b):
    h = _rms(x_ref[...], pre_ref[...]).astype(BF16)
    q_ref[...] = (_dot(h, wq_ref[...]) * q_scale).astype(BF16)
    kv_ref[...] = _dot(h, wkv_ref[...])
    b_ref[...] = _dot(h, wb_ref[...])
    z_ref[...] = _dot(h, wz_ref[...])
    ab = _dot(h, wgb_ref[...])
    g = -jnp.exp(alog_ref[...]) * _softplus(ab + dtb_ref[...])
    beta = jax.nn.sigmoid(ab)
    lane = lax.broadcasted_iota(jnp.int32, ab.shape, 1)
    gb_ref[...] = jnp.where(lane < n_heads_b, g, beta)


def _inproj(x, pre, wq, wkv, wb, wz, wgb, alog, dtb, *, q_scale, n_heads_b):
    t, d = x.shape
    tm = _row_tile(t)
    row = lambda n: pl.BlockSpec((tm, n), lambda i: (i, 0))
    outs = [(wq.shape[1], BF16), (wkv.shape[1], F32), (wb.shape[1], F32), (wz.shape[1], F32),
            (wgb.shape[1], F32)]
    return pl.pallas_call(
        functools.partial(_inproj_kernel, q_scale=q_scale, n_heads_b=n_heads_b),
        grid=(t // tm,),
        in_specs=[row(d), _const_spec((1, d)),
                  _weight_spec(wq.shape), _weight_spec(wkv.shape), _weight_spec(wb.shape),
                  _weight_spec(wz.shape), _weight_spec(wgb.shape),
                  _const_spec(alog.shape), _const_spec(dtb.shape)],
        out_specs=[row(n) for n, _ in outs],
        out_shape=[jax.ShapeDtypeStruct((t, n), dt) for n, dt in outs],
        compiler_params=_params(("parallel",)),
        name="inproj",
    )(x, pre, wq, wkv, wb, wz, wgb, alog, dtb)


def _merge_kernel(x_ref, oa_ref, ob_ref, pre_ref, post_ref, wga_ref, wgb_ref, wa_ref, wb_ref,
                  wo_ref, o_ref):
    x = x_ref[...]
    h = _rms(x, pre_ref[...]).astype(BF16)
    ga = jax.nn.sigmoid(_dot(h, wga_ref[...]))
    gb = jax.nn.sigmoid(_dot(h, wgb_ref[...]))
    merged = ga * _dot(oa_ref[...], wa_ref[...]) + gb * _dot(ob_ref[...], wb_ref[...])
    m = _dot(merged.astype(BF16), wo_ref[...])
    o_ref[...] = x + _rms(m, post_ref[...])


def _merge(x, oa, ob, pre, post, wga, wgb, wa, wb, wo):
    t, d = x.shape
    tm = _row_tile(t)
    row = lambda n: pl.BlockSpec((tm, n), lambda i: (i, 0))
    return pl.pallas_call(
        _merge_kernel,
        grid=(t // tm,),
        in_specs=[row(d), row(oa.shape[1]), row(ob.shape[1]), _const_spec((1, d)),
                  _const_spec((1, d)), _weight_spec(wga.shape), _weight_spec(wgb.shape),
                  _weight_spec(wa.shape), _weight_spec(wb.shape), _weight_spec(wo.shape)],
        out_specs=row(d),
        out_shape=jax.ShapeDtypeStruct((t, d), F32),
        compiler_params=_params(("parallel",)),
        name="merge",
    )(x, oa, ob, pre, post, wga, wgb, wa, wb, wo)


def _sink_softmax(s, mask, sink):
    s = jnp.where(mask, s, NEG_BIG)
    m = jnp.maximum(jnp.max(s, axis=-1, keepdims=True), sink)
    e = jnp.exp(s - m)
    return e / (jnp.sum(e, axis=-1, keepdims=True) + jnp.exp(sink - m))


def _swa_prompt_kernel(sink_ref, q_ref, kvp_ref, kvc_ref, o_ref, *, kv_heads, group, head_dim):
    n = pl.program_id(1)
    w = WINDOW
    blocks = q_ref.shape[0] // w
    dkv = kv_heads * head_dim
    pairs = group // 2
    kv_all = jnp.concatenate([kvp_ref[...], kvc_ref[...]], axis=0)
    lo = lax.broadcasted_iota(jnp.int32, (kv_all.shape[0], dkv), 1) < head_dim

    def lane_halves(x):
        x_rot = pltpu.roll(x, head_dim, axis=1)
        pick = lambda kv, half: jnp.where(lo if half == 0 else ~lo, x if kv == half else x_rot, 0.0)
        return [[pick(kv, half).astype(BF16) for half in range(2)] for kv in range(kv_heads)]

    k_ext = lane_halves(kv_all[:, :dkv])
    v_ext = lane_halves(kv_all[:, dkv:])

    i = lax.broadcasted_iota(jnp.int32, (w, 2 * w), 0)
    j = lax.broadcasted_iota(jnp.int32, (w, 2 * w), 1)
    bias = jnp.where(j > i, jnp.where(j <= i + w, 0.0, NEG_BIG), NEG_BIG)
    bias_seq_start = jnp.where(j >= w, bias, NEG_BIG)
    lo_out = lax.broadcasted_iota(jnp.int32, (w, 2 * head_dim), 1) < head_dim

    inst = [(t, kv) for t in range(blocks) for kv in range(kv_heads)]
    band = lambda x, t: x[t * w:(t + 2) * w]
    q2 = [jnp.concatenate([q_ref[t * w:(t + 1) * w, (kv * pairs + p) * LANES:(kv * pairs + p + 1) * LANES]
                           for p in range(pairs)], axis=0) for t, kv in inst]
    scores = [[_dot_nt(q_, band(k_ext[kv][half], t)).reshape(pairs, w, 2 * w) for half in range(2)]
              for q_, (t, kv) in zip(q2, inst)]
    biases = [jnp.where(n > 0, bias, bias_seq_start) if t == 0 else bias for t, _ in inst]
    sinks = [[jnp.concatenate([jnp.full((1, w, 1), sink_ref[kv * group + 2 * p + half], F32)
                               for p in range(pairs)], axis=0) for half in range(2)] for _, kv in inst]
    scores = [[s + b[None] for s in sc] for sc, b in zip(scores, biases)]
    maxes = [[jnp.maximum(jnp.max(s, axis=-1, keepdims=True), sk) for s, sk in zip(sc, sks)]
             for sc, sks in zip(scores, sinks)]
    exps = [[jnp.exp(s - m) for s, m in zip(sc, ms)] for sc, ms in zip(scores, maxes)]
    recips = [[1.0 / (jnp.sum(e, axis=-1, keepdims=True) + jnp.exp(sk - m)) for e, sk, m in zip(es, sks, ms)]
              for es, sks, ms in zip(exps, sinks, maxes)]
    for (t, kv), es, rs in zip(inst, exps, recips):
        vband = jnp.concatenate([band(v_ext[kv][0], t), band(v_ext[kv][1], t)], axis=0)
        for p in range(pairs):
            e2 = jnp.concatenate([es[0][p], es[1][p]], axis=1).astype(BF16)
            scale = jnp.where(lo_out, rs[0][p], rs[1][p])
            cs = slice((kv * pairs + p) * LANES, (kv * pairs + p + 1) * LANES)
            o_ref[t * w:(t + 1) * w, cs] = (_dot(e2, vband) * scale).astype(o_ref.dtype)


def _swa_prompt(sinks, q, kv, *, batch, kv_heads, head_dim):
    t, dq = q.shape
    seq = t // batch
    assert seq % WINDOW == 0
    group = dq // (kv_heads * head_dim)
    assert 2 * head_dim == LANES and kv_heads == 2 and group % 2 == 0
    blocks = SWA_BLOCKS_PER_STEP if (seq // WINDOW) % SWA_BLOCKS_PER_STEP == 0 else 1
    nb = seq // (blocks * WINDOW)
    rows = blocks * WINDOW
    return pl.pallas_call(
        functools.partial(_swa_prompt_kernel, kv_heads=kv_heads, group=group, head_dim=head_dim),
        grid=(batch, nb),
        in_specs=[pl.BlockSpec(memory_space=pltpu.SMEM),
                  pl.BlockSpec((rows, dq), lambda b, n: (b * nb + n, 0)),
                  pl.BlockSpec((WINDOW, kv.shape[1]),
                               lambda b, n: ((b * nb + n) * blocks - jnp.minimum(n, 1), 0)),
                  pl.BlockSpec((rows, kv.shape[1]), lambda b, n: (b * nb + n, 0))],
        out_specs=pl.BlockSpec((rows, dq), lambda b, n: (b * nb + n, 0)),
        out_shape=jax.ShapeDtypeStruct((t, dq), BF16),
        compiler_params=_params(("parallel", "parallel")),
        name="swa_prompt",
    )(sinks, q, kv, kv)


def _swa_sample_kernel(sink_ref, q_ref, kvn_ref, ck_ref, cv_ref, o_ref, nk_ref, nv_ref, *,
                       kv_heads, group, head_dim):
    bb, t, _ = q_ref.shape
    wb = ck_ref.shape[1]
    dkv = kv_heads * head_dim
    q = q_ref[...]
    kc = jnp.concatenate([ck_ref[...], kvn_ref[:, :, :dkv]], axis=1)
    vc = jnp.concatenate([cv_ref[...], kvn_ref[:, :, dkv:]], axis=1)
    nk_ref[...] = kc[:, t:, :]
    nv_ref[...] = vc[:, t:, :]
    kcb = kc.astype(BF16)
    vcb = vc.astype(BF16)
    assert t & (t - 1) == 0
    i = lax.broadcasted_iota(jnp.int32, (group * t, wb + t), 0) & (t - 1)
    j = lax.broadcasted_iota(jnp.int32, (group * t, wb + t), 1)
    diff = wb + i - j
    mask = ((diff >= 0) & (diff < WINDOW))[None]
    outs = []
    for kv in range(kv_heads):
        hs = slice(kv * head_dim, (kv + 1) * head_dim)
        heads = [kv * group + g for g in range(group)]
        qs = jnp.concatenate([q[:, :, h * head_dim:(h + 1) * head_dim] for h in heads], axis=1)
        sink = jnp.concatenate([jnp.full((1, t, 1), sink_ref[h], F32) for h in heads], axis=1)
        s = jnp.einsum('bqd,bkd->bqk', qs, kcb[:, :, hs], preferred_element_type=F32)
        p = _sink_softmax(s, mask, sink)
        o = jnp.einsum('bqk,bkd->bqd', p.astype(BF16), vcb[:, :, hs], preferred_element_type=F32)
        outs.extend(o[:, g * t:(g + 1) * t, :] for g in range(group))
    o_ref[...] = jnp.concatenate(outs, axis=2).astype(o_ref.dtype)


def _swa_sample(sinks, q, kvn, ck, cv, *, kv_heads, head_dim):
    nbatch, t, dq = q.shape
    wb = ck.shape[1]
    group = dq // (kv_heads * head_dim)
    bb = SUBLANES if nbatch % SUBLANES == 0 else 1
    blk = lambda a: pl.BlockSpec((bb,) + a.shape[1:], lambda i: (i, 0, 0))
    return pl.pallas_call(
        functools.partial(_swa_sample_kernel, kv_heads=kv_heads, group=group, head_dim=head_dim),
        grid=(nbatch // bb,),
        in_specs=[pl.BlockSpec(memory_space=pltpu.SMEM), blk(q), blk(kvn), blk(ck), blk(cv)],
        out_specs=[blk(q), blk(ck), blk(cv)],
        out_shape=[jax.ShapeDtypeStruct(q.shape, BF16), jax.ShapeDtypeStruct(ck.shape, F32),
                   jax.ShapeDtypeStruct(cv.shape, F32)],
        compiler_params=_params(("parallel",)),
        name="swa_sample",
    )(sinks, q, kvn, ck, cv)


def _chunk_masks(r, c):
    ri = lax.broadcasted_iota(jnp.int32, (r, r), 0)
    ci = lax.broadcasted_iota(jnp.int32, (r, r), 1)
    sel = lambda mask: jnp.where(mask, 1.0, 0.0).astype(BF16)
    incl, strict, same_sel = ci <= ri, ci < ri, jnp.ones((r, r), BF16)
    if r != c:
        assert c & (c - 1) == 0
        shift = c.bit_length() - 1
        same = (ri >> shift) == (ci >> shift)
        incl, strict, same_sel = same & incl, same & strict, sel(same)
    return dict(incl=incl, strict=strict, eye=jnp.where(ri == ci, 1.0, 0.0),
                incl_sel=sel(incl), same_sel=same_sel)


def _delta_gates(gb, masks, heads, dk):
    r = gb.shape[0]
    gc_all = _dot_sel(masks["incl_sel"], gb)
    gl_all = _dot_sel(masks["same_sel"], gb)
    bc = lambda a, lane: jnp.broadcast_to(a[:, lane:lane + 1], (r, dk))
    return [(bc(gc_all, h), bc(gl_all, h), bc(gb, heads + h)) for h in range(heads)]


def _delta_intra(insts, masks, c):
    r, dk = insts[0][0].shape
    assert r <= dk
    incl, strict, eye = masks["incl"], masks["strict"], masks["eye"]
    unit = lambda x: x * lax.rsqrt(jnp.sum(x * x, axis=-1, keepdims=True) + EPS)
    qs = [unit(i[0]) * (dk ** -0.5) for i in insts]
    ks = [unit(i[1]) for i in insts]
    vs, gcs, gls, betas = ([i[j] for i in insts] for j in (2, 3, 4, 5))
    g_rows = [jnp.transpose(gc)[:r, :r] for gc in gcs]
    decays = [jnp.where(incl, jnp.exp(gc[:, :r] - g_row), 0.0) for gc, g_row in zip(gcs, g_rows)]
    egcs = [jnp.exp(gc) for gc in gcs]
    kbs = [k * beta for k, beta in zip(ks, betas)]
    kbfs = [k.astype(BF16) for k in ks]
    ms = [jnp.where(strict, _dot_nt(kb.astype(BF16), kbf) * decay, 0.0)
          for kb, kbf, decay in zip(kbs, kbfs, decays)]
    qks = [_dot_nt(q.astype(BF16), kbf) * decay for q, kbf, decay in zip(qs, kbfs, decays)]
    ps = [-m for m in ms]
    tinvs = [eye + p for p in ps]
    for _ in range(max((c - 1).bit_length() - 1, 0)):
        ps = [_dot_hi(p, p) for p in ps]
        tinvs = [t + _dot_hi(t, p) for t, p in zip(tinvs, ps)]
    sols = [_dot_hi(t, jnp.concatenate([kb * egc, v * beta], axis=1))
            for t, kb, egc, v, beta in zip(tinvs, kbs, egcs, vs, betas)]
    return [(sol[:, :dk], sol[:, dk:], qk, q * egc, k * jnp.exp(gl - gc), jnp.exp(gl))
            for sol, qk, q, egc, k, gl, gc in zip(sols, qks, qs, egcs, ks, gls, gcs)]


def _gated_out_norm(o, w, z):
    return (_rms(o, w) * _silu(z)).astype(BF16)


def _head_cols(xb, h, heads, dk, dv):
    q = xb[:, h * dk:(h + 1) * dk]
    k = xb[:, (heads + h) * dk:(heads + h + 1) * dk]
    v = xb[:, 2 * heads * dk + h * dv:2 * heads * dk + (h + 1) * dv]
    return q, k, v


def _delta_prompt_kernel(x_ref, z_ref, gb_ref, cw_ref, nw_ref, o_ref, s_out_ref, xc_ref, s_ref, *,
                         heads, dk, dv, conv_w, chunk):
    n = pl.program_id(1)
    rows = x_ref.shape[0]
    pad = SUBLANES

    @pl.when(n == 0)
    def _():
        xc_ref[0:pad, :] = jnp.zeros((pad, xc_ref.shape[1]), F32)
        s_ref[...] = jnp.zeros(s_ref.shape, F32)

    xc_ref[pad:pad + rows, :] = x_ref[...]
    acc = None
    for jj in range(conv_w):
        off = pad - (conv_w - 1) + jj
        term = xc_ref[off:off + rows, :] * cw_ref[jj:jj + 1, :]
        acc = term if acc is None else acc + term
    xb = _silu(acc)
    xc_ref[0:pad, :] = xc_ref[rows:rows + pad, :]

    gb = gb_ref[...]
    masks = _chunk_masks(chunk, chunk)
    insts = []
    for g in range(rows // chunk):
        rs = slice(g * chunk, (g + 1) * chunk)
        gates = _delta_gates(gb[rs], masks, heads, dk)
        insts.extend(_head_cols(xb[rs], h, heads, dk, dv) + gates[h] for h in range(heads))
    intra = _delta_intra(insts, masks, chunk)

    state = [s_ref[h] for h in range(heads)]
    for g in range(rows // chunk):
        rs = slice(g * chunk, (g + 1) * chunk)
        cur = intra[g * heads:(g + 1) * heads]
        res = [_dot(jnp.concatenate([i[0], i[3]], axis=0).astype(BF16), s.astype(BF16))
               for i, s in zip(cur, state)]
        v_new = [i[1] - r_[:chunk] for i, r_ in zip(cur, res)]
        outs = [r_[chunk:] + _dot(i[2].astype(BF16), vn.astype(BF16))
                for i, r_, vn in zip(cur, res, v_new)]
        state = [s * i[5][0:1, :] + _dot_tn(i[4].astype(BF16), vn.astype(BF16))
                 for i, s, vn in zip(cur, state, v_new)]
        for h in range(heads):
            cs = slice(h * dv, (h + 1) * dv)
            o_ref[rs, cs] = _gated_out_norm(outs[h], nw_ref[...], z_ref[rs, cs])
    for h in range(heads):
        s_ref[h] = state[h]

    @pl.when(n == pl.num_programs(1) - 1)
    def _():
        s_out_ref[0] = s_ref[...]


def _delta_prompt(x, z, gb, conv_w, norm_w, *, batch, heads, dk, dv):
    t, cdim = x.shape
    seq = t // batch
    c = DN_CHUNK if seq % DN_CHUNK == 0 else seq
    assert c % SUBLANES == 0 and c >= conv_w.shape[0] - 1 and dk == dv
    group = PROMPT_CHUNKS_PER_STEP if (seq // c) % PROMPT_CHUNKS_PER_STEP == 0 else 1
    rows = group * c
    steps = seq // rows
    blk = lambda n_: pl.BlockSpec((rows, n_), lambda b, n: (b * steps + n, 0))
    return pl.pallas_call(
        functools.partial(_delta_prompt_kernel, heads=heads, dk=dk, dv=dv, conv_w=conv_w.shape[0],
                          chunk=c),
        grid=(batch, steps),
        in_specs=[blk(cdim), blk(z.shape[1]), blk(gb.shape[1]),
                  _const_spec(conv_w.shape), _const_spec(norm_w.shape)],
        out_specs=[blk(heads * dv), pl.BlockSpec((1, heads, dk, dv), lambda b, n: (b, 0, 0, 0))],
        out_shape=[jax.ShapeDtypeStruct((t, heads * dv), BF16),
                   jax.ShapeDtypeStruct((batch, heads, dk, dv), F32)],
        scratch_shapes=[pltpu.VMEM((rows + SUBLANES, cdim), F32), pltpu.VMEM((heads, dk, dv), F32)],
        compiler_params=_params(("parallel", "arbitrary")),
        name="delta_prompt",
    )(x, z, gb, conv_w, norm_w)


def _delta_sample_kernel(x_ref, cbuf_ref, z_ref, gb_ref, s_in_ref, cw_ref, nw_ref, o_ref, s_out_ref,
                         xc_ref, *, heads, dk, dv, conv_w):
    nb, hist, cdim = cbuf_ref.shape
    r = x_ref.shape[0]
    t = r // nb
    xc_ref[:, SUBLANES:SUBLANES + t, :] = x_ref[...].reshape(nb, t, cdim)
    xc_ref[:, SUBLANES - hist:SUBLANES, :] = cbuf_ref[...]
    acc = None
    for jj in range(conv_w):
        off = SUBLANES - (conv_w - 1) + jj
        term = xc_ref[:, off:off + t, :] * cw_ref[jj:jj + 1, :][None]
        acc = term if acc is None else acc + term
    xb = _silu(acc).reshape(r, cdim)

    masks = _chunk_masks(r, t)
    gates = _delta_gates(gb_ref[...], masks, heads, dk)
    intra = _delta_intra([_head_cols(xb, h, heads, dk, dv) + gates[h] for h in range(heads)], masks, t)
    rows = [slice(b * t, (b + 1) * t) for b in range(nb)]
    for h in range(heads):
        k_cum, v_u, qk, q_dec, k_dec, g_tot = intra[h]
        state = [s_in_ref[b, h] for b in range(nb)]
        res = [_dot(jnp.concatenate([k_cum[rs], q_dec[rs]], axis=0).astype(BF16), s.astype(BF16))
               for rs, s in zip(rows, state)]
        v_new = [v_u[rs] - r_[:t] for rs, r_ in zip(rows, res)]
        for b in range(nb):
            s_out_ref[b, h] = (state[b] * g_tot[b * t:b * t + 1, :]
                               + _dot_tn(k_dec[rows[b]].astype(BF16), v_new[b].astype(BF16)))
        o = (jnp.concatenate([r_[t:] for r_ in res], axis=0)
             + _dot(qk.astype(BF16), jnp.concatenate(v_new, axis=0).astype(BF16)))
        o_ref[:, h * dv:(h + 1) * dv] = _gated_out_norm(o, nw_ref[...], z_ref[:, h * dv:(h + 1) * dv])


def _delta_sample(x, cbuf, z, gb, s0, conv_w, norm_w, *, heads, dk, dv):
    rows, cdim = x.shape
    nbatch, hist, _ = cbuf.shape
    t = rows // nbatch
    assert t == SUBLANES and hist == conv_w.shape[0] - 1 and hist <= t and dk == dv
    nb = SAMPLE_GROUP if nbatch % SAMPLE_GROUP == 0 else 1
    r = nb * t
    blk = lambda n_: pl.BlockSpec((r, n_), lambda i: (i, 0))
    sblk = pl.BlockSpec((nb, heads, dk, dv), lambda i: (i, 0, 0, 0))
    return pl.pallas_call(
        functools.partial(_delta_sample_kernel, heads=heads, dk=dk, dv=dv, conv_w=conv_w.shape[0]),
        grid=(nbatch // nb,),
        in_specs=[blk(cdim), pl.BlockSpec((nb, hist, cdim), lambda i: (i, 0, 0)), blk(z.shape[1]),
                  blk(gb.shape[1]), sblk, _const_spec(conv_w.shape), _const_spec(norm_w.shape)],
        out_specs=[blk(heads * dv), sblk],
        out_shape=[jax.ShapeDtypeStruct((rows, heads * dv), BF16), jax.ShapeDtypeStruct(s0.shape, F32)],
        scratch_shapes=[pltpu.VMEM((nb, 2 * SUBLANES, cdim), F32)],
        compiler_params=_params(("parallel",)),
        name="delta_sample",
    )(x, cbuf, z, gb, s0, conv_w, norm_w)


def _layer(xp, xs, ck, cv, cconv, sdelta, lw, dims):
    (f1_pre, f1_post, f1_up, f1_down, mix_pre, mix_post, w_in, sinks, conv_w, a_log, dt_bias,
     out_norm, w_a, w_b, w_o, f2_pre, f2_post, f2_up, f2_down) = lw
    batch, seq, d = xp.shape
    dbatch, dseq, _ = xs.shape
    kvh, hd = dims["kv_heads"], dims["head_dim"]
    hb, dk, dv = dims["b_heads"], dims["dk"], dims["dv"]
    a_heads = w_a.shape[0] // hd
    cdim = conv_w.shape[1]
    d_ff = f1_down.shape[0]
    bf = lambda a: a.astype(BF16)
    row = lambda a: a.reshape(1, -1).astype(F32)

    sizes = (a_heads * hd, kvh * hd, kvh * hd, cdim, hb * dv, hb, hb, d, d)
    offs = [0]
    for s_ in sizes:
        offs.append(offs[-1] + s_)
    col = lambda i0, i1: w_in[:, offs[i0]:offs[i1]]
    wq, wkv, wqkvb, wz = bf(col(0, 1)), bf(col(1, 3)), bf(col(3, 4)), bf(col(4, 5))
    wgbeta = bf(jnp.pad(col(5, 7), ((0, 0), (0, LANES - 2 * hb))))
    wga, wgb = bf(col(7, 8)), bf(col(8, 9))
    alog = jnp.pad(row(a_log), ((0, 0), (0, LANES - hb)))
    dtb = jnp.pad(row(dt_bias), ((0, 0), (0, LANES - hb)))

    def ffn(x, pre, post, up, down):
        return _ffn(x, row(pre), row(post), bf(up[:, :d_ff]), bf(up[:, d_ff:]), bf(down))

    def inproj(x):
        return _inproj(x, row(mix_pre), wq, wkv, wqkvb, wz, wgbeta, alog, dtb,
                       q_scale=hd ** -0.5, n_heads_b=hb)

    def merge(x, oa, ob):
        return _merge(x, oa, ob, row(mix_pre), row(mix_post), wga, wgb, bf(w_a), bf(w_b), bf(w_o))

    sinks = sinks.astype(F32)
    norm_w = row(out_norm)
    conv_w = conv_w.astype(F32)
    hist = conv_w.shape[0] - 1

    x = ffn(xp.reshape(batch * seq, d), f1_pre, f1_post, f1_up, f1_down)
    q, kv, xb, z, gb = inproj(x)
    oa = _swa_prompt(sinks, q, kv, batch=batch, kv_heads=kvh, head_dim=hd)
    ob, p_state = _delta_prompt(xb, z, gb, conv_w, norm_w, batch=batch, heads=hb, dk=dk, dv=dv)
    x = merge(x, oa, ob)
    yp = ffn(x, f2_pre, f2_post, f2_up, f2_down).reshape(batch, seq, d)
    kv3 = kv.reshape(batch, seq, 2, kvh, hd)
    p_k, p_v = kv3[:, seq - WINDOW:, 0], kv3[:, seq - WINDOW:, 1]
    p_conv = xb.reshape(batch, seq, cdim)[:, seq - hist:]

    x = ffn(xs.reshape(dbatch * dseq, d), f1_pre, f1_post, f1_up, f1_down)
    q, kv, xb, z, gb = inproj(x)
    wb = ck.shape[1]
    oa, s_k, s_v = _swa_sample(sinks, q.reshape(dbatch, dseq, -1), kv.reshape(dbatch, dseq, -1),
                               ck.reshape(dbatch, wb, kvh * hd), cv.reshape(dbatch, wb, kvh * hd),
                               kv_heads=kvh, head_dim=hd)
    ob, s_state = _delta_sample(xb, cconv, z, gb, sdelta, conv_w, norm_w, heads=hb, dk=dk, dv=dv)
    x = merge(x, oa.reshape(dbatch * dseq, -1), ob)
    ys = ffn(x, f2_pre, f2_post, f2_up, f2_down).reshape(dbatch, dseq, d)
    s_k = s_k.reshape(dbatch, wb, kvh, hd)
    s_v = s_v.reshape(dbatch, wb, kvh, hd)
    s_conv = xb.reshape(dbatch, dseq, cdim)[:, dseq - hist:]
    return yp, ys, (p_k, p_v, p_conv, p_state), (s_k, s_v, s_conv, s_state)


def kernel(x_prompt, x_sample, cache_swa_k, cache_swa_v, state_conv, state_delta, ffn1_norm_pre, ffn1_norm_post, ffn1_w_up, ffn1_w_down, mix_norm_pre, mix_norm_post, w_in, attn_sinks, conv_w, dn_a_log, dn_dt_bias, dn_out_norm, w_branch_a, w_branch_b, w_out, ffn2_norm_pre, ffn2_norm_post, ffn2_w_up, ffn2_w_down):
    weights = (ffn1_norm_pre, ffn1_norm_post, ffn1_w_up, ffn1_w_down, mix_norm_pre, mix_norm_post,
               w_in, attn_sinks, conv_w, dn_a_log, dn_dt_bias, dn_out_norm, w_branch_a, w_branch_b,
               w_out, ffn2_norm_pre, ffn2_norm_post, ffn2_w_up, ffn2_w_down)
    depth = w_in.shape[0]
    dims = dict(kv_heads=cache_swa_k.shape[3], head_dim=cache_swa_k.shape[4],
                b_heads=state_delta.shape[2], dk=state_delta.shape[3], dv=state_delta.shape[4])
    yp, ys = x_prompt, x_sample
    p_out, s_out = [], []
    for l in range(depth):
        yp, ys, p_new, s_new = _layer(yp, ys, cache_swa_k[l], cache_swa_v[l], state_conv[l],
                                      state_delta[l], tuple(w[l] for w in weights), dims)
        p_out.append(p_new)
        s_out.append(s_new)
    stack = lambda outs, i: jnp.stack([o[i] for o in outs])
    return (yp, ys,
            stack(p_out, 0), stack(p_out, 1), stack(p_out, 2), stack(p_out, 3),
            stack(s_out, 0), stack(s_out, 1), stack(s_out, 2), stack(s_out, 3))
```

```python
import functools

import jax
import jax.numpy as jnp
from jax import lax
from jax.experimental import pallas as pl
from jax.experimental.pallas import tpu as pltpu

WINDOW = 128
DN_CHUNK = 64
EPS = 1e-6
F32 = jnp.float32
BF16 = jnp.bfloat16
NEG_BIG = -1e30
LANES = 128
SUBLANES = 8
VMEM_LIMIT_BYTES = 56 * 1024 * 1024
FF_CHUNK = 256
SAMPLE_GROUP = 16
SWA_BLOCKS_PER_STEP = 4
PROMPT_CHUNKS_PER_STEP = 2


def _rms(x, w):
    return x * lax.rsqrt(jnp.mean(x * x, axis=-1, keepdims=True) + EPS) * w


def _silu(x):
    return x * jax.nn.sigmoid(x)


def _softplus(x):
    return jnp.maximum(x, 0.0) + jnp.log1p(jnp.exp(-jnp.abs(x)))


def _dot(a, b):
    return jnp.dot(a, b, preferred_element_type=F32)


def _dot_nt(a, b):
    return lax.dot_general(a, b, (((1,), (1,)), ((), ())), preferred_element_type=F32)


def _dot_tn(a, b):
    return lax.dot_general(a, b, (((0,), (0,)), ((), ())), preferred_element_type=F32)


def _split3(b):
    hi = b.astype(BF16)
    r = b - hi.astype(F32)
    mid = r.astype(BF16)
    lo = (r - mid.astype(F32)).astype(BF16)
    return hi, mid, lo


def _dot_sel(sel_bf16, b):
    hi, mid, lo = _split3(b)
    return _dot(sel_bf16, hi) + _dot(sel_bf16, mid) + _dot(sel_bf16, lo)


def _ffn_kernel(x_ref, pre_ref, post_ref, wg_ref, wu_ref, wd_ref, o_ref):
    x = x_ref[...]
    xn = _rms(x, pre_ref[...]).astype(BF16)
    d_ff = wd_ref.shape[0]
    acc = jnp.zeros(x.shape, F32)
    for c in range(d_ff // FF_CHUNK):
        sl = slice(c * FF_CHUNK, (c + 1) * FF_CHUNK)
        g = _dot(xn, wg_ref[:, sl])
        u = _dot(xn, wu_ref[:, sl])
        acc = acc + _dot((_silu(g) * u).astype(BF16), wd_ref[sl, :])
    o_ref[...] = x + 0.5 * _rms(acc, post_ref[...])


def _const_spec(shape):
    nd = len(shape)
    return pl.BlockSpec(shape, lambda *_: (0,) * nd)


def _weight_spec(shape):
    nd = len(shape)
    return pl.BlockSpec(shape, lambda *_: (0,) * nd, pipeline_mode=pl.Buffered(1))


def _row_tile(t):
    for tm in (512, 256, 128, 64, 32, 16, 8):
        if t % tm == 0:
            return tm
    raise ValueError(f"token count {t} is not a multiple of 8")


def _params(sem):
    return pltpu.CompilerParams(dimension_semantics=sem, vmem_limit_bytes=VMEM_LIMIT_BYTES)


def _ffn(x, pre, post, wg, wu, wd):
    t, d = x.shape
    f = wd.shape[0]
    assert f % FF_CHUNK == 0
    tm = _row_tile(t)
    row = pl.BlockSpec((tm, d), lambda i: (i, 0))
    return pl.pallas_call(
        _ffn_kernel,
        grid=(t // tm,),
        in_specs=[row, _const_spec((1, d)), _const_spec((1, d)),
                  _weight_spec((d, f)), _weight_spec((d, f)), _weight_spec((f, d))],
        out_specs=row,
        out_shape=jax.ShapeDtypeStruct((t, d), F32),
        compiler_params=_params(("parallel",)),
        name="ffn",
    )(x, pre, post, wg, wu, wd)


def _inproj_kernel(x_ref, pre_ref, wq_ref, wkv_ref, wb_ref, wz_ref, wgb_ref, alog_ref, dtb_ref,
                   q_ref, kv_ref, b_ref, z_ref, gb_ref, *, q_scale, n_heads_b):
    h = _rms(x_ref[...], pre_ref[...]).astype(BF16)
    q_ref[...] = (_dot(h, wq_ref[...]) * q_scale).astype(BF16)
    kv_ref[...] = _dot(h, wkv_ref[...])
    b_ref[...] = _dot(h, wb_ref[...])
    z_ref[...] = _dot(h, wz_ref[...])
    ab = _dot(h, wgb_ref[...])
    g = -jnp.exp(alog_ref[...]) * _softplus(ab + dtb_ref[...])
    beta = jax.nn.sigmoid(ab)
    lane = lax.broadcasted_iota(jnp.int32, ab.shape, 1)
    gb_ref[...] = jnp.where(lane < n_heads_b, g, beta)


def _inproj(x, pre, wq, wkv, wb, wz, wgb, alog, dtb, *, q_scale, n_heads_b):
    t, d = x.shape
    tm = _row_tile(t)
    row = lambda n: pl.BlockSpec((tm, n), lambda i: (i, 0))
    outs = [(wq.shape[1], BF16), (wkv.shape[1], F32), (wb.shape[1], F32), (wz.shape[1], F32),
            (wgb.shape[1], F32)]
    return pl.pallas_call(
        functools.partial(_inproj_kernel, q_scale=q_scale, n_heads_b=n_heads_b),
        grid=(t // tm,),
        in_specs=[row(d), _const_spec((1, d)),
                  _weight_spec(wq.shape), _weight_spec(wkv.shape), _weight_spec(wb.shape),
                  _weight_spec(wz.shape), _weight_spec(wgb.shape),
                  _const_spec(alog.shape), _const_spec(dtb.shape)],
        out_specs=[row(n) for n, _ in outs],
        out_shape=[jax.ShapeDtypeStruct((t, n), dt) for n, dt in outs],
        compiler_params=_params(("parallel",)),
        name="inproj",
    )(x, pre, wq, wkv, wb, wz, wgb, alog, dtb)


def _merge_kernel(x_ref, oa_ref, ob_ref, pre_ref, post_ref, wga_ref, wgb_ref, wa_ref, wb_ref,
                  wo_ref, o_ref):
    x = x_ref[...]
    h = _rms(x, pre_ref[...]).astype(BF16)
    ga = jax.nn.sigmoid(_dot(h, wga_ref[...]))
    gb = jax.nn.sigmoid(_dot(h, wgb_ref[...]))
    merged = ga * _dot(oa_ref[...], wa_ref[...]) + gb * _dot(ob_ref[...], wb_ref[...])
    m = _dot(merged.astype(BF16), wo_ref[...])
    o_ref[...] = x + _rms(m, post_ref[...])


def _merge(x, oa, ob, pre, post, wga, wgb, wa, wb, wo):
    t, d = x.shape
    tm = _row_tile(t)
    row = lambda n: pl.BlockSpec((tm, n), lambda i: (i, 0))
    return pl.pallas_call(
        _merge_kernel,
        grid=(t // tm,),
        in_specs=[row(d), row(oa.shape[1]), row(ob.shape[1]), _const_spec((1, d)),
                  _const_spec((1, d)), _weight_spec(wga.shape), _weight_spec(wgb.shape),
                  _weight_spec(wa.shape), _weight_spec(wb.shape), _weight_spec(wo.shape)],
        out_specs=row(d),
        out_shape=jax.ShapeDtypeStruct((t, d), F32),
        compiler_params=_params(("parallel",)),
        name="merge",
    )(x, oa, ob, pre, post, wga, wgb, wa, wb, wo)


def _sink_softmax(s, mask, sink):
    s = jnp.where(mask, s, NEG_BIG)
    m = jnp.maximum(jnp.max(s, axis=-1, keepdims=True), sink)
    e = jnp.exp(s - m)
    return e / (jnp.sum(e, axis=-1, keepdims=True) + jnp.exp(sink - m))


def _swa_prompt_kernel(sink_ref, q_ref, kvp_ref, kvc_ref, o_ref, *, kv_heads, group, head_dim):
    n = pl.program_id(1)
    w = WINDOW
    blocks = q_ref.shape[0] // w
    dkv = kv_heads * head_dim
    pairs = group // 2
    kv_all = jnp.concatenate([kvp_ref[...], kvc_ref[...]], axis=0)
    lo = lax.broadcasted_iota(jnp.int32, (kv_all.shape[0], dkv), 1) < head_dim

    def lane_halves(x):
        x_rot = pltpu.roll(x, head_dim, axis=1)
        pick = lambda kv, half: jnp.where(lo if half == 0 else ~lo, x if kv == half else x_rot, 0.0)
        return [[pick(kv, half).astype(BF16) for half in range(2)] for kv in range(kv_heads)]

    k_ext = lane_halves(kv_all[:, :dkv])
    v_ext = lane_halves(kv_all[:, dkv:])

    i = lax.broadcasted_iota(jnp.int32, (w, 2 * w), 0)
    j = lax.broadcasted_iota(jnp.int32, (w, 2 * w), 1)
    bias = jnp.where(j > i, jnp.where(j <= i + w, 0.0, NEG_BIG), NEG_BIG)
    bias_seq_start = jnp.where(j >= w, bias, NEG_BIG)
    lo_out = lax.broadcasted_iota(jnp.int32, (w, 2 * head_dim), 1) < head_dim

    ones_half = [jnp.where(lo if half == 0 else ~lo, 1.0, 0.0).astype(BF16) for half in range(2)]
    v_ext = [[jnp.concatenate([v_ext[kv][half], ones_half[half]], axis=1) for half in range(2)]
             for kv in range(kv_heads)]

    inst = [(t, kv) for t in range(blocks) for kv in range(kv_heads)]
    band = lambda x, t: x[t * w:(t + 2) * w]
    q2 = [jnp.concatenate([q_ref[t * w:(t + 1) * w, (kv * pairs + p) * LANES:(kv * pairs + p + 1) * LANES]
                           for p in range(pairs)], axis=0) for t, kv in inst]
    scores = [[_dot_nt(q_, band(k_ext[kv][half], t)) for half in range(2)]
              for q_, (t, kv) in zip(q2, inst)]
    bias_first = jnp.where(n > 0, bias, bias_seq_start)
    for (t, kv), sc in zip(inst, scores):
        vband = jnp.concatenate([band(v_ext[kv][0], t), band(v_ext[kv][1], t)], axis=0)
        for p in range(pairs):
            exps, sink_terms = [], []
            for half in range(2):
                sink = sink_ref[kv * group + 2 * p + half]
                s = sc[half][p * w:(p + 1) * w] + (bias_first if t == 0 else bias)
                m = jnp.maximum(jnp.max(s, axis=-1, keepdims=True), sink)
                exps.append(jnp.exp(s - m).astype(BF16))
                sink_terms.append(jnp.exp(sink - m))
            o = _dot(jnp.concatenate(exps, axis=1), vband)
            den = o[:, LANES:] + jnp.where(lo_out, sink_terms[0], sink_terms[1])
            cs = slice((kv * pairs + p) * LANES, (kv * pairs + p + 1) * LANES)
            o_ref[t * w:(t + 1) * w, cs] = (o[:, :LANES] / den).astype(o_ref.dtype)


def _swa_prompt(sinks, q, kv, *, batch, kv_heads, head_dim):
    t, dq = q.shape
    seq = t // batch
    assert seq % WINDOW == 0
    group = dq // (kv_heads * head_dim)
    assert 2 * head_dim == LANES and kv_heads == 2 and group % 2 == 0
    blocks = SWA_BLOCKS_PER_STEP if (seq // WINDOW) % SWA_BLOCKS_PER_STEP == 0 else 1
    nb = seq // (blocks * WINDOW)
    rows = blocks * WINDOW
    return pl.pallas_call(
        functools.partial(_swa_prompt_kernel, kv_heads=kv_heads, group=group, head_dim=head_dim),
        grid=(batch, nb),
        in_specs=[pl.BlockSpec(memory_space=pltpu.SMEM),
                  pl.BlockSpec((rows, dq), lambda b, n: (b * nb + n, 0)),
                  pl.BlockSpec((WINDOW, kv.shape[1]),
                               lambda b, n: ((b * nb + n) * blocks - jnp.minimum(n, 1), 0)),
                  pl.BlockSpec((rows, kv.shape[1]), lambda b, n: (b * nb + n, 0))],
        out_specs=pl.BlockSpec((rows, dq), lambda b, n: (b * nb + n, 0)),
        out_shape=jax.ShapeDtypeStruct((t, dq), BF16),
        compiler_params=_params(("parallel", "parallel")),
        name="swa_prompt",
    )(sinks, q, kv, kv)


def _swa_sample_kernel(sink_ref, q_ref, kvn_ref, ck_ref, cv_ref, o_ref, nk_ref, nv_ref, *,
                       kv_heads, group, head_dim):
    bb, t, _ = q_ref.shape
    wb = ck_ref.shape[1]
    dkv = kv_heads * head_dim
    q = q_ref[...]
    kc = jnp.concatenate([ck_ref[...], kvn_ref[:, :, :dkv]], axis=1)
    vc = jnp.concatenate([cv_ref[...], kvn_ref[:, :, dkv:]], axis=1)
    nk_ref[...] = kc[:, t:, :]
    nv_ref[...] = vc[:, t:, :]
    kcb = kc.astype(BF16)
    vcb = vc.astype(BF16)
    assert t & (t - 1) == 0
    i = lax.broadcasted_iota(jnp.int32, (group * t, wb + t), 0) & (t - 1)
    j = lax.broadcasted_iota(jnp.int32, (group * t, wb + t), 1)
    diff = wb + i - j
    mask = ((diff >= 0) & (diff < WINDOW))[None]
    outs = []
    for kv in range(kv_heads):
        hs = slice(kv * head_dim, (kv + 1) * head_dim)
        heads = [kv * group + g for g in range(group)]
        qs = jnp.concatenate([q[:, :, h * head_dim:(h + 1) * head_dim] for h in heads], axis=1)
        sink = jnp.concatenate([jnp.full((1, t, 1), sink_ref[h], F32) for h in heads], axis=1)
        s = jnp.einsum('bqd,bkd->bqk', qs, kcb[:, :, hs], preferred_element_type=F32)
        p = _sink_softmax(s, mask, sink)
        o = jnp.einsum('bqk,bkd->bqd', p.astype(BF16), vcb[:, :, hs], preferred_element_type=F32)
        outs.extend(o[:, g * t:(g + 1) * t, :] for g in range(group))
    o_ref[...] = jnp.concatenate(outs, axis=2).astype(o_ref.dtype)


def _swa_sample(sinks, q, kvn, ck, cv, *, kv_heads, head_dim):
    nbatch, t, dq = q.shape
    wb = ck.shape[1]
    group = dq // (kv_heads * head_dim)
    bb = SUBLANES if nbatch % SUBLANES == 0 else 1
    blk = lambda a: pl.BlockSpec((bb,) + a.shape[1:], lambda i: (i, 0, 0))
    return pl.pallas_call(
        functools.partial(_swa_sample_kernel, kv_heads=kv_heads, group=group, head_dim=head_dim),
        grid=(nbatch // bb,),
        in_specs=[pl.BlockSpec(memory_space=pltpu.SMEM), blk(q), blk(kvn), blk(ck), blk(cv)],
        out_specs=[blk(q), blk(ck), blk(cv)],
        out_shape=[jax.ShapeDtypeStruct(q.shape, BF16), jax.ShapeDtypeStruct(ck.shape, F32),
                   jax.ShapeDtypeStruct(cv.shape, F32)],
        compiler_params=_params(("parallel",)),
        name="swa_sample",
    )(sinks, q, kvn, ck, cv)


def _chunk_masks(r, c):
    assert r & (r - 1) == 0 and c & (c - 1) == 0
    ri = lax.broadcasted_iota(jnp.int32, (r, 2 * r), 0)
    cj = lax.broadcasted_iota(jnp.int32, (r, 2 * r), 1)
    ci = cj & (r - 1)
    incl, strict = ci <= ri, ci < ri
    if r != c:
        shift = c.bit_length() - 1
        same = (ri >> shift) == (ci >> shift)
        incl, strict = same & incl, same & strict
    f01 = lambda mask: jnp.where(mask, 1.0, 0.0)
    same_sel = jnp.ones((r, r), BF16) if r == c else f01(same)[:, :r].astype(BF16)
    lo = cj < r
    return dict(incl=incl, strict=strict, eye=f01(ri == ci), lo=lo,
                keep_lo=f01(lo).astype(BF16), keep_hi=f01(~lo).astype(BF16),
                incl_sel=f01(incl)[:, :r].astype(BF16), same_sel=same_sel)


def _delta_gates(gb, masks, heads, dk):
    r = gb.shape[0]
    gc_all = _dot_sel(masks["incl_sel"], gb)
    gl_all = _dot_sel(masks["same_sel"], gb)
    bc = lambda a, lane: jnp.broadcast_to(a[:, lane:lane + 1], (r, dk))
    return [(bc(gc_all, h), bc(gl_all, h), bc(gb, heads + h)) for h in range(heads)]


def _split2(x):
    hi = x.astype(BF16)
    return hi, (x - hi.astype(F32)).astype(BF16)


def _dot_hi_split(a, b):
    return _dot(a[0], b[0]) + _dot(a[0], b[1]) + _dot(a[1], b[0])


def _delta_intra(insts, masks, c):
    r, dk = insts[0][0].shape
    dv = insts[0][2].shape[1]
    assert r <= dk and len(insts) % 2 == 0
    incl, strict, eye, lo = masks["incl"], masks["strict"], masks["eye"], masks["lo"]
    pairs = [(2 * i, 2 * i + 1) for i in range(len(insts) // 2)]
    unit = lambda x: x * lax.rsqrt(jnp.sum(x * x, axis=-1, keepdims=True) + EPS)
    qs = [unit(i[0]) * (dk ** -0.5) for i in insts]
    ks = [unit(i[1]) for i in insts]
    vs, gcs, gls, betas = ([i[j] for i in insts] for j in (2, 3, 4, 5))

    def lane_pair(a, b):
        if r % LANES == 0:
            return jnp.concatenate([a[:, :r], b[:, :r]], axis=1)
        assert a.shape[1] == 2 * r
        return jnp.where(lo, a, b)

    def block_diag(x):
        return jnp.concatenate([x * masks["keep_lo"], x * masks["keep_hi"]], axis=0)

    def split_bd(x):
        hi, lo_ = _split2(x)
        return (hi, lo_), (block_diag(hi), block_diag(lo_))

    g_cols = [lane_pair(gcs[a], gcs[b]) for a, b in pairs]
    g_rows = [jnp.transpose(jnp.concatenate([gcs[a], gcs[b]], axis=0))[:r, :] for a, b in pairs]
    decays = [jnp.where(incl, jnp.exp(gc - gr), 0.0) for gc, gr in zip(g_cols, g_rows)]
    egcs = [jnp.exp(gc) for gc in gcs]
    kbs = [k * beta for k, beta in zip(ks, betas)]
    zk = jnp.zeros((r, dk), BF16)
    k_bd = [jnp.concatenate([jnp.concatenate([ks[a].astype(BF16), zk], axis=1),
                             jnp.concatenate([zk, ks[b].astype(BF16)], axis=1)], axis=0) for a, b in pairs]
    both = lambda xs, a, b: jnp.concatenate([xs[a].astype(BF16), xs[b].astype(BF16)], axis=1)
    ms = [jnp.where(strict, _dot_nt(both(kbs, a, b), kd) * decay, 0.0)
          for (a, b), kd, decay in zip(pairs, k_bd, decays)]
    qks = [_dot_nt(both(qs, a, b), kd) * decay for (a, b), kd, decay in zip(pairs, k_bd, decays)]
    ps = [-m for m in ms]
    tinvs = [eye + p for p in ps]
    p_split = [split_bd(p) for p in ps]
    for _ in range(max((c - 1).bit_length() - 1, 0)):
        ps = [_dot_hi_split(x, x_bd) for x, x_bd in p_split]
        p_split = [split_bd(p) for p in ps]
        tinvs = [t + _dot_hi_split(_split2(t), x_bd) for t, (_, x_bd) in zip(tinvs, p_split)]
    zr = jnp.zeros((r, dk + dv), BF16)
    rhs = [_split2(jnp.concatenate([kb * egc, v * beta], axis=1))
           for kb, egc, v, beta in zip(kbs, egcs, vs, betas)]
    rhs_bd = [tuple(jnp.concatenate([jnp.concatenate([rhs[a][i], zr], axis=1),
                                     jnp.concatenate([zr, rhs[b][i]], axis=1)], axis=0) for i in range(2))
              for a, b in pairs]
    sols = [_dot_hi_split(_split2(t), rb) for t, rb in zip(tinvs, rhs_bd)]
    out = []
    for (a, b), sol, qk in zip(pairs, sols, qks):
        for slot, i in enumerate((a, b)):
            s_ = sol[:, slot * (dk + dv):(slot + 1) * (dk + dv)]
            out.append((s_[:, :dk], s_[:, dk:], qs[i] * egcs[i], ks[i] * jnp.exp(gls[i] - gcs[i]),
                        jnp.exp(gls[i]), qk, slot))
    return out


def _qk_apply(qk_pair, slot, v_new):
    z = jnp.zeros(v_new.shape, BF16)
    v2 = [v_new.astype(BF16), z] if slot == 0 else [z, v_new.astype(BF16)]
    return _dot(qk_pair.astype(BF16), jnp.concatenate(v2, axis=0))


def _gated_out_norm(o, w, z):
    return (_rms(o, w) * _silu(z)).astype(BF16)


def _head_cols(xb, h, heads, dk, dv):
    q = xb[:, h * dk:(h + 1) * dk]
    k = xb[:, (heads + h) * dk:(heads + h + 1) * dk]
    v = xb[:, 2 * heads * dk + h * dv:2 * heads * dk + (h + 1) * dv]
    return q, k, v


def _delta_prompt_kernel(x_ref, z_ref, gb_ref, cw_ref, nw_ref, o_ref, s_out_ref, xc_ref, cv_ref, s_ref, *,
                         heads, dk, dv, conv_w, chunk):
    n = pl.program_id(1)
    rows = x_ref.shape[0]
    pad = SUBLANES

    @pl.when(n == 0)
    def _():
        xc_ref[0:pad, :] = jnp.zeros((pad, xc_ref.shape[1]), F32)
        s_ref[...] = jnp.zeros(s_ref.shape, F32)

    xc_ref[pad:pad + rows, :] = x_ref[...]
    acc = None
    for jj in range(conv_w):
        off = pad - (conv_w - 1) + jj
        term = xc_ref[off:off + rows, :] * cw_ref[jj:jj + 1, :]
        acc = term if acc is None else acc + term
    cv_ref[...] = acc
    xb = _silu(cv_ref[...])
    xc_ref[0:pad, :] = xc_ref[rows:rows + pad, :]

    gb = gb_ref[...]
    masks = _chunk_masks(chunk, chunk)
    insts = []
    for g in range(rows // chunk):
        rs = slice(g * chunk, (g + 1) * chunk)
        gates = _delta_gates(gb[rs], masks, heads, dk)
        insts.extend(_head_cols(xb[rs], h, heads, dk, dv) + gates[h] for h in range(heads))
    intra = _delta_intra(insts, masks, chunk)

    state = [s_ref[h] for h in range(heads)]
    for g in range(rows // chunk):
        rs = slice(g * chunk, (g + 1) * chunk)
        cur = intra[g * heads:(g + 1) * heads]
        res = [_dot(jnp.concatenate([i[0], i[2]], axis=0).astype(BF16), s.astype(BF16))
               for i, s in zip(cur, state)]
        v_new = [i[1] - r_[:chunk] for i, r_ in zip(cur, res)]
        outs = [r_[chunk:] + _qk_apply(i[5], i[6], vn) for i, r_, vn in zip(cur, res, v_new)]
        state = [s * i[4][0:1, :] + _dot_tn(i[3].astype(BF16), vn.astype(BF16))
                 for i, s, vn in zip(cur, state, v_new)]
        for h in range(heads):
            cs = slice(h * dv, (h + 1) * dv)
            o_ref[rs, cs] = _gated_out_norm(outs[h], nw_ref[...], z_ref[rs, cs])
    for h in range(heads):
        s_ref[h] = state[h]

    @pl.when(n == pl.num_programs(1) - 1)
    def _():
        s_out_ref[0] = s_ref[...]


def _delta_prompt(x, z, gb, conv_w, norm_w, *, batch, heads, dk, dv):
    t, cdim = x.shape
    seq = t // batch
    c = DN_CHUNK if seq % DN_CHUNK == 0 else seq
    assert c % SUBLANES == 0 and c >= conv_w.shape[0] - 1 and dk == dv
    group = PROMPT_CHUNKS_PER_STEP if (seq // c) % PROMPT_CHUNKS_PER_STEP == 0 else 1
    rows = group * c
    steps = seq // rows
    blk = lambda n_: pl.BlockSpec((rows, n_), lambda b, n: (b * steps + n, 0))
    return pl.pallas_call(
        functools.partial(_delta_prompt_kernel, heads=heads, dk=dk, dv=dv, conv_w=conv_w.shape[0],
                          chunk=c),
        grid=(batch, steps),
        in_specs=[blk(cdim), blk(z.shape[1]), blk(gb.shape[1]),
                  _const_spec(conv_w.shape), _const_spec(norm_w.shape)],
        out_specs=[blk(heads * dv), pl.BlockSpec((1, heads, dk, dv), lambda b, n: (b, 0, 0, 0))],
        out_shape=[jax.ShapeDtypeStruct((t, heads * dv), BF16),
                   jax.ShapeDtypeStruct((batch, heads, dk, dv), F32)],
        scratch_shapes=[pltpu.VMEM((rows + SUBLANES, cdim), F32), pltpu.VMEM((rows, cdim), F32),
                        pltpu.VMEM((heads, dk, dv), F32)],
        compiler_params=_params(("parallel", "arbitrary")),
        name="delta_prompt",
    )(x, z, gb, conv_w, norm_w)


def _delta_sample_kernel(x_ref, cbuf_ref, z_ref, gb_ref, s_in_ref, cw_ref, nw_ref, o_ref, s_out_ref,
                         xc_ref, *, heads, dk, dv, conv_w):
    nb, hist, cdim = cbuf_ref.shape
    r = x_ref.shape[0]
    t = r // nb
    xc_ref[:, SUBLANES:SUBLANES + t, :] = x_ref[...].reshape(nb, t, cdim)
    xc_ref[:, SUBLANES - hist:SUBLANES, :] = cbuf_ref[...]
    acc = None
    for jj in range(conv_w):
        off = SUBLANES - (conv_w - 1) + jj
        term = xc_ref[:, off:off + t, :] * cw_ref[jj:jj + 1, :][None]
        acc = term if acc is None else acc + term
    xb = _silu(acc).reshape(r, cdim)

    masks = _chunk_masks(r, t)
    gates = _delta_gates(gb_ref[...], masks, heads, dk)
    intra = _delta_intra([_head_cols(xb, h, heads, dk, dv) + gates[h] for h in range(heads)], masks, t)
    rows = [slice(b * t, (b + 1) * t) for b in range(nb)]
    for h in range(heads):
        k_cum, v_u, q_dec, k_dec, g_tot, qk, slot = intra[h]
        state = [s_in_ref[b, h] for b in range(nb)]
        res = [_dot(jnp.concatenate([k_cum[rs], q_dec[rs]], axis=0).astype(BF16), s.astype(BF16))
               for rs, s in zip(rows, state)]
        v_new = [v_u[rs] - r_[:t] for rs, r_ in zip(rows, res)]
        for b in range(nb):
            s_out_ref[b, h] = (state[b] * g_tot[b * t:b * t + 1, :]
                               + _dot_tn(k_dec[rows[b]].astype(BF16), v_new[b].astype(BF16)))
        o = (jnp.concatenate([r_[t:] for r_ in res], axis=0)
             + _qk_apply(qk, slot, jnp.concatenate(v_new, axis=0)))
        o_ref[:, h * dv:(h + 1) * dv] = _gated_out_norm(o, nw_ref[...], z_ref[:, h * dv:(h + 1) * dv])


def _delta_sample(x, cbuf, z, gb, s0, conv_w, norm_w, *, heads, dk, dv):
    rows, cdim = x.shape
    nbatch, hist, _ = cbuf.shape
    t = rows // nbatch
    assert t == SUBLANES and hist == conv_w.shape[0] - 1 and hist <= t and dk == dv
    nb = SAMPLE_GROUP if nbatch % SAMPLE_GROUP == 0 else 1
    r = nb * t
    blk = lambda n_: pl.BlockSpec((r, n_), lambda i: (i, 0))
    sblk = pl.BlockSpec((nb, heads, dk, dv), lambda i: (i, 0, 0, 0))
    return pl.pallas_call(
        functools.partial(_delta_sample_kernel, heads=heads, dk=dk, dv=dv, conv_w=conv_w.shape[0]),
        grid=(nbatch // nb,),
        in_specs=[blk(cdim), pl.BlockSpec((nb, hist, cdim), lambda i: (i, 0, 0)), blk(z.shape[1]),
                  blk(gb.shape[1]), sblk, _const_spec(conv_w.shape), _const_spec(norm_w.shape)],
        out_specs=[blk(heads * dv), sblk],
        out_shape=[jax.ShapeDtypeStruct((rows, heads * dv), BF16), jax.ShapeDtypeStruct(s0.shape, F32)],
        scratch_shapes=[pltpu.VMEM((nb, 2 * SUBLANES, cdim), F32)],
        compiler_params=_params(("parallel",)),
        name="delta_sample",
    )(x, cbuf, z, gb, s0, conv_w, norm_w)


def _layer(xp, xs, ck, cv, cconv, sdelta, lw, dims):
    (f1_pre, f1_post, f1_up, f1_down, mix_pre, mix_post, w_in, sinks, conv_w, a_log, dt_bias,
     out_norm, w_a, w_b, w_o, f2_pre, f2_post, f2_up, f2_down) = lw
    batch, seq, d = xp.shape
    dbatch, dseq, _ = xs.shape
    kvh, hd = dims["kv_heads"], dims["head_dim"]
    hb, dk, dv = dims["b_heads"], dims["dk"], dims["dv"]
    a_heads = w_a.shape[0] // hd
    cdim = conv_w.shape[1]
    d_ff = f1_down.shape[0]
    bf = lambda a: a.astype(BF16)
    row = lambda a: a.reshape(1, -1).astype(F32)

    sizes = (a_heads * hd, kvh * hd, kvh * hd, cdim, hb * dv, hb, hb, d, d)
    offs = [0]
    for s_ in sizes:
        offs.append(offs[-1] + s_)
    col = lambda i0, i1: w_in[:, offs[i0]:offs[i1]]
    wq, wkv, wqkvb, wz = bf(col(0, 1)), bf(col(1, 3)), bf(col(3, 4)), bf(col(4, 5))
    wgbeta = bf(jnp.pad(col(5, 7), ((0, 0), (0, LANES - 2 * hb))))
    wga, wgb = bf(col(7, 8)), bf(col(8, 9))
    alog = jnp.pad(row(a_log), ((0, 0), (0, LANES - hb)))
    dtb = jnp.pad(row(dt_bias), ((0, 0), (0, LANES - hb)))

    def ffn(x, pre, post, up, down):
        return _ffn(x, row(pre), row(post), bf(up[:, :d_ff]), bf(up[:, d_ff:]), bf(down))

    def inproj(x):
        return _inproj(x, row(mix_pre), wq, wkv, wqkvb, wz, wgbeta, alog, dtb,
                       q_scale=hd ** -0.5, n_heads_b=hb)

    def merge(x, oa, ob):
        return _merge(x, oa, ob, row(mix_pre), row(mix_post), wga, wgb, bf(w_a), bf(w_b), bf(w_o))

    sinks = sinks.astype(F32)
    norm_w = row(out_norm)
    conv_w = conv_w.astype(F32)
    hist = conv_w.shape[0] - 1

    x = ffn(xp.reshape(batch * seq, d), f1_pre, f1_post, f1_up, f1_down)
    q, kv, xb, z, gb = inproj(x)
    oa = _swa_prompt(sinks, q, kv, batch=batch, kv_heads=kvh, head_dim=hd)
    ob, p_state = _delta_prompt(xb, z, gb, conv_w, norm_w, batch=batch, heads=hb, dk=dk, dv=dv)
    x = merge(x, oa, ob)
    yp = ffn(x, f2_pre, f2_post, f2_up, f2_down).reshape(batch, seq, d)
    kv3 = kv.reshape(batch, seq, 2, kvh, hd)
    p_k, p_v = kv3[:, seq - WINDOW:, 0], kv3[:, seq - WINDOW:, 1]
    p_conv = xb.reshape(batch, seq, cdim)[:, seq - hist:]

    x = ffn(xs.reshape(dbatch * dseq, d), f1_pre, f1_post, f1_up, f1_down)
    q, kv, xb, z, gb = inproj(x)
    wb = ck.shape[1]
    oa, s_k, s_v = _swa_sample(sinks, q.reshape(dbatch, dseq, -1), kv.reshape(dbatch, dseq, -1),
                               ck.reshape(dbatch, wb, kvh * hd), cv.reshape(dbatch, wb, kvh * hd),
                               kv_heads=kvh, head_dim=hd)
    ob, s_state = _delta_sample(xb, cconv, z, gb, sdelta, conv_w, norm_w, heads=hb, dk=dk, dv=dv)
    x = merge(x, oa.reshape(dbatch * dseq, -1), ob)
    ys = ffn(x, f2_pre, f2_post, f2_up, f2_down).reshape(dbatch, dseq, d)
    s_k = s_k.reshape(dbatch, wb, kvh, hd)
    s_v = s_v.reshape(dbatch, wb, kvh, hd)
    s_conv = xb.reshape(dbatch, dseq, cdim)[:, dseq - hist:]
    return yp, ys, (p_k, p_v, p_conv, p_state), (s_k, s_v, s_conv, s_state)


def kernel(x_prompt, x_sample, cache_swa_k, cache_swa_v, state_conv, state_delta, ffn1_norm_pre, ffn1_norm_post, ffn1_w_up, ffn1_w_down, mix_norm_pre, mix_norm_post, w_in, attn_sinks, conv_w, dn_a_log, dn_dt_bias, dn_out_norm, w_branch_a, w_branch_b, w_out, ffn2_norm_pre, ffn2_norm_post, ffn2_w_up, ffn2_w_down):
    weights = (ffn1_norm_pre, ffn1_norm_post, ffn1_w_up, ffn1_w_down, mix_norm_pre, mix_norm_post,
               w_in, attn_sinks, conv_w, dn_a_log, dn_dt_bias, dn_out_norm, w_branch_a, w_branch_b,
               w_out, ffn2_norm_pre, ffn2_norm_post, ffn2_w_up, ffn2_w_down)
    depth = w_in.shape[0]
    dims = dict(kv_heads=cache_swa_k.shape[3], head_dim=cache_swa_k.shape[4],
                b_heads=state_delta.shape[2], dk=state_delta.shape[3], dv=state_delta.shape[4])
    yp, ys = x_prompt, x_sample
    p_out, s_out = [], []
    for l in range(depth):
        yp, ys, p_new, s_new = _layer(yp, ys, cache_swa_k[l], cache_swa_v[l], state_conv[l],
                                      state_delta[l], tuple(w[l] for w in weights), dims)
        p_out.append(p_new)
        s_out.append(s_new)
    stack = lambda outs, i: jnp.stack([o[i] for o in outs])
    return (yp, ys,
            stack(p_out, 0), stack(p_out, 1), stack(p_out, 2), stack(p_out, 3),
            stack(s_out, 0), stack(s_out, 1), stack(s_out, 2), stack(s_out, 3))
```

```python
import functools

import jax
import jax.numpy as jnp
from jax import lax
from jax.experimental import pallas as pl
from jax.experimental.pallas import tpu as pltpu

WINDOW = 128
DN_CHUNK = 64
EPS = 1e-6
F32 = jnp.float32
BF16 = jnp.bfloat16
NEG_BIG = -1e30
LANES = 128
SUBLANES = 8
VMEM_LIMIT_BYTES = 56 * 1024 * 1024
FF_CHUNK = 256
SAMPLE_GROUP = 16
SWA_BLOCKS_PER_STEP = 4
PROMPT_CHUNKS_PER_STEP = 4


def _rms(x, w):
    return x * lax.rsqrt(jnp.mean(x * x, axis=-1, keepdims=True) + EPS) * w


def _silu(x):
    return x * jax.nn.sigmoid(x)


def _softplus(x):
    return jnp.maximum(x, 0.0) + jnp.log1p(jnp.exp(-jnp.abs(x)))


def _dot(a, b):
    return jnp.dot(a, b, preferred_element_type=F32)


def _dot_nt(a, b):
    return lax.dot_general(a, b, (((1,), (1,)), ((), ())), preferred_element_type=F32)


def _dot_tn(a, b):
    return lax.dot_general(a, b, (((0,), (0,)), ((), ())), preferred_element_type=F32)


def _split3(b):
    hi = b.astype(BF16)
    r = b - hi.astype(F32)
    mid = r.astype(BF16)
    lo = (r - mid.astype(F32)).astype(BF16)
    return hi, mid, lo


def _dot_sel(sel_bf16, b):
    hi, mid, lo = _split3(b)
    return _dot(sel_bf16, hi) + _dot(sel_bf16, mid) + _dot(sel_bf16, lo)


def _ffn_kernel(x_ref, pre_ref, post_ref, wg_ref, wu_ref, wd_ref, o_ref):
    x = x_ref[...]
    xn = _rms(x, pre_ref[...]).astype(BF16)
    d_ff = wd_ref.shape[0]
    acc = jnp.zeros(x.shape, F32)
    for c in range(d_ff // FF_CHUNK):
        sl = slice(c * FF_CHUNK, (c + 1) * FF_CHUNK)
        g = _dot(xn, wg_ref[:, sl])
        u = _dot(xn, wu_ref[:, sl])
        acc = acc + _dot((_silu(g) * u).astype(BF16), wd_ref[sl, :])
    o_ref[...] = x + 0.5 * _rms(acc, post_ref[...])


def _const_spec(shape):
    nd = len(shape)
    return pl.BlockSpec(shape, lambda *_: (0,) * nd)


def _weight_spec(shape):
    nd = len(shape)
    return pl.BlockSpec(shape, lambda *_: (0,) * nd, pipeline_mode=pl.Buffered(1))


def _row_tile(t):
    for tm in (512, 256, 128, 64, 32, 16, 8):
        if t % tm == 0:
            return tm
    raise ValueError(f"token count {t} is not a multiple of 8")


def _params(sem):
    return pltpu.CompilerParams(dimension_semantics=sem, vmem_limit_bytes=VMEM_LIMIT_BYTES)


def _ffn(x, pre, post, wg, wu, wd):
    t, d = x.shape
    f = wd.shape[0]
    assert f % FF_CHUNK == 0
    tm = _row_tile(t)
    row = pl.BlockSpec((tm, d), lambda i: (i, 0))
    return pl.pallas_call(
        _ffn_kernel,
        grid=(t // tm,),
        in_specs=[row, _const_spec((1, d)), _const_spec((1, d)),
                  _weight_spec((d, f)), _weight_spec((d, f)), _weight_spec((f, d))],
        out_specs=row,
        out_shape=jax.ShapeDtypeStruct((t, d), F32),
        compiler_params=_params(("parallel",)),
        name="ffn",
    )(x, pre, post, wg, wu, wd)


def _inproj_kernel(x_ref, pre_ref, wq_ref, wkv_ref, wb_ref, wz_ref, wgb_ref, alog_ref, dtb_ref,
                   q_ref, kv_ref, b_ref, z_ref, gb_ref, *, q_scale, n_heads_b):
    h = _rms(x_ref[...], pre_ref[...]).astype(BF16)
    q_ref[...] = (_dot(h, wq_ref[...]) * q_scale).astype(BF16)
    kv_ref[...] = _dot(h, wkv_ref[...])
    b_ref[...] = _dot(h, wb_ref[...])
    z_ref[...] = _dot(h, wz_ref[...])
    ab = _dot(h, wgb_ref[...])
    g = -jnp.exp(alog_ref[...]) * _softplus(ab + dtb_ref[...])
    beta = jax.nn.sigmoid(ab)
    lane = lax.broadcasted_iota(jnp.int32, ab.shape, 1)
    gb_ref[...] = jnp.where(lane < n_heads_b, g, beta)


def _inproj(x, pre, wq, wkv, wb, wz, wgb, alog, dtb, *, q_scale, n_heads_b):
    t, d = x.shape
    tm = _row_tile(t)
    row = lambda n: pl.BlockSpec((tm, n), lambda i: (i, 0))
    outs = [(wq.shape[1], BF16), (wkv.shape[1], F32), (wb.shape[1], F32), (wz.shape[1], F32),
            (wgb.shape[1], F32)]
    return pl.pallas_call(
        functools.partial(_inproj_kernel, q_scale=q_scale, n_heads_b=n_heads_b),
        grid=(t // tm,),
        in_specs=[row(d), _const_spec((1, d)),
                  _weight_spec(wq.shape), _weight_spec(wkv.shape), _weight_spec(wb.shape),
                  _weight_spec(wz.shape), _weight_spec(wgb.shape),
                  _const_spec(alog.shape), _const_spec(dtb.shape)],
        out_specs=[row(n) for n, _ in outs],
        out_shape=[jax.ShapeDtypeStruct((t, n), dt) for n, dt in outs],
        compiler_params=_params(("parallel",)),
        name="inproj",
    )(x, pre, wq, wkv, wb, wz, wgb, alog, dtb)


def _merge_kernel(x_ref, oa_ref, ob_ref, pre_ref, post_ref, wga_ref, wgb_ref, wa_ref, wb_ref,
                  wo_ref, o_ref):
    x = x_ref[...]
    h = _rms(x, pre_ref[...]).astype(BF16)
    ga = jax.nn.sigmoid(_dot(h, wga_ref[...]))
    gb = jax.nn.sigmoid(_dot(h, wgb_ref[...]))
    merged = ga * _dot(oa_ref[...], wa_ref[...]) + gb * _dot(ob_ref[...], wb_ref[...])
    m = _dot(merged.astype(BF16), wo_ref[...])
    o_ref[...] = x + _rms(m, post_ref[...])


def _merge(x, oa, ob, pre, post, wga, wgb, wa, wb, wo):
    t, d = x.shape
    tm = _row_tile(t)
    row = lambda n: pl.BlockSpec((tm, n), lambda i: (i, 0))
    return pl.pallas_call(
        _merge_kernel,
        grid=(t // tm,),
        in_specs=[row(d), row(oa.shape[1]), row(ob.shape[1]), _const_spec((1, d)),
                  _const_spec((1, d)), _weight_spec(wga.shape), _weight_spec(wgb.shape),
                  _weight_spec(wa.shape), _weight_spec(wb.shape), _weight_spec(wo.shape)],
        out_specs=row(d),
        out_shape=jax.ShapeDtypeStruct((t, d), F32),
        compiler_params=_params(("parallel",)),
        name="merge",
    )(x, oa, ob, pre, post, wga, wgb, wa, wb, wo)


def _sink_softmax(s, mask, sink):
    s = jnp.where(mask, s, NEG_BIG)
    m = jnp.maximum(jnp.max(s, axis=-1, keepdims=True), sink)
    e = jnp.exp(s - m)
    return e / (jnp.sum(e, axis=-1, keepdims=True) + jnp.exp(sink - m))


def _swa_prompt_kernel(sink_ref, q_ref, kvp_ref, kvc_ref, o_ref, *, kv_heads, group, head_dim):
    n = pl.program_id(1)
    w = WINDOW
    blocks = q_ref.shape[0] // w
    dkv = kv_heads * head_dim
    pairs = group // 2
    kv_all = jnp.concatenate([kvp_ref[...], kvc_ref[...]], axis=0)
    lo = lax.broadcasted_iota(jnp.int32, (kv_all.shape[0], dkv), 1) < head_dim

    def lane_halves(x):
        x_rot = pltpu.roll(x, head_dim, axis=1)
        pick = lambda kv, half: jnp.where(lo if half == 0 else ~lo, x if kv == half else x_rot, 0.0)
        return [[pick(kv, half).astype(BF16) for half in range(2)] for kv in range(kv_heads)]

    k_ext = lane_halves(kv_all[:, :dkv])
    v_ext = lane_halves(kv_all[:, dkv:])

    i = lax.broadcasted_iota(jnp.int32, (w, 2 * w), 0)
    j = lax.broadcasted_iota(jnp.int32, (w, 2 * w), 1)
    bias = jnp.where(j > i, jnp.where(j <= i + w, 0.0, NEG_BIG), NEG_BIG)
    bias_seq_start = jnp.where(j >= w, bias, NEG_BIG)
    lo_out = lax.broadcasted_iota(jnp.int32, (w, 2 * head_dim), 1) < head_dim

    ones_half = [jnp.where(lo if half == 0 else ~lo, 1.0, 0.0).astype(BF16) for half in range(2)]
    v_ext = [[jnp.concatenate([v_ext[kv][half], ones_half[half]], axis=1) for half in range(2)]
             for kv in range(kv_heads)]

    inst = [(t, kv) for t in range(blocks) for kv in range(kv_heads)]
    band = lambda x, t: x[t * w:(t + 2) * w]
    q2 = [jnp.concatenate([q_ref[t * w:(t + 1) * w, (kv * pairs + p) * LANES:(kv * pairs + p + 1) * LANES]
                           for p in range(pairs)], axis=0) for t, kv in inst]
    scores = [[_dot_nt(q_, band(k_ext[kv][half], t)) for half in range(2)]
              for q_, (t, kv) in zip(q2, inst)]
    bias_first = jnp.where(n > 0, bias, bias_seq_start)
    for (t, kv), sc in zip(inst, scores):
        vband = jnp.concatenate([band(v_ext[kv][0], t), band(v_ext[kv][1], t)], axis=0)
        for p in range(pairs):
            exps, sink_terms = [], []
            for half in range(2):
                sink = sink_ref[kv * group + 2 * p + half]
                s = sc[half][p * w:(p + 1) * w] + (bias_first if t == 0 else bias)
                m = jnp.maximum(jnp.max(s, axis=-1, keepdims=True), sink)
                exps.append(jnp.exp(s - m).astype(BF16))
                sink_terms.append(jnp.exp(sink - m))
            o = _dot(jnp.concatenate(exps, axis=1), vband)
            den = o[:, LANES:] + jnp.where(lo_out, sink_terms[0], sink_terms[1])
            cs = slice((kv * pairs + p) * LANES, (kv * pairs + p + 1) * LANES)
            o_ref[t * w:(t + 1) * w, cs] = (o[:, :LANES] / den).astype(o_ref.dtype)


def _swa_prompt(sinks, q, kv, *, batch, kv_heads, head_dim):
    t, dq = q.shape
    seq = t // batch
    assert seq % WINDOW == 0
    group = dq // (kv_heads * head_dim)
    assert 2 * head_dim == LANES and kv_heads == 2 and group % 2 == 0
    blocks = SWA_BLOCKS_PER_STEP if (seq // WINDOW) % SWA_BLOCKS_PER_STEP == 0 else 1
    nb = seq // (blocks * WINDOW)
    rows = blocks * WINDOW
    return pl.pallas_call(
        functools.partial(_swa_prompt_kernel, kv_heads=kv_heads, group=group, head_dim=head_dim),
        grid=(batch, nb),
        in_specs=[pl.BlockSpec(memory_space=pltpu.SMEM),
                  pl.BlockSpec((rows, dq), lambda b, n: (b * nb + n, 0)),
                  pl.BlockSpec((WINDOW, kv.shape[1]),
                               lambda b, n: ((b * nb + n) * blocks - jnp.minimum(n, 1), 0)),
                  pl.BlockSpec((rows, kv.shape[1]), lambda b, n: (b * nb + n, 0))],
        out_specs=pl.BlockSpec((rows, dq), lambda b, n: (b * nb + n, 0)),
        out_shape=jax.ShapeDtypeStruct((t, dq), BF16),
        compiler_params=_params(("parallel", "parallel")),
        name="swa_prompt",
    )(sinks, q, kv, kv)


def _swa_sample_kernel(sink_ref, q_ref, kvn_ref, ck_ref, cv_ref, o_ref, nk_ref, nv_ref, *,
                       kv_heads, group, head_dim):
    bb, t, _ = q_ref.shape
    wb = ck_ref.shape[1]
    dkv = kv_heads * head_dim
    q = q_ref[...]
    kc = jnp.concatenate([ck_ref[...], kvn_ref[:, :, :dkv]], axis=1)
    vc = jnp.concatenate([cv_ref[...], kvn_ref[:, :, dkv:]], axis=1)
    nk_ref[...] = kc[:, t:, :]
    nv_ref[...] = vc[:, t:, :]
    kcb = kc.astype(BF16)
    vcb = vc.astype(BF16)
    assert t & (t - 1) == 0
    i = lax.broadcasted_iota(jnp.int32, (group * t, wb + t), 0) & (t - 1)
    j = lax.broadcasted_iota(jnp.int32, (group * t, wb + t), 1)
    diff = wb + i - j
    mask = ((diff >= 0) & (diff < WINDOW))[None]
    outs = []
    for kv in range(kv_heads):
        hs = slice(kv * head_dim, (kv + 1) * head_dim)
        heads = [kv * group + g for g in range(group)]
        qs = jnp.concatenate([q[:, :, h * head_dim:(h + 1) * head_dim] for h in heads], axis=1)
        sink = jnp.concatenate([jnp.full((1, t, 1), sink_ref[h], F32) for h in heads], axis=1)
        s = jnp.einsum('bqd,bkd->bqk', qs, kcb[:, :, hs], preferred_element_type=F32)
        p = _sink_softmax(s, mask, sink)
        o = jnp.einsum('bqk,bkd->bqd', p.astype(BF16), vcb[:, :, hs], preferred_element_type=F32)
        outs.extend(o[:, g * t:(g + 1) * t, :] for g in range(group))
    o_ref[...] = jnp.concatenate(outs, axis=2).astype(o_ref.dtype)


def _swa_sample(sinks, q, kvn, ck, cv, *, kv_heads, head_dim):
    nbatch, t, dq = q.shape
    wb = ck.shape[1]
    group = dq // (kv_heads * head_dim)
    bb = SUBLANES if nbatch % SUBLANES == 0 else 1
    blk = lambda a: pl.BlockSpec((bb,) + a.shape[1:], lambda i: (i, 0, 0))
    return pl.pallas_call(
        functools.partial(_swa_sample_kernel, kv_heads=kv_heads, group=group, head_dim=head_dim),
        grid=(nbatch // bb,),
        in_specs=[pl.BlockSpec(memory_space=pltpu.SMEM), blk(q), blk(kvn), blk(ck), blk(cv)],
        out_specs=[blk(q), blk(ck), blk(cv)],
        out_shape=[jax.ShapeDtypeStruct(q.shape, BF16), jax.ShapeDtypeStruct(ck.shape, F32),
                   jax.ShapeDtypeStruct(cv.shape, F32)],
        compiler_params=_params(("parallel",)),
        name="swa_sample",
    )(sinks, q, kvn, ck, cv)


def _chunk_masks(r, c):
    assert r & (r - 1) == 0 and c & (c - 1) == 0 and LANES % r == 0
    width = 2 * LANES
    ri = lax.broadcasted_iota(jnp.int32, (r, width), 0)
    cj = lax.broadcasted_iota(jnp.int32, (r, width), 1)
    ci = cj & (r - 1)
    incl, strict = ci <= ri, ci < ri
    if r != c:
        shift = c.bit_length() - 1
        same = (ri >> shift) == (ci >> shift)
        incl, strict = same & incl, same & strict
    f01 = lambda mask: jnp.where(mask, 1.0, 0.0)
    block = cj >> (r.bit_length() - 1)
    first_in_tile = lax.broadcasted_iota(jnp.int32, (r, LANES), 1) < r
    return dict(incl=incl, strict=strict, eye=f01(ri == ci), first_in_tile=first_in_tile,
                keep=[f01(block == x).astype(BF16) for x in range(width // r)],
                incl_sel=f01(incl)[:, :r].astype(BF16),
                same_sel=None if r == c else f01(same)[:, :r].astype(BF16))


def _delta_gates(gb, masks, heads, dk):
    r = gb.shape[0]
    gc_all = _dot_sel(masks["incl_sel"], gb)
    if masks["same_sel"] is None:
        gl_all = jnp.broadcast_to(gc_all[r - 1:r, :], gc_all.shape)
    else:
        gl_all = _dot_sel(masks["same_sel"], gb)
    bc = lambda a, lane: jnp.broadcast_to(a[:, lane:lane + 1], (r, dk))
    return [(bc(gc_all, h), bc(gl_all, h), bc(gb, heads + h)) for h in range(heads)]


def _split2(x):
    hi = x.astype(BF16)
    return hi, (x - hi.astype(F32)).astype(BF16)


def _rows(xs):
    return jnp.concatenate(xs, axis=0)


def _lanes(xs):
    return xs[0] if len(xs) == 1 else jnp.concatenate(xs, axis=1)


def _block_diag(blocks):
    n = len(blocks)
    z = jnp.zeros(blocks[0].shape, BF16)
    return _rows([_lanes([blocks[i] if j == i else z for j in range(n)]) for i in range(n)])


def _delta_intra(insts, masks, c):
    r, dk = insts[0][0].shape
    dv = insts[0][2].shape[1]
    m = LANES // r
    n = 2 * m
    assert dk == LANES and len(insts) % n == 0
    incl, strict, eye, keep = masks["incl"], masks["strict"], masks["eye"], masks["keep"]
    tiles = [list(range(i, i + m)) for i in range(0, len(insts), m)]
    groups = [(2 * i, 2 * i + 1) for i in range(len(tiles) // 2)]
    unit = lambda x: x * lax.rsqrt(jnp.sum(x * x, axis=-1, keepdims=True) + EPS)
    qs = [unit(i[0]) * (dk ** -0.5) for i in insts]
    ks = [unit(i[1]) for i in insts]
    vs, gcs, gls, betas = ([i[j] for i in insts] for j in (2, 3, 4, 5))
    bf = lambda x: x.astype(BF16)

    def lane_tile(xs):
        return xs[0] if m == 1 else jnp.where(masks["first_in_tile"], xs[0], xs[1])

    def block_diag_x(x):
        return _rows([x * keep[i] for i in range(n)])

    g_cols = [_lanes([lane_tile([gcs[i] for i in tiles[t]]) for t in g]) for g in groups]
    g_rows = [jnp.transpose(_rows([gcs[i] for t in g for i in tiles[t]]))[:r, :] for g in groups]
    decays = [jnp.where(incl, jnp.exp(gc - gr), 0.0) for gc, gr in zip(g_cols, g_rows)]
    egcs = [jnp.exp(gc) for gc in gcs]
    kbs = [k * beta for k, beta in zip(ks, betas)]
    kq = [_dot_nt(_rows([_lanes([bf(kbs[i]) for i in t]), _lanes([bf(qs[i]) for i in t])]),
                  _block_diag([bf(ks[i]) for i in t])) for t in tiles]
    ms = [jnp.where(strict, _lanes([kq[t][:r] for t in g]) * decay, 0.0) for g, decay in zip(groups, decays)]
    qks = [_lanes([kq[t][r:] for t in g]) * decay for g, decay in zip(groups, decays)]
    rounds = max((c - 1).bit_length(), 1)
    ps = [-m_ for m_ in ms]
    tinvs = [None] * len(groups)
    for j in range(rounds):
        square = j + 1 < rounds
        if not square and tinvs[0] is None:
            tinvs = [eye + p for p in ps]
            break
        p_hl = [_split2(p) for p in ps]
        w_hi = [block_diag_x(hi) for hi, _ in p_hl]
        w_lo = [block_diag_x(lo) for _, lo in p_hl]
        t_hl = [None if t is None else _split2(t) for t in tinvs]
        lhs_a = [_rows(([hl[0], hl[1]] if square else []) + ([] if t is None else [t[0], t[1]]))
                 for hl, t in zip(p_hl, t_hl)]
        lhs_b = [_rows(([hl[0]] if square else []) + ([] if t is None else [t[0]]))
                 for hl, t in zip(p_hl, t_hl)]
        ra = [_dot(a, w) for a, w in zip(lhs_a, w_hi)]
        rb = [_dot(b, w) for b, w in zip(lhs_b, w_lo)]
        new_ps, new_ts = [], []
        for p, t, a, b in zip(ps, tinvs, ra, rb):
            oa = ob = 0
            if square:
                new_ps.append(a[:r] + a[r:2 * r] + b[:r])
                oa, ob = 2 * r, r
            new_ts.append(eye + p if t is None else t + (a[oa:oa + r] + a[oa + r:oa + 2 * r] + b[ob:ob + r]))
        ps, tinvs = new_ps if square else ps, new_ts
    rhs = [_split2(jnp.concatenate([kb * egc, v * beta], axis=1))
           for kb, egc, v, beta in zip(kbs, egcs, vs, betas)]
    t_hl = [_split2(t) for t in tinvs]
    sols = []
    for ti, t in enumerate(tiles):
        hi, lo = (x[:, (ti % 2) * LANES:(ti % 2 + 1) * LANES] for x in t_hl[ti // 2])
        a = _dot(_rows([hi, lo]), _block_diag([rhs[i][0] for i in t]))
        b = _dot(hi, _block_diag([rhs[i][1] for i in t]))
        sols.append(a[:r] + a[r:] + b)
    out = []
    for ti, t in enumerate(tiles):
        per_inst = []
        for slot, i in enumerate(t):
            s_ = sols[ti][:, slot * (dk + dv):(slot + 1) * (dk + dv)]
            per_inst.append((s_[:, :dk], s_[:, dk:], qs[i] * egcs[i], ks[i] * jnp.exp(gls[i] - gcs[i]),
                             jnp.exp(gls[i])))
        out.append((qks[ti // 2][:, (ti % 2) * LANES:(ti % 2 + 1) * LANES], per_inst))
    return out


def _gated_out_norm(o, w, z):
    return (_rms(o, w) * _silu(z)).astype(BF16)


def _head_cols(xb, h, heads, dk, dv):
    q = xb[:, h * dk:(h + 1) * dk]
    k = xb[:, (heads + h) * dk:(heads + h + 1) * dk]
    v = xb[:, 2 * heads * dk + h * dv:2 * heads * dk + (h + 1) * dv]
    return q, k, v


def _delta_prompt_intra_kernel(x_ref, gb_ref, cw_ref, kq_ref, vu_ref, kd_ref, qk_ref, gt_ref,
                               xc_ref, cv_ref, *, heads, dk, dv, conv_w, chunk):
    n = pl.program_id(1)
    rows = x_ref.shape[0]
    pad = SUBLANES

    @pl.when(n == 0)
    def _():
        xc_ref[0:pad, :] = jnp.zeros((pad, xc_ref.shape[1]), F32)

    xc_ref[pad:pad + rows, :] = x_ref[...]
    acc = None
    for jj in range(conv_w):
        off = pad - (conv_w - 1) + jj
        term = xc_ref[off:off + rows, :] * cw_ref[jj:jj + 1, :]
        acc = term if acc is None else acc + term
    cv_ref[...] = acc
    xb = _silu(cv_ref[...])
    xc_ref[0:pad, :] = xc_ref[rows:rows + pad, :]

    gb = gb_ref[...]
    masks = _chunk_masks(chunk, chunk)
    insts = []
    for g in range(rows // chunk):
        rs = slice(g * chunk, (g + 1) * chunk)
        gates = _delta_gates(gb[rs], masks, heads, dk)
        insts.extend(_head_cols(xb[rs], h, heads, dk, dv) + gates[h] for h in range(heads))
    intra = _delta_intra(insts, masks, chunk)

    tiles_per_chunk = len(intra) // (rows // chunk)
    for ti, (qk, per_inst) in enumerate(intra):
        g, tc = divmod(ti, tiles_per_chunk)
        qk_ref[0, g, tc] = qk.astype(BF16)
        for x, (k_cum, v_u, q_dec, k_dec, g_tot) in enumerate(per_inst):
            h = tc * len(per_inst) + x
            kq_ref[0, g, h] = _rows([k_cum, q_dec]).astype(BF16)
            vu_ref[0, g, h] = v_u
            kd_ref[0, g, h] = k_dec.astype(BF16)
            gt_ref[0, g, h] = g_tot[0:SUBLANES, :]


def _delta_prompt_state_kernel(kq_ref, vu_ref, kd_ref, qk_ref, gt_ref, z_ref, nw_ref, o_ref, s_out_ref,
                               s_ref):
    n = pl.program_id(0)
    batch, _, heads, c, dv = vu_ref.shape
    tiles = qk_ref.shape[2]
    m = heads // tiles

    @pl.when(n == 0)
    def _():
        s_ref[...] = jnp.zeros(s_ref.shape, F32)

    inst = [(b, h) for b in range(batch) for h in range(heads)]
    state = [s_ref[b, h] for b, h in inst]
    res = [_dot(kq_ref[b, 0, h], s.astype(BF16)) for (b, h), s in zip(inst, state)]
    v_new = [vu_ref[b, 0, h] - r_[:c] for (b, h), r_ in zip(inst, res)]
    vn_bf = [vn.astype(BF16) for vn in v_new]
    o_intra = [_dot(qk_ref[b, 0, t], _block_diag(vn_bf[(b * tiles + t) * m:(b * tiles + t + 1) * m]))
               for b in range(batch) for t in range(tiles)]
    for i, (b, h) in enumerate(inst):
        s_ref[b, h] = state[i] * gt_ref[b, 0, h][0:1, :] + _dot_tn(kd_ref[b, 0, h], vn_bf[i])
    for i, (b, h) in enumerate(inst):
        o = res[i][c:] + o_intra[i // m][:, (i % m) * dv:(i % m + 1) * dv]
        cs = slice(h * dv, (h + 1) * dv)
        o_ref[b, 0, :, cs] = _gated_out_norm(o, nw_ref[...], z_ref[b, 0, :, cs])

    @pl.when(n == pl.num_programs(0) - 1)
    def _():
        s_out_ref[...] = s_ref[...]


def _delta_prompt(x, z, gb, conv_w, norm_w, *, batch, heads, dk, dv):
    t, cdim = x.shape
    seq = t // batch
    c = DN_CHUNK if seq % DN_CHUNK == 0 else seq
    assert c % SUBLANES == 0 and c >= conv_w.shape[0] - 1 and dk == dv == LANES
    nc = seq // c
    group = PROMPT_CHUNKS_PER_STEP if nc % PROMPT_CHUNKS_PER_STEP == 0 else 1
    rows = group * c
    steps = seq // rows
    tiles = heads * c // LANES
    blk = lambda n_: pl.BlockSpec((rows, n_), lambda b, n: (b * steps + n, 0))
    per_chunk = lambda per, r_, dt: (
        pl.BlockSpec((1, group, per, r_, LANES), lambda b, n: (b, n, 0, 0, 0)),
        jax.ShapeDtypeStruct((batch, nc, per, r_, LANES), dt))
    outs = [per_chunk(heads, 2 * c, BF16), per_chunk(heads, c, F32), per_chunk(heads, c, BF16),
            per_chunk(tiles, c, BF16), per_chunk(heads, SUBLANES, F32)]
    kq, vu, kd, qk, gt = pl.pallas_call(
        functools.partial(_delta_prompt_intra_kernel, heads=heads, dk=dk, dv=dv, conv_w=conv_w.shape[0],
                          chunk=c),
        grid=(batch, steps),
        in_specs=[blk(cdim), blk(gb.shape[1]), _const_spec(conv_w.shape)],
        out_specs=[o[0] for o in outs],
        out_shape=[o[1] for o in outs],
        scratch_shapes=[pltpu.VMEM((rows + SUBLANES, cdim), F32), pltpu.VMEM((rows, cdim), F32)],
        compiler_params=_params(("parallel", "arbitrary")),
        name="delta_prompt_intra",
    )(x, gb, conv_w)
    all_seq = lambda a: pl.BlockSpec((batch, 1) + a.shape[2:], lambda n: (0, n) + (0,) * (a.ndim - 2))
    z4 = z.reshape(batch, nc, c, heads * dv)
    o, s_out = pl.pallas_call(
        _delta_prompt_state_kernel,
        grid=(nc,),
        in_specs=[all_seq(kq), all_seq(vu), all_seq(kd), all_seq(qk), all_seq(gt), all_seq(z4),
                  _const_spec(norm_w.shape)],
        out_specs=[all_seq(z4), _const_spec((batch, heads, dk, dv))],
        out_shape=[jax.ShapeDtypeStruct(z4.shape, BF16), jax.ShapeDtypeStruct((batch, heads, dk, dv), F32)],
        scratch_shapes=[pltpu.VMEM((batch, heads, dk, dv), F32)],
        compiler_params=_params(("arbitrary",)),
        name="delta_prompt_state",
    )(kq, vu, kd, qk, gt, z4, norm_w)
    return o.reshape(t, heads * dv), s_out


def _delta_sample_kernel(x_ref, cbuf_ref, z_ref, gb_ref, s_in_ref, cw_ref, nw_ref, o_ref, s_out_ref,
                         xc_ref, *, heads, dk, dv, conv_w):
    nb, hist, cdim = cbuf_ref.shape
    r = x_ref.shape[0]
    t = r // nb
    xc_ref[:, SUBLANES:SUBLANES + t, :] = x_ref[...].reshape(nb, t, cdim)
    xc_ref[:, SUBLANES - hist:SUBLANES, :] = cbuf_ref[...]
    acc = None
    for jj in range(conv_w):
        off = SUBLANES - (conv_w - 1) + jj
        term = xc_ref[:, off:off + t, :] * cw_ref[jj:jj + 1, :][None]
        acc = term if acc is None else acc + term
    xb = _silu(acc).reshape(r, cdim)

    masks = _chunk_masks(r, t)
    gates = _delta_gates(gb_ref[...], masks, heads, dk)
    intra = _delta_intra([_head_cols(xb, h, heads, dk, dv) + gates[h] for h in range(heads)], masks, t)
    rows = [slice(b * t, (b + 1) * t) for b in range(nb)]
    for h in range(heads):
        qk, ((k_cum, v_u, q_dec, k_dec, g_tot),) = intra[h]
        state = [s_in_ref[b, h] for b in range(nb)]
        res = [_dot(jnp.concatenate([k_cum[rs], q_dec[rs]], axis=0).astype(BF16), s.astype(BF16))
               for rs, s in zip(rows, state)]
        v_new = [v_u[rs] - r_[:t] for rs, r_ in zip(rows, res)]
        for b in range(nb):
            s_out_ref[b, h] = (state[b] * g_tot[b * t:b * t + 1, :]
                               + _dot_tn(k_dec[rows[b]].astype(BF16), v_new[b].astype(BF16)))
        o = (jnp.concatenate([r_[t:] for r_ in res], axis=0)
             + _dot(qk.astype(BF16), jnp.concatenate(v_new, axis=0).astype(BF16)))
        o_ref[:, h * dv:(h + 1) * dv] = _gated_out_norm(o, nw_ref[...], z_ref[:, h * dv:(h + 1) * dv])


def _delta_sample(x, cbuf, z, gb, s0, conv_w, norm_w, *, heads, dk, dv):
    rows, cdim = x.shape
    nbatch, hist, _ = cbuf.shape
    t = rows // nbatch
    assert t == SUBLANES and hist == conv_w.shape[0] - 1 and hist <= t and dk == dv
    nb = SAMPLE_GROUP if nbatch % SAMPLE_GROUP == 0 else 1
    r = nb * t
    blk = lambda n_: pl.BlockSpec((r, n_), lambda i: (i, 0))
    sblk = pl.BlockSpec((nb, heads, dk, dv), lambda i: (i, 0, 0, 0))
    return pl.pallas_call(
        functools.partial(_delta_sample_kernel, heads=heads, dk=dk, dv=dv, conv_w=conv_w.shape[0]),
        grid=(nbatch // nb,),
        in_specs=[blk(cdim), pl.BlockSpec((nb, hist, cdim), lambda i: (i, 0, 0)), blk(z.shape[1]),
                  blk(gb.shape[1]), sblk, _const_spec(conv_w.shape), _const_spec(norm_w.shape)],
        out_specs=[blk(heads * dv), sblk],
        out_shape=[jax.ShapeDtypeStruct((rows, heads * dv), BF16), jax.ShapeDtypeStruct(s0.shape, F32)],
        scratch_shapes=[pltpu.VMEM((nb, 2 * SUBLANES, cdim), F32)],
        compiler_params=_params(("parallel",)),
        name="delta_sample",
    )(x, cbuf, z, gb, s0, conv_w, norm_w)


def _layer(xp, xs, ck, cv, cconv, sdelta, lw, dims):
    (f1_pre, f1_post, f1_up, f1_down, mix_pre, mix_post, w_in, sinks, conv_w, a_log, dt_bias,
     out_norm, w_a, w_b, w_o, f2_pre, f2_post, f2_up, f2_down) = lw
    batch, seq, d = xp.shape
    dbatch, dseq, _ = xs.shape
    kvh, hd = dims["kv_heads"], dims["head_dim"]
    hb, dk, dv = dims["b_heads"], dims["dk"], dims["dv"]
    a_heads = w_a.shape[0] // hd
    cdim = conv_w.shape[1]
    d_ff = f1_down.shape[0]
    bf = lambda a: a.astype(BF16)
    row = lambda a: a.reshape(1, -1).astype(F32)

    sizes = (a_heads * hd, kvh * hd, kvh * hd, cdim, hb * dv, hb, hb, d, d)
    offs = [0]
    for s_ in sizes:
        offs.append(offs[-1] + s_)
    col = lambda i0, i1: w_in[:, offs[i0]:offs[i1]]
    wq, wkv, wqkvb, wz = bf(col(0, 1)), bf(col(1, 3)), bf(col(3, 4)), bf(col(4, 5))
    wgbeta = bf(jnp.pad(col(5, 7), ((0, 0), (0, LANES - 2 * hb))))
    wga, wgb = bf(col(7, 8)), bf(col(8, 9))
    alog = jnp.pad(row(a_log), ((0, 0), (0, LANES - hb)))
    dtb = jnp.pad(row(dt_bias), ((0, 0), (0, LANES - hb)))

    def ffn(x, pre, post, up, down):
        return _ffn(x, row(pre), row(post), bf(up[:, :d_ff]), bf(up[:, d_ff:]), bf(down))

    def inproj(x):
        return _inproj(x, row(mix_pre), wq, wkv, wqkvb, wz, wgbeta, alog, dtb,
                       q_scale=hd ** -0.5, n_heads_b=hb)

    def merge(x, oa, ob):
        return _merge(x, oa, ob, row(mix_pre), row(mix_post), wga, wgb, bf(w_a), bf(w_b), bf(w_o))

    sinks = sinks.astype(F32)
    norm_w = row(out_norm)
    conv_w = conv_w.astype(F32)
    hist = conv_w.shape[0] - 1

    x = ffn(xp.reshape(batch * seq, d), f1_pre, f1_post, f1_up, f1_down)
    q, kv, xb, z, gb = inproj(x)
    oa = _swa_prompt(sinks, q, kv, batch=batch, kv_heads=kvh, head_dim=hd)
    ob, p_state = _delta_prompt(xb, z, gb, conv_w, norm_w, batch=batch, heads=hb, dk=dk, dv=dv)
    x = merge(x, oa, ob)
    yp = ffn(x, f2_pre, f2_post, f2_up, f2_down).reshape(batch, seq, d)
    kv3 = kv.reshape(batch, seq, 2, kvh, hd)
    p_k, p_v = kv3[:, seq - WINDOW:, 0], kv3[:, seq - WINDOW:, 1]
    p_conv = xb.reshape(batch, seq, cdim)[:, seq - hist:]

    x = ffn(xs.reshape(dbatch * dseq, d), f1_pre, f1_post, f1_up, f1_down)
    q, kv, xb, z, gb = inproj(x)
    wb = ck.shape[1]
    oa, s_k, s_v = _swa_sample(sinks, q.reshape(dbatch, dseq, -1), kv.reshape(dbatch, dseq, -1),
                               ck.reshape(dbatch, wb, kvh * hd), cv.reshape(dbatch, wb, kvh * hd),
                               kv_heads=kvh, head_dim=hd)
    ob, s_state = _delta_sample(xb, cconv, z, gb, sdelta, conv_w, norm_w, heads=hb, dk=dk, dv=dv)
    x = merge(x, oa.reshape(dbatch * dseq, -1), ob)
    ys = ffn(x, f2_pre, f2_post, f2_up, f2_down).reshape(dbatch, dseq, d)
    s_k = s_k.reshape(dbatch, wb, kvh, hd)
    s_v = s_v.reshape(dbatch, wb, kvh, hd)
    s_conv = xb.reshape(dbatch, dseq, cdim)[:, dseq - hist:]
    return yp, ys, (p_k, p_v, p_conv, p_state), (s_k, s_v, s_conv, s_state)


def kernel(x_prompt, x_sample, cache_swa_k, cache_swa_v, state_conv, state_delta, ffn1_norm_pre, ffn1_norm_post, ffn1_w_up, ffn1_w_down, mix_norm_pre, mix_norm_post, w_in, attn_sinks, conv_w, dn_a_log, dn_dt_bias, dn_out_norm, w_branch_a, w_branch_b, w_out, ffn2_norm_pre, ffn2_norm_post, ffn2_w_up, ffn2_w_down):
    weights = (ffn1_norm_pre, ffn1_norm_post, ffn1_w_up, ffn1_w_down, mix_norm_pre, mix_norm_post,
               w_in, attn_sinks, conv_w, dn_a_log, dn_dt_bias, dn_out_norm, w_branch_a, w_branch_b,
               w_out, ffn2_norm_pre, ffn2_norm_post, ffn2_w_up, ffn2_w_down)
    depth = w_in.shape[0]
    dims = dict(kv_heads=cache_swa_k.shape[3], head_dim=cache_swa_k.shape[4],
                b_heads=state_delta.shape[2], dk=state_delta.shape[3], dv=state_delta.shape[4])
    yp, ys = x_prompt, x_sample
    p_out, s_out = [], []
    for l in range(depth):
        yp, ys, p_new, s_new = _layer(yp, ys, cache_swa_k[l], cache_swa_v[l], state_conv[l],
                                      state_delta[l], tuple(w[l] for w in weights), dims)
        p_out.append(p_new)
        s_out.append(s_new)
    stack = lambda outs, i: jnp.stack([o[i] for o in outs])
    return (yp, ys,
            stack(p_out, 0), stack(p_out, 1), stack(p_out, 2), stack(p_out, 3),
            stack(s_out, 0), stack(s_out, 1), stack(s_out, 2), stack(s_out, 3))
```

```python
import functools

import jax
import jax.numpy as jnp
from jax import lax
from jax.experimental import pallas as pl
from jax.experimental.pallas import tpu as pltpu

WINDOW = 128
DN_CHUNK = 64
EPS = 1e-6
F32 = jnp.float32
BF16 = jnp.bfloat16
NEG_BIG = -1e30
LANES = 128
SUBLANES = 8
VMEM_LIMIT_BYTES = 56 * 1024 * 1024
FF_CHUNK = 256
CONV_ROW_BLOCK = 128
SAMPLE_GROUP = 16
SWA_BLOCKS_PER_STEP = 4
PROMPT_CHUNKS_PER_STEP = 4


def _rms(x, w):
    return x * lax.rsqrt(jnp.mean(x * x, axis=-1, keepdims=True) + EPS) * w


def _silu(x):
    return x * jax.nn.sigmoid(x)


def _unit(x):
    return x * lax.rsqrt(jnp.sum(x * x, axis=-1, keepdims=True) + EPS)


def _softplus(x):
    return jnp.maximum(x, 0.0) + jnp.log1p(jnp.exp(-jnp.abs(x)))


def _dot(a, b):
    return jnp.dot(a, b, preferred_element_type=F32)


def _dot_nt(a, b):
    return lax.dot_general(a, b, (((1,), (1,)), ((), ())), preferred_element_type=F32)


def _dot_tn(a, b):
    return lax.dot_general(a, b, (((0,), (0,)), ((), ())), preferred_element_type=F32)


def _split3(b):
    hi = b.astype(BF16)
    r = b - hi.astype(F32)
    mid = r.astype(BF16)
    lo = (r - mid.astype(F32)).astype(BF16)
    return hi, mid, lo


def _dot_sel(sel_bf16, b):
    hi, mid, lo = _split3(b)
    return _dot(sel_bf16, hi) + _dot(sel_bf16, mid) + _dot(sel_bf16, lo)


def _ffn_kernel(x_ref, pre_ref, post_ref, wup_ref, wd_ref, o_ref):
    x = x_ref[...]
    xn = _rms(x, pre_ref[...]).astype(BF16)
    d_ff = wd_ref.shape[0]
    acc = jnp.zeros(x.shape, F32)
    for c in range(d_ff // FF_CHUNK):
        sl = slice(c * FF_CHUNK, (c + 1) * FF_CHUNK)
        g = _dot(xn, wup_ref[:, sl])
        u = _dot(xn, wup_ref[:, d_ff + c * FF_CHUNK:d_ff + (c + 1) * FF_CHUNK])
        acc = acc + _dot((_silu(g) * u).astype(BF16), wd_ref[sl, :])
    o_ref[...] = x + 0.5 * _rms(acc, post_ref[...])


def _const_spec(shape):
    nd = len(shape)
    return pl.BlockSpec(shape, lambda *_: (0,) * nd)


def _weight_spec(shape):
    nd = len(shape)
    return pl.BlockSpec(shape, lambda *_: (0,) * nd, pipeline_mode=pl.Buffered(1))


def _row_tile(t):
    for tm in (512, 256, 128, 64, 32, 16, 8):
        if t % tm == 0:
            return tm
    raise ValueError(f"token count {t} is not a multiple of 8")


def _params(sem):
    return pltpu.CompilerParams(dimension_semantics=sem, vmem_limit_bytes=VMEM_LIMIT_BYTES)


def _ffn(x, pre, post, wup, wd):
    t, d = x.shape
    f = wd.shape[0]
    assert f % FF_CHUNK == 0 and wup.shape == (d, 2 * f)
    tm = _row_tile(t)
    row = pl.BlockSpec((tm, d), lambda i: (i, 0))
    return pl.pallas_call(
        _ffn_kernel,
        grid=(t // tm,),
        in_specs=[row, _const_spec((1, d)), _const_spec((1, d)),
                  _weight_spec((d, 2 * f)), _weight_spec((f, d))],
        out_specs=row,
        out_shape=jax.ShapeDtypeStruct((t, d), F32),
        compiler_params=_params(("parallel",)),
        name="ffn",
    )(x, pre, post, wup, wd)


def _conv_silu_unit(xc_ref, cv_ref, cw_ref, r0, rows, heads, dk):
    taps = cw_ref.shape[0]
    rs = slice(r0, r0 + rows)
    for cb in range(xc_ref.shape[1] // dk):
        cs = slice(cb * dk, (cb + 1) * dk)
        acc = None
        for jj in range(taps):
            off = r0 + SUBLANES - (taps - 1) + jj
            term = xc_ref[off:off + rows, cs] * cw_ref[jj:jj + 1, cs]
            acc = term if acc is None else acc + term
        cv_ref[rs, cs] = acc
        y = _silu(cv_ref[rs, cs])
        if cb < 2 * heads:
            y = _unit(y) * (dk ** -0.5) if cb < heads else _unit(y)
        cv_ref[rs, cs] = y


def _inproj_kernel(x_ref, pre_ref, w_ref, alog_ref, dtb_ref, *rest, q_scale, n_heads_b, offs, conv):
    tm = x_ref.shape[0]
    h = _rms(x_ref[...], pre_ref[...]).astype(BF16)
    if conv is None:
        q_ref, kv_ref, b_ref, z_ref, gb_ref = rest
        sub = tm
    else:
        cw_ref, q_ref, kv_ref, b_ref, z_ref, gb_ref, tail_ref, xc_ref = rest
        seq_rows, dk = conv
        sub = min(tm, CONV_ROW_BLOCK)

        @pl.when((pl.program_id(0) * tm) % seq_rows == 0)
        def _():
            xc_ref[0:SUBLANES, :] = jnp.zeros((SUBLANES, xc_ref.shape[1]), F32)

    for r0 in range(0, tm, sub):
        rs = slice(r0, r0 + sub)
        proj = lambda i: _dot(h[rs], w_ref[:, offs[i]:offs[i + 1]])
        if conv is None:
            b_ref[rs, :] = proj(2)
        else:
            xc_ref[SUBLANES + r0:SUBLANES + r0 + sub, :] = proj(2)
            if r0 > 0:
                _conv_silu_unit(xc_ref, b_ref, cw_ref, r0 - sub, sub, n_heads_b, dk)
        q_ref[rs, :] = (proj(0) * q_scale).astype(BF16)
        kv_ref[rs, :] = proj(1)
        z_ref[rs, :] = proj(3)
        ab = _dot(h[rs], w_ref[:, offs[4]:offs[4] + LANES])
        g = -jnp.exp(alog_ref[...]) * _softplus(ab + dtb_ref[...])
        beta = jax.nn.sigmoid(ab)
        lane = lax.broadcasted_iota(jnp.int32, ab.shape, 1)
        gb_ref[rs, :] = jnp.where(lane < n_heads_b, g, beta)
    if conv is not None:
        _conv_silu_unit(xc_ref, b_ref, cw_ref, tm - sub, sub, n_heads_b, dk)
        tail_ref[0] = xc_ref[tm:tm + SUBLANES, :]
        xc_ref[0:SUBLANES, :] = xc_ref[tm:tm + SUBLANES, :]


def _inproj(x, pre, w, alog, dtb, *, q_scale, n_heads_b, offs, conv_w=None, seq_rows=None, dk=None):
    t, d = x.shape
    tm = _row_tile(t)
    assert all(o % LANES == 0 for o in offs[:5]) and offs[4] + LANES <= w.shape[1]
    row = lambda n: pl.BlockSpec((tm, n), lambda i: (i, 0))
    cdim = offs[3] - offs[2]
    outs = [(offs[1] - offs[0], BF16), (offs[2] - offs[1], F32), (cdim, F32), (offs[4] - offs[3], F32),
            (LANES, F32)]
    in_specs = [row(d), _const_spec((1, d)), _weight_spec(w.shape),
                _const_spec(alog.shape), _const_spec(dtb.shape)]
    out_specs = [row(n) for n, _ in outs]
    out_shape = [jax.ShapeDtypeStruct((t, n), dt) for n, dt in outs]
    args, scratch, conv = [x, pre, w, alog, dtb], [], None
    if conv_w is not None:
        assert seq_rows % tm == 0 and conv_w.shape[0] - 1 <= SUBLANES
        in_specs.append(_const_spec(conv_w.shape))
        args.append(conv_w)
        out_specs.append(pl.BlockSpec((1, SUBLANES, cdim), lambda i: (i, 0, 0)))
        out_shape.append(jax.ShapeDtypeStruct((t // tm, SUBLANES, cdim), F32))
        scratch = [pltpu.VMEM((tm + SUBLANES, cdim), F32)]
        conv = (seq_rows, dk)
    return pl.pallas_call(
        functools.partial(_inproj_kernel, q_scale=q_scale, n_heads_b=n_heads_b, offs=offs, conv=conv),
        grid=(t // tm,),
        in_specs=in_specs,
        out_specs=out_specs,
        out_shape=out_shape,
        scratch_shapes=scratch,
        compiler_params=_params(("arbitrary",)),
        name="inproj",
    )(*args)


def _merge_kernel(x_ref, oa_ref, ob_ref, pre_ref, post_ref, wga_ref, wgb_ref, wa_ref, wb_ref,
                  wo_ref, o_ref):
    x = x_ref[...]
    h = _rms(x, pre_ref[...]).astype(BF16)
    ga = jax.nn.sigmoid(_dot(h, wga_ref[...]))
    gb = jax.nn.sigmoid(_dot(h, wgb_ref[...]))
    merged = ga * _dot(oa_ref[...], wa_ref[...]) + gb * _dot(ob_ref[...], wb_ref[...])
    m = _dot(merged.astype(BF16), wo_ref[...])
    o_ref[...] = x + _rms(m, post_ref[...])


def _merge(x, oa, ob, pre, post, wga, wgb, wa, wb, wo):
    t, d = x.shape
    tm = _row_tile(t)
    row = lambda n: pl.BlockSpec((tm, n), lambda i: (i, 0))
    return pl.pallas_call(
        _merge_kernel,
        grid=(t // tm,),
        in_specs=[row(d), row(oa.shape[1]), row(ob.shape[1]), _const_spec((1, d)),
                  _const_spec((1, d)), _weight_spec(wga.shape), _weight_spec(wgb.shape),
                  _weight_spec(wa.shape), _weight_spec(wb.shape), _weight_spec(wo.shape)],
        out_specs=row(d),
        out_shape=jax.ShapeDtypeStruct((t, d), F32),
        compiler_params=_params(("parallel",)),
        name="merge",
    )(x, oa, ob, pre, post, wga, wgb, wa, wb, wo)


def _sink_softmax(s, mask, sink):
    s = jnp.where(mask, s, NEG_BIG)
    m = jnp.maximum(jnp.max(s, axis=-1, keepdims=True), sink)
    e = jnp.exp(s - m)
    return e / (jnp.sum(e, axis=-1, keepdims=True) + jnp.exp(sink - m))


def _swa_prompt_kernel(sink_ref, q_ref, kvp_ref, kvc_ref, o_ref, kv_last_ref, *, kv_heads, group,
                       head_dim):
    n = pl.program_id(1)
    w = WINDOW
    blocks = q_ref.shape[0] // w
    dkv = kv_heads * head_dim
    pairs = group // 2
    kv_all = jnp.concatenate([kvp_ref[...], kvc_ref[...]], axis=0)

    @pl.when(n == pl.num_programs(1) - 1)
    def _():
        kv_last_ref[0] = kvc_ref[(blocks - 1) * w:blocks * w, :]
    lo = lax.broadcasted_iota(jnp.int32, (kv_all.shape[0], dkv), 1) < head_dim

    def lane_halves(x):
        x_rot = pltpu.roll(x, head_dim, axis=1)
        pick = lambda kv, half: jnp.where(lo if half == 0 else ~lo, x if kv == half else x_rot, 0.0)
        return [[pick(kv, half).astype(BF16) for half in range(2)] for kv in range(kv_heads)]

    k_ext = lane_halves(kv_all[:, :dkv])
    v_ext = lane_halves(kv_all[:, dkv:])

    i = lax.broadcasted_iota(jnp.int32, (w, 2 * w), 0)
    j = lax.broadcasted_iota(jnp.int32, (w, 2 * w), 1)
    bias = jnp.where(j > i, jnp.where(j <= i + w, 0.0, NEG_BIG), NEG_BIG)
    bias_seq_start = jnp.where(j >= w, bias, NEG_BIG)
    lo_out = lax.broadcasted_iota(jnp.int32, (w, 2 * head_dim), 1) < head_dim

    ones_half = [jnp.where(lo if half == 0 else ~lo, 1.0, 0.0).astype(BF16) for half in range(2)]
    v_ext = [[jnp.concatenate([v_ext[kv][half], ones_half[half]], axis=1) for half in range(2)]
             for kv in range(kv_heads)]

    inst = [(t, kv) for t in range(blocks) for kv in range(kv_heads)]
    band = lambda x, t: x[t * w:(t + 2) * w]
    q2 = [jnp.concatenate([q_ref[t * w:(t + 1) * w, (kv * pairs + p) * LANES:(kv * pairs + p + 1) * LANES]
                           for p in range(pairs)], axis=0) for t, kv in inst]
    scores = [[_dot_nt(q_, band(k_ext[kv][half], t)) for half in range(2)]
              for q_, (t, kv) in zip(q2, inst)]
    bias_first = jnp.where(n > 0, bias, bias_seq_start)
    for (t, kv), sc in zip(inst, scores):
        vband = jnp.concatenate([band(v_ext[kv][0], t), band(v_ext[kv][1], t)], axis=0)
        for p in range(pairs):
            exps, sink_terms = [], []
            for half in range(2):
                sink = sink_ref[kv * group + 2 * p + half]
                s = sc[half][p * w:(p + 1) * w] + (bias_first if t == 0 else bias)
                m = jnp.maximum(jnp.max(s, axis=-1, keepdims=True), sink)
                exps.append(jnp.exp(s - m).astype(BF16))
                sink_terms.append(jnp.exp(sink - m))
            o = _dot(jnp.concatenate(exps, axis=1), vband)
            den = o[:, LANES:] + jnp.where(lo_out, sink_terms[0], sink_terms[1])
            cs = slice((kv * pairs + p) * LANES, (kv * pairs + p + 1) * LANES)
            o_ref[t * w:(t + 1) * w, cs] = (o[:, :LANES] / den).astype(o_ref.dtype)


def _swa_prompt(sinks, q, kv, *, batch, kv_heads, head_dim):
    t, dq = q.shape
    seq = t // batch
    assert seq % WINDOW == 0
    group = dq // (kv_heads * head_dim)
    assert 2 * head_dim == LANES and kv_heads == 2 and group % 2 == 0
    blocks = SWA_BLOCKS_PER_STEP if (seq // WINDOW) % SWA_BLOCKS_PER_STEP == 0 else 1
    nb = seq // (blocks * WINDOW)
    rows = blocks * WINDOW
    return pl.pallas_call(
        functools.partial(_swa_prompt_kernel, kv_heads=kv_heads, group=group, head_dim=head_dim),
        grid=(batch, nb),
        in_specs=[pl.BlockSpec(memory_space=pltpu.SMEM),
                  pl.BlockSpec((rows, dq), lambda b, n: (b * nb + n, 0)),
                  pl.BlockSpec((WINDOW, kv.shape[1]),
                               lambda b, n: ((b * nb + n) * blocks - jnp.minimum(n, 1), 0)),
                  pl.BlockSpec((rows, kv.shape[1]), lambda b, n: (b * nb + n, 0))],
        out_specs=[pl.BlockSpec((rows, dq), lambda b, n: (b * nb + n, 0)),
                   pl.BlockSpec((1, WINDOW, kv.shape[1]), lambda b, n: (b, 0, 0))],
        out_shape=[jax.ShapeDtypeStruct((t, dq), BF16),
                   jax.ShapeDtypeStruct((batch, WINDOW, kv.shape[1]), F32)],
        compiler_params=_params(("parallel", "arbitrary")),
        name="swa_prompt",
    )(sinks, q, kv, kv)


def _swa_sample_kernel(sink_ref, q_ref, kvn_ref, ck_ref, cv_ref, o_ref, nk_ref, nv_ref, *,
                       kv_heads, group, head_dim):
    bb, t, _ = q_ref.shape
    wb = ck_ref.shape[1]
    dkv = kv_heads * head_dim
    q = q_ref[...]
    kc = jnp.concatenate([ck_ref[...], kvn_ref[:, :, :dkv]], axis=1)
    vc = jnp.concatenate([cv_ref[...], kvn_ref[:, :, dkv:]], axis=1)
    nk_ref[...] = kc[:, t:, :]
    nv_ref[...] = vc[:, t:, :]
    kcb = kc.astype(BF16)
    vcb = vc.astype(BF16)
    assert t & (t - 1) == 0
    i = lax.broadcasted_iota(jnp.int32, (group * t, wb + t), 0) & (t - 1)
    j = lax.broadcasted_iota(jnp.int32, (group * t, wb + t), 1)
    diff = wb + i - j
    mask = ((diff >= 0) & (diff < WINDOW))[None]
    outs = []
    for kv in range(kv_heads):
        hs = slice(kv * head_dim, (kv + 1) * head_dim)
        heads = [kv * group + g for g in range(group)]
        qs = jnp.concatenate([q[:, :, h * head_dim:(h + 1) * head_dim] for h in heads], axis=1)
        sink = jnp.concatenate([jnp.full((1, t, 1), sink_ref[h], F32) for h in heads], axis=1)
        s = jnp.einsum('bqd,bkd->bqk', qs, kcb[:, :, hs], preferred_element_type=F32)
        p = _sink_softmax(s, mask, sink)
        o = jnp.einsum('bqk,bkd->bqd', p.astype(BF16), vcb[:, :, hs], preferred_element_type=F32)
        outs.extend(o[:, g * t:(g + 1) * t, :] for g in range(group))
    o_ref[...] = jnp.concatenate(outs, axis=2).astype(o_ref.dtype)


def _swa_sample(sinks, q, kvn, ck, cv, *, kv_heads, head_dim):
    nbatch, t, dq = q.shape
    wb = ck.shape[1]
    group = dq // (kv_heads * head_dim)
    bb = SUBLANES if nbatch % SUBLANES == 0 else 1
    blk = lambda a: pl.BlockSpec((bb,) + a.shape[1:], lambda i: (i, 0, 0))
    return pl.pallas_call(
        functools.partial(_swa_sample_kernel, kv_heads=kv_heads, group=group, head_dim=head_dim),
        grid=(nbatch // bb,),
        in_specs=[pl.BlockSpec(memory_space=pltpu.SMEM), blk(q), blk(kvn), blk(ck), blk(cv)],
        out_specs=[blk(q), blk(ck), blk(cv)],
        out_shape=[jax.ShapeDtypeStruct(q.shape, BF16), jax.ShapeDtypeStruct(ck.shape, F32),
                   jax.ShapeDtypeStruct(cv.shape, F32)],
        compiler_params=_params(("parallel",)),
        name="swa_sample",
    )(sinks, q, kvn, ck, cv)


def _chunk_masks(r, c):
    assert r & (r - 1) == 0 and c & (c - 1) == 0 and LANES % r == 0
    width = 2 * LANES
    ri = lax.broadcasted_iota(jnp.int32, (r, width), 0)
    cj = lax.broadcasted_iota(jnp.int32, (r, width), 1)
    ci = cj & (r - 1)
    incl, strict = ci <= ri, ci < ri
    if r != c:
        shift = c.bit_length() - 1
        same = (ri >> shift) == (ci >> shift)
        incl, strict = same & incl, same & strict
    f01 = lambda mask: jnp.where(mask, 1.0, 0.0)
    block = cj >> (r.bit_length() - 1)
    first_in_tile = lax.broadcasted_iota(jnp.int32, (r, LANES), 1) < r
    return dict(incl=incl, strict=strict, eye=f01(ri == ci), first_in_tile=first_in_tile,
                keep=[f01(block == x).astype(BF16) for x in range(width // r)],
                incl_sel=f01(incl)[:, :r].astype(BF16),
                same_sel=None if r == c else f01(same)[:, :r].astype(BF16))


def _delta_gates(gb, masks, heads, dk):
    r = gb.shape[0]
    gc_all = _dot_sel(masks["incl_sel"], gb)
    if masks["same_sel"] is None:
        gl_all = jnp.broadcast_to(gc_all[r - 1:r, :], gc_all.shape)
    else:
        gl_all = _dot_sel(masks["same_sel"], gb)
    bc = lambda a, lane: jnp.broadcast_to(a[:, lane:lane + 1], (r, dk))
    return [(bc(gc_all, h), bc(gl_all, h), bc(gb, heads + h)) for h in range(heads)]


def _split2(x):
    hi = x.astype(BF16)
    return hi, (x - hi.astype(F32)).astype(BF16)


def _rows(xs):
    return jnp.concatenate(xs, axis=0)


def _lanes(xs):
    return xs[0] if len(xs) == 1 else jnp.concatenate(xs, axis=1)


def _block_diag(blocks):
    n = len(blocks)
    z = jnp.zeros(blocks[0].shape, BF16)
    return _rows([_lanes([blocks[i] if j == i else z for j in range(n)]) for i in range(n)])


def _delta_intra(insts, masks, c):
    r, dk = insts[0][0].shape
    dv = insts[0][2].shape[1]
    m = LANES // r
    n = 2 * m
    assert dk == LANES and len(insts) % n == 0
    incl, strict, eye, keep = masks["incl"], masks["strict"], masks["eye"], masks["keep"]
    tiles = [list(range(i, i + m)) for i in range(0, len(insts), m)]
    groups = [(2 * i, 2 * i + 1) for i in range(len(tiles) // 2)]
    qs, ks, vs, gcs, gls, betas = ([i[j] for i in insts] for j in range(6))
    bf = lambda x: x.astype(BF16)

    def lane_tile(xs):
        return xs[0] if m == 1 else jnp.where(masks["first_in_tile"], xs[0], xs[1])

    def block_diag_x(x):
        return _rows([x * keep[i] for i in range(n)])

    g_cols = [_lanes([lane_tile([gcs[i] for i in tiles[t]]) for t in g]) for g in groups]
    g_rows = [jnp.transpose(_rows([gcs[i] for t in g for i in tiles[t]]))[:r, :] for g in groups]
    decays = [jnp.where(incl, jnp.exp(gc - gr), 0.0) for gc, gr in zip(g_cols, g_rows)]
    egcs = [jnp.exp(gc) for gc in gcs]
    kbs = [k * beta for k, beta in zip(ks, betas)]
    kq = [_dot_nt(_rows([_lanes([bf(kbs[i]) for i in t]), _lanes([bf(qs[i]) for i in t])]),
                  _block_diag([bf(ks[i]) for i in t])) for t in tiles]
    ms = [jnp.where(strict, _lanes([kq[t][:r] for t in g]) * decay, 0.0) for g, decay in zip(groups, decays)]
    qks = [_lanes([kq[t][r:] for t in g]) * decay for g, decay in zip(groups, decays)]
    rounds = max((c - 1).bit_length(), 1)
    ps = [-m_ for m_ in ms]
    tinvs = [None] * len(groups)
    for j in range(rounds):
        square = j + 1 < rounds
        if not square and tinvs[0] is None:
            tinvs = [eye + p for p in ps]
            break
        p_hl = [_split2(p) for p in ps]
        w_hi = [block_diag_x(hi) for hi, _ in p_hl]
        w_lo = [block_diag_x(lo) for _, lo in p_hl]
        t_hl = [None if t is None else _split2(t) for t in tinvs]
        lhs_a = [_rows(([hl[0], hl[1]] if square else []) + ([] if t is None else [t[0], t[1]]))
                 for hl, t in zip(p_hl, t_hl)]
        lhs_b = [_rows(([hl[0]] if square else []) + ([] if t is None else [t[0]]))
                 for hl, t in zip(p_hl, t_hl)]
        ra = [_dot(a, w) for a, w in zip(lhs_a, w_hi)]
        rb = [_dot(b, w) for b, w in zip(lhs_b, w_lo)]
        new_ps, new_ts = [], []
        for p, t, a, b in zip(ps, tinvs, ra, rb):
            oa = ob = 0
            if square:
                new_ps.append(a[:r] + a[r:2 * r] + b[:r])
                oa, ob = 2 * r, r
            new_ts.append(eye + p if t is None else t + (a[oa:oa + r] + a[oa + r:oa + 2 * r] + b[ob:ob + r]))
        ps, tinvs = new_ps if square else ps, new_ts
    rhs = [_split2(jnp.concatenate([kb * egc, v * beta], axis=1))
           for kb, egc, v, beta in zip(kbs, egcs, vs, betas)]
    t_hl = [_split2(t) for t in tinvs]
    sols = []
    for ti, t in enumerate(tiles):
        hi, lo = (x[:, (ti % 2) * LANES:(ti % 2 + 1) * LANES] for x in t_hl[ti // 2])
        a = _dot(_rows([hi, lo]), _block_diag([rhs[i][0] for i in t]))
        b = _dot(hi, _block_diag([rhs[i][1] for i in t]))
        sols.append(a[:r] + a[r:] + b)
    out = []
    for ti, t in enumerate(tiles):
        per_inst = []
        for slot, i in enumerate(t):
            s_ = sols[ti][:, slot * (dk + dv):(slot + 1) * (dk + dv)]
            per_inst.append((s_[:, :dk], s_[:, dk:], qs[i] * egcs[i], ks[i] * jnp.exp(gls[i] - gcs[i]),
                             jnp.exp(gls[i])))
        out.append((qks[ti // 2][:, (ti % 2) * LANES:(ti % 2 + 1) * LANES], per_inst))
    return out


def _gated_out_norm(o, w, z):
    return (_rms(o, w) * _silu(z)).astype(BF16)


def _head_cols(xb, h, heads, dk, dv):
    q = xb[:, h * dk:(h + 1) * dk]
    k = xb[:, (heads + h) * dk:(heads + h + 1) * dk]
    v = xb[:, 2 * heads * dk + h * dv:2 * heads * dk + (h + 1) * dv]
    return q, k, v


def _delta_prompt_intra_kernel(x_ref, gb_ref, kq_ref, vu_ref, kd_ref, qk_ref, gt_ref, *,
                               heads, dk, dv, chunk):
    rows = x_ref.shape[0]
    xb = x_ref[...]
    gb = gb_ref[...]
    masks = _chunk_masks(chunk, chunk)
    insts = []
    for g in range(rows // chunk):
        rs = slice(g * chunk, (g + 1) * chunk)
        gates = _delta_gates(gb[rs], masks, heads, dk)
        insts.extend(_head_cols(xb[rs], h, heads, dk, dv) + gates[h] for h in range(heads))
    intra = _delta_intra(insts, masks, chunk)

    tiles_per_chunk = len(intra) // (rows // chunk)
    for ti, (qk, per_inst) in enumerate(intra):
        g, tc = divmod(ti, tiles_per_chunk)
        qk_ref[0, g, tc] = qk.astype(BF16)
        for x, (k_cum, v_u, q_dec, k_dec, g_tot) in enumerate(per_inst):
            h = tc * len(per_inst) + x
            kq_ref[0, g, h] = _rows([k_cum, q_dec]).astype(BF16)
            vu_ref[0, g, h] = v_u
            kd_ref[0, g, h] = k_dec.astype(BF16)
            gt_ref[0, g, h] = g_tot[0:SUBLANES, :]


def _delta_prompt_state_kernel(kq_ref, vu_ref, kd_ref, qk_ref, gt_ref, z_ref, nw_ref, o_ref, s_out_ref,
                               s_ref):
    n = pl.program_id(0)
    batch, _, heads, c, dv = vu_ref.shape
    tiles = qk_ref.shape[2]
    m = heads // tiles

    @pl.when(n == 0)
    def _():
        s_ref[...] = jnp.zeros(s_ref.shape, F32)

    inst = [(b, h) for b in range(batch) for h in range(heads)]
    state = [s_ref[b, h] for b, h in inst]
    res = [_dot(kq_ref[b, 0, h], s.astype(BF16)) for (b, h), s in zip(inst, state)]
    v_new = [vu_ref[b, 0, h] - r_[:c] for (b, h), r_ in zip(inst, res)]
    vn_bf = [vn.astype(BF16) for vn in v_new]
    o_intra = [_dot(qk_ref[b, 0, t], _block_diag(vn_bf[(b * tiles + t) * m:(b * tiles + t + 1) * m]))
               for b in range(batch) for t in range(tiles)]
    for i, (b, h) in enumerate(inst):
        s_ref[b, h] = state[i] * gt_ref[b, 0, h][0:1, :] + _dot_tn(kd_ref[b, 0, h], vn_bf[i])
    for i, (b, h) in enumerate(inst):
        o = res[i][c:] + o_intra[i // m][:, (i % m) * dv:(i % m + 1) * dv]
        cs = slice(h * dv, (h + 1) * dv)
        o_ref[b, 0, :, cs] = _gated_out_norm(o, nw_ref[...], z_ref[b, 0, :, cs])

    @pl.when(n == pl.num_programs(0) - 1)
    def _():
        s_out_ref[...] = s_ref[...]


def _delta_prompt(x, z, gb, norm_w, *, batch, heads, dk, dv):
    t, cdim = x.shape
    seq = t // batch
    c = DN_CHUNK if seq % DN_CHUNK == 0 else seq
    assert c % SUBLANES == 0 and dk == dv == LANES
    nc = seq // c
    group = PROMPT_CHUNKS_PER_STEP if nc % PROMPT_CHUNKS_PER_STEP == 0 else 1
    rows = group * c
    steps = seq // rows
    tiles = heads * c // LANES
    blk = lambda n_: pl.BlockSpec((rows, n_), lambda b, n: (b * steps + n, 0))
    per_chunk = lambda per, r_, dt: (
        pl.BlockSpec((1, group, per, r_, LANES), lambda b, n: (b, n, 0, 0, 0)),
        jax.ShapeDtypeStruct((batch, nc, per, r_, LANES), dt))
    outs = [per_chunk(heads, 2 * c, BF16), per_chunk(heads, c, F32), per_chunk(heads, c, BF16),
            per_chunk(tiles, c, BF16), per_chunk(heads, SUBLANES, F32)]
    kq, vu, kd, qk, gt = pl.pallas_call(
        functools.partial(_delta_prompt_intra_kernel, heads=heads, dk=dk, dv=dv, chunk=c),
        grid=(batch, steps),
        in_specs=[blk(cdim), blk(gb.shape[1])],
        out_specs=[o[0] for o in outs],
        out_shape=[o[1] for o in outs],
        compiler_params=_params(("parallel", "parallel")),
        name="delta_prompt_intra",
    )(x, gb)
    all_seq = lambda a: pl.BlockSpec((batch, 1) + a.shape[2:], lambda n: (0, n) + (0,) * (a.ndim - 2))
    z4 = z.reshape(batch, nc, c, heads * dv)
    o, s_out = pl.pallas_call(
        _delta_prompt_state_kernel,
        grid=(nc,),
        in_specs=[all_seq(kq), all_seq(vu), all_seq(kd), all_seq(qk), all_seq(gt), all_seq(z4),
                  _const_spec(norm_w.shape)],
        out_specs=[all_seq(z4), _const_spec((batch, heads, dk, dv))],
        out_shape=[jax.ShapeDtypeStruct(z4.shape, BF16), jax.ShapeDtypeStruct((batch, heads, dk, dv), F32)],
        scratch_shapes=[pltpu.VMEM((batch, heads, dk, dv), F32)],
        compiler_params=_params(("arbitrary",)),
        name="delta_prompt_state",
    )(kq, vu, kd, qk, gt, z4, norm_w)
    return o.reshape(t, heads * dv), s_out


def _delta_sample_kernel(x_ref, cbuf_ref, z_ref, gb_ref, s_in_ref, cw_ref, nw_ref, o_ref, s_out_ref,
                         xc_ref, *, heads, dk, dv, conv_w):
    nb, hist, cdim = cbuf_ref.shape
    r = x_ref.shape[0]
    t = r // nb
    xc_ref[:, SUBLANES:SUBLANES + t, :] = x_ref[...].reshape(nb, t, cdim)
    xc_ref[:, SUBLANES - hist:SUBLANES, :] = cbuf_ref[...]
    acc = None
    for jj in range(conv_w):
        off = SUBLANES - (conv_w - 1) + jj
        term = xc_ref[:, off:off + t, :] * cw_ref[jj:jj + 1, :][None]
        acc = term if acc is None else acc + term
    xb = _silu(acc).reshape(r, cdim)

    masks = _chunk_masks(r, t)
    gates = _delta_gates(gb_ref[...], masks, heads, dk)
    insts = []
    for h in range(heads):
        q, k, v = _head_cols(xb, h, heads, dk, dv)
        insts.append((_unit(q) * (dk ** -0.5), _unit(k), v) + gates[h])
    intra = _delta_intra(insts, masks, t)
    rows = [slice(b * t, (b + 1) * t) for b in range(nb)]
    for h in range(heads):
        qk, ((k_cum, v_u, q_dec, k_dec, g_tot),) = intra[h]
        state = [s_in_ref[b, h] for b in range(nb)]
        res = [_dot(jnp.concatenate([k_cum[rs], q_dec[rs]], axis=0).astype(BF16), s.astype(BF16))
               for rs, s in zip(rows, state)]
        v_new = [v_u[rs] - r_[:t] for rs, r_ in zip(rows, res)]
        for b in range(nb):
            s_out_ref[b, h] = (state[b] * g_tot[b * t:b * t + 1, :]
                               + _dot_tn(k_dec[rows[b]].astype(BF16), v_new[b].astype(BF16)))
        o = (jnp.concatenate([r_[t:] for r_ in res], axis=0)
             + _dot(qk.astype(BF16), jnp.concatenate(v_new, axis=0).astype(BF16)))
        o_ref[:, h * dv:(h + 1) * dv] = _gated_out_norm(o, nw_ref[...], z_ref[:, h * dv:(h + 1) * dv])


def _delta_sample(x, cbuf, z, gb, s0, conv_w, norm_w, *, heads, dk, dv):
    rows, cdim = x.shape
    nbatch, hist, _ = cbuf.shape
    t = rows // nbatch
    assert t == SUBLANES and hist == conv_w.shape[0] - 1 and hist <= t and dk == dv
    nb = SAMPLE_GROUP if nbatch % SAMPLE_GROUP == 0 else 1
    r = nb * t
    blk = lambda n_: pl.BlockSpec((r, n_), lambda i: (i, 0))
    sblk = pl.BlockSpec((nb, heads, dk, dv), lambda i: (i, 0, 0, 0))
    return pl.pallas_call(
        functools.partial(_delta_sample_kernel, heads=heads, dk=dk, dv=dv, conv_w=conv_w.shape[0]),
        grid=(nbatch // nb,),
        in_specs=[blk(cdim), pl.BlockSpec((nb, hist, cdim), lambda i: (i, 0, 0)), blk(z.shape[1]),
                  blk(gb.shape[1]), sblk, _const_spec(conv_w.shape), _const_spec(norm_w.shape)],
        out_specs=[blk(heads * dv), sblk],
        out_shape=[jax.ShapeDtypeStruct((rows, heads * dv), BF16), jax.ShapeDtypeStruct(s0.shape, F32)],
        scratch_shapes=[pltpu.VMEM((nb, 2 * SUBLANES, cdim), F32)],
        compiler_params=_params(("parallel",)),
        name="delta_sample",
    )(x, cbuf, z, gb, s0, conv_w, norm_w)


def _layer(xp, xs, ck, cv, cconv, sdelta, lw, dims):
    (f1_pre, f1_post, f1_up, f1_down, mix_pre, mix_post, w_in, sinks, conv_w, a_log, dt_bias,
     out_norm, w_a, w_b, w_o, f2_pre, f2_post, f2_up, f2_down) = lw
    batch, seq, d = xp.shape
    dbatch, dseq, _ = xs.shape
    kvh, hd = dims["kv_heads"], dims["head_dim"]
    hb, dk, dv = dims["b_heads"], dims["dk"], dims["dv"]
    a_heads = w_a.shape[0] // hd
    cdim = conv_w.shape[1]
    d_ff = f1_down.shape[0]
    bf = lambda a: a.astype(BF16)
    row = lambda a: a.reshape(1, -1).astype(F32)

    sizes = (a_heads * hd, kvh * hd, kvh * hd, cdim, hb * dv, hb, hb, d, d)
    offs = [0]
    for s_ in sizes:
        offs.append(offs[-1] + s_)
    w_in_bf = bf(w_in)
    wga, wgb = w_in_bf[:, offs[7]:offs[8]], w_in_bf[:, offs[8]:offs[9]]
    proj_offs = (offs[0], offs[1], offs[3], offs[4], offs[5])
    alog = jnp.pad(row(a_log), ((0, 0), (0, LANES - hb)))
    dtb = jnp.pad(row(dt_bias), ((0, 0), (0, LANES - hb)))

    def ffn(x, pre, post, up, down):
        return _ffn(x, row(pre), row(post), bf(up), bf(down))

    def inproj(x, **conv):
        return _inproj(x, row(mix_pre), w_in_bf, alog, dtb, q_scale=hd ** -0.5, n_heads_b=hb,
                       offs=proj_offs, **conv)

    def merge(x, oa, ob):
        return _merge(x, oa, ob, row(mix_pre), row(mix_post), wga, wgb, bf(w_a), bf(w_b), bf(w_o))

    sinks = sinks.astype(F32)
    norm_w = row(out_norm)
    conv_w = conv_w.astype(F32)
    hist = conv_w.shape[0] - 1

    x = ffn(xp.reshape(batch * seq, d), f1_pre, f1_post, f1_up, f1_down)
    q, kv, xb, z, gb, tail = inproj(x, conv_w=conv_w, seq_rows=seq, dk=dk)
    oa, kv_last = _swa_prompt(sinks, q, kv, batch=batch, kv_heads=kvh, head_dim=hd)
    ob, p_state = _delta_prompt(xb, z, gb, norm_w, batch=batch, heads=hb, dk=dk, dv=dv)
    x = merge(x, oa, ob)
    yp = ffn(x, f2_pre, f2_post, f2_up, f2_down).reshape(batch, seq, d)
    kv_last = kv_last.reshape(batch, WINDOW, 2, kvh, hd)
    p_k, p_v = kv_last[:, :, 0], kv_last[:, :, 1]
    p_conv = tail.reshape(batch, -1, SUBLANES, cdim)[:, -1, SUBLANES - hist:]

    x = ffn(xs.reshape(dbatch * dseq, d), f1_pre, f1_post, f1_up, f1_down)
    q, kv, xb, z, gb = inproj(x)
    wb = ck.shape[1]
    oa, s_k, s_v = _swa_sample(sinks, q.reshape(dbatch, dseq, -1), kv.reshape(dbatch, dseq, -1),
                               ck.reshape(dbatch, wb, kvh * hd), cv.reshape(dbatch, wb, kvh * hd),
                               kv_heads=kvh, head_dim=hd)
    ob, s_state = _delta_sample(xb, cconv, z, gb, sdelta, conv_w, norm_w, heads=hb, dk=dk, dv=dv)
    x = merge(x, oa.reshape(dbatch * dseq, -1), ob)
    ys = ffn(x, f2_pre, f2_post, f2_up, f2_down).reshape(dbatch, dseq, d)
    s_k = s_k.reshape(dbatch, wb, kvh, hd)
    s_v = s_v.reshape(dbatch, wb, kvh, hd)
    s_conv = xb.reshape(dbatch, dseq, cdim)[:, dseq - hist:]
    return yp, ys, (p_k, p_v, p_conv, p_state), (s_k, s_v, s_conv, s_state)


def kernel(x_prompt, x_sample, cache_swa_k, cache_swa_v, state_conv, state_delta, ffn1_norm_pre, ffn1_norm_post, ffn1_w_up, ffn1_w_down, mix_norm_pre, mix_norm_post, w_in, attn_sinks, conv_w, dn_a_log, dn_dt_bias, dn_out_norm, w_branch_a, w_branch_b, w_out, ffn2_norm_pre, ffn2_norm_post, ffn2_w_up, ffn2_w_down):
    weights = (ffn1_norm_pre, ffn1_norm_post, ffn1_w_up, ffn1_w_down, mix_norm_pre, mix_norm_post,
               w_in, attn_sinks, conv_w, dn_a_log, dn_dt_bias, dn_out_norm, w_branch_a, w_branch_b,
               w_out, ffn2_norm_pre, ffn2_norm_post, ffn2_w_up, ffn2_w_down)
    depth = w_in.shape[0]
    dims = dict(kv_heads=cache_swa_k.shape[3], head_dim=cache_swa_k.shape[4],
                b_heads=state_delta.shape[2], dk=state_delta.shape[3], dv=state_delta.shape[4])
    yp, ys = x_prompt, x_sample
    p_out, s_out = [], []
    for l in range(depth):
        yp, ys, p_new, s_new = _layer(yp, ys, cache_swa_k[l], cache_swa_v[l], state_conv[l],
                                      state_delta[l], tuple(w[l] for w in weights), dims)
        p_out.append(p_new)
        s_out.append(s_new)
    stack = lambda outs, i: jnp.stack([o[i] for o in outs])
    return (yp, ys,
            stack(p_out, 0), stack(p_out, 1), stack(p_out, 2), stack(p_out, 3),
            stack(s_out, 0), stack(s_out, 1), stack(s_out, 2), stack(s_out, 3))
```

```python
import functools

import jax
import jax.numpy as jnp
from jax import lax
from jax.experimental import pallas as pl
from jax.experimental.pallas import tpu as pltpu

WINDOW = 128
DN_CHUNK = 64
EPS = 1e-6
F32 = jnp.float32
BF16 = jnp.bfloat16
NEG_BIG = -1e30
LANES = 128
SUBLANES = 8
VMEM_LIMIT_BYTES = 56 * 1024 * 1024
FF_CHUNK = 256
MXU_COLS = 256
CONV_ROW_BLOCK = 128
SAMPLE_GROUP = 16
SWA_BLOCKS_PER_STEP = 4
PROMPT_CHUNKS_PER_STEP = 4


def _rms(x, w):
    return x * lax.rsqrt(jnp.mean(x * x, axis=-1, keepdims=True) + EPS) * w


def _silu(x):
    return x * jax.nn.sigmoid(x)


def _unit(x):
    return x * lax.rsqrt(jnp.sum(x * x, axis=-1, keepdims=True) + EPS)


def _softplus(x):
    return jnp.maximum(x, 0.0) + jnp.log1p(jnp.exp(-jnp.abs(x)))


def _dot(a, b):
    return jnp.dot(a, b, preferred_element_type=F32)


def _dot_nt(a, b):
    return lax.dot_general(a, b, (((1,), (1,)), ((), ())), preferred_element_type=F32)


def _dot_tn(a, b):
    return lax.dot_general(a, b, (((0,), (0,)), ((), ())), preferred_element_type=F32)


def _split3(b):
    hi = b.astype(BF16)
    r = b - hi.astype(F32)
    mid = r.astype(BF16)
    lo = (r - mid.astype(F32)).astype(BF16)
    return hi, mid, lo


def _dot_sel(sel_bf16, b):
    hi, mid, lo = _split3(b)
    return _dot(sel_bf16, hi) + _dot(sel_bf16, mid) + _dot(sel_bf16, lo)


def _ffn_kernel(x_ref, pre_ref, post_ref, wup_ref, wd_ref, o_ref):
    x = x_ref[...]
    xn = _rms(x, pre_ref[...]).astype(BF16)
    d_ff = wd_ref.shape[0]
    acc = jnp.zeros(x.shape, F32)
    for c in range(d_ff // FF_CHUNK):
        sl = slice(c * FF_CHUNK, (c + 1) * FF_CHUNK)
        g = _dot(xn, wup_ref[:, sl])
        u = _dot(xn, wup_ref[:, d_ff + c * FF_CHUNK:d_ff + (c + 1) * FF_CHUNK])
        acc = acc + _dot((_silu(g) * u).astype(BF16), wd_ref[sl, :])
    o_ref[...] = x + 0.5 * _rms(acc, post_ref[...])


def _const_spec(shape):
    nd = len(shape)
    return pl.BlockSpec(shape, lambda *_: (0,) * nd)


def _weight_spec(shape):
    nd = len(shape)
    return pl.BlockSpec(shape, lambda *_: (0,) * nd, pipeline_mode=pl.Buffered(1))


def _row_tile(t):
    for tm in (512, 256, 128, 64, 32, 16, 8):
        if t % tm == 0:
            return tm
    raise ValueError(f"token count {t} is not a multiple of 8")


def _params(sem):
    return pltpu.CompilerParams(dimension_semantics=sem, vmem_limit_bytes=VMEM_LIMIT_BYTES)


def _ffn(x, pre, post, wup, wd):
    t, d = x.shape
    f = wd.shape[0]
    assert f % FF_CHUNK == 0 and wup.shape == (d, 2 * f)
    tm = _row_tile(t)
    row = pl.BlockSpec((tm, d), lambda i: (i, 0))
    return pl.pallas_call(
        _ffn_kernel,
        grid=(t // tm,),
        in_specs=[row, _const_spec((1, d)), _const_spec((1, d)),
                  _weight_spec((d, 2 * f)), _weight_spec((f, d))],
        out_specs=row,
        out_shape=jax.ShapeDtypeStruct((t, d), F32),
        compiler_params=_params(("parallel",)),
        name="ffn",
    )(x, pre, post, wup, wd)


def _conv_silu_unit(xc_ref, cv_ref, cw_ref, r0, rows, cb, heads, dk):
    taps = cw_ref.shape[0]
    rs = slice(r0, r0 + rows)
    cs = slice(cb * dk, (cb + 1) * dk)
    acc = None
    for jj in range(taps):
        off = r0 + SUBLANES - (taps - 1) + jj
        term = xc_ref[off:off + rows, cs] * cw_ref[jj:jj + 1, cs]
        acc = term if acc is None else acc + term
    cv_ref[rs, cs] = acc
    y = _silu(cv_ref[rs, cs])
    if cb < 2 * heads:
        y = _unit(y) * (dk ** -0.5) if cb < heads else _unit(y)
    cv_ref[rs, cs] = y


def _inproj_kernel(x_ref, pre_ref, w_ref, alog_ref, dtb_ref, *rest, q_scale, n_heads_b, offs, conv):
    tm = x_ref.shape[0]
    h = _rms(x_ref[...], pre_ref[...]).astype(BF16)
    if conv is None:
        q_ref, kv_ref, b_ref, z_ref, gb_ref = rest
        sub = tm
    else:
        cw_ref, q_ref, kv_ref, b_ref, z_ref, gb_ref, tail_ref, xc_ref = rest
        seq_rows, dk = conv
        sub = min(tm, CONV_ROW_BLOCK)

        @pl.when((pl.program_id(0) * tm) % seq_rows == 0)
        def _():
            xc_ref[0:SUBLANES, :] = jnp.zeros((SUBLANES, xc_ref.shape[1]), F32)

    def project(r0, i, c0, width):
        rs = slice(r0, r0 + sub)
        cs = slice(c0, c0 + width)
        y = _dot(h[rs], w_ref[:, offs[i] + c0:offs[i] + c0 + width])
        if i == 0:
            q_ref[rs, cs] = (y * q_scale).astype(BF16)
        elif i == 1:
            kv_ref[rs, cs] = y
        elif i == 2 and conv is None:
            b_ref[rs, cs] = y
        elif i == 2:
            xc_ref[SUBLANES + r0:SUBLANES + r0 + sub, cs] = y
        elif i == 3:
            z_ref[rs, cs] = y
        else:
            g = -jnp.exp(alog_ref[...]) * _softplus(y + dtb_ref[...])
            lane = lax.broadcasted_iota(jnp.int32, y.shape, 1)
            gb_ref[rs, :] = jnp.where(lane < n_heads_b, g, jax.nn.sigmoid(y))

    widths = [offs[i + 1] - offs[i] for i in range(4)] + [LANES]
    pieces = [(i, c0, min(MXU_COLS, widths[i] - c0))
              for i in (2, 0, 1, 3, 4) for c0 in range(0, widths[i], MXU_COLS)]
    for r0 in range(0, tm + (sub if conv is not None else 0), sub):
        mm = [functools.partial(project, r0, *p) for p in pieces] if r0 < tm else []
        cv = []
        if conv is not None and r0 > 0:
            cv = [functools.partial(_conv_silu_unit, xc_ref, b_ref, cw_ref, r0 - sub, sub, cb, n_heads_b, dk)
                  for cb in range(xc_ref.shape[1] // dk)]
        for k in range(max(len(mm), len(cv))):
            if k < len(mm):
                mm[k]()
            if k < len(cv):
                cv[k]()
    if conv is not None:
        tail_ref[0] = xc_ref[tm:tm + SUBLANES, :]
        xc_ref[0:SUBLANES, :] = xc_ref[tm:tm + SUBLANES, :]


def _inproj(x, pre, w, alog, dtb, *, q_scale, n_heads_b, offs, conv_w=None, seq_rows=None, dk=None):
    t, d = x.shape
    tm = _row_tile(t)
    assert all(o % LANES == 0 for o in offs[:5]) and offs[4] + LANES <= w.shape[1]
    row = lambda n: pl.BlockSpec((tm, n), lambda i: (i, 0))
    cdim = offs[3] - offs[2]
    outs = [(offs[1] - offs[0], BF16), (offs[2] - offs[1], F32), (cdim, F32), (offs[4] - offs[3], F32),
            (LANES, F32)]
    in_specs = [row(d), _const_spec((1, d)), _weight_spec(w.shape),
                _const_spec(alog.shape), _const_spec(dtb.shape)]
    out_specs = [row(n) for n, _ in outs]
    out_shape = [jax.ShapeDtypeStruct((t, n), dt) for n, dt in outs]
    args, scratch, conv = [x, pre, w, alog, dtb], [], None
    if conv_w is not None:
        assert seq_rows % tm == 0 and conv_w.shape[0] - 1 <= SUBLANES
        in_specs.append(_const_spec(conv_w.shape))
        args.append(conv_w)
        out_specs.append(pl.BlockSpec((1, SUBLANES, cdim), lambda i: (i, 0, 0)))
        out_shape.append(jax.ShapeDtypeStruct((t // tm, SUBLANES, cdim), F32))
        scratch = [pltpu.VMEM((tm + SUBLANES, cdim), F32)]
        conv = (seq_rows, dk)
    return pl.pallas_call(
        functools.partial(_inproj_kernel, q_scale=q_scale, n_heads_b=n_heads_b, offs=offs, conv=conv),
        grid=(t // tm,),
        in_specs=in_specs,
        out_specs=out_specs,
        out_shape=out_shape,
        scratch_shapes=scratch,
        compiler_params=_params(("arbitrary",)),
        name="inproj",
    )(*args)


def _merge_kernel(x_ref, oa_ref, ob_ref, pre_ref, post_ref, wga_ref, wgb_ref, wa_ref, wb_ref,
                  wo_ref, o_ref):
    x = x_ref[...]
    h = _rms(x, pre_ref[...]).astype(BF16)
    ga = jax.nn.sigmoid(_dot(h, wga_ref[...]))
    gb = jax.nn.sigmoid(_dot(h, wgb_ref[...]))
    merged = ga * _dot(oa_ref[...], wa_ref[...]) + gb * _dot(ob_ref[...], wb_ref[...])
    m = _dot(merged.astype(BF16), wo_ref[...])
    o_ref[...] = x + _rms(m, post_ref[...])


def _merge(x, oa, ob, pre, post, wga, wgb, wa, wb, wo):
    t, d = x.shape
    tm = _row_tile(t)
    row = lambda n: pl.BlockSpec((tm, n), lambda i: (i, 0))
    return pl.pallas_call(
        _merge_kernel,
        grid=(t // tm,),
        in_specs=[row(d), row(oa.shape[1]), row(ob.shape[1]), _const_spec((1, d)),
                  _const_spec((1, d)), _weight_spec(wga.shape), _weight_spec(wgb.shape),
                  _weight_spec(wa.shape), _weight_spec(wb.shape), _weight_spec(wo.shape)],
        out_specs=row(d),
        out_shape=jax.ShapeDtypeStruct((t, d), F32),
        compiler_params=_params(("parallel",)),
        name="merge",
    )(x, oa, ob, pre, post, wga, wgb, wa, wb, wo)


def _sink_softmax(s, mask, sink):
    s = jnp.where(mask, s, NEG_BIG)
    m = jnp.maximum(jnp.max(s, axis=-1, keepdims=True), sink)
    e = jnp.exp(s - m)
    return e / (jnp.sum(e, axis=-1, keepdims=True) + jnp.exp(sink - m))


def _swa_prompt_kernel(sink_ref, q_ref, kvp_ref, kvc_ref, o_ref, kv_last_ref, *, kv_heads, group,
                       head_dim):
    n = pl.program_id(1)
    w = WINDOW
    blocks = q_ref.shape[0] // w
    dkv = kv_heads * head_dim
    pairs = group // 2
    kv_all = jnp.concatenate([kvp_ref[...], kvc_ref[...]], axis=0)

    @pl.when(n == pl.num_programs(1) - 1)
    def _():
        kv_last_ref[0] = kvc_ref[(blocks - 1) * w:blocks * w, :]
    lo = lax.broadcasted_iota(jnp.int32, (kv_all.shape[0], dkv), 1) < head_dim

    def lane_halves(x):
        x_rot = pltpu.roll(x, head_dim, axis=1)
        pick = lambda kv, half: jnp.where(lo if half == 0 else ~lo, x if kv == half else x_rot, 0.0)
        return [[pick(kv, half).astype(BF16) for half in range(2)] for kv in range(kv_heads)]

    k_ext = lane_halves(kv_all[:, :dkv])
    v_ext = lane_halves(kv_all[:, dkv:])

    i = lax.broadcasted_iota(jnp.int32, (w, 2 * w), 0)
    j = lax.broadcasted_iota(jnp.int32, (w, 2 * w), 1)
    bias = jnp.where(j > i, jnp.where(j <= i + w, 0.0, NEG_BIG), NEG_BIG)
    bias_seq_start = jnp.where(j >= w, bias, NEG_BIG)
    lo_out = lax.broadcasted_iota(jnp.int32, (w, 2 * head_dim), 1) < head_dim

    ones_half = [jnp.where(lo if half == 0 else ~lo, 1.0, 0.0).astype(BF16) for half in range(2)]
    v_ext = [[jnp.concatenate([v_ext[kv][half], ones_half[half]], axis=1) for half in range(2)]
             for kv in range(kv_heads)]

    inst = [(t, kv) for t in range(blocks) for kv in range(kv_heads)]
    band = lambda x, t: x[t * w:(t + 2) * w]
    q2 = [jnp.concatenate([q_ref[t * w:(t + 1) * w, (kv * pairs + p) * LANES:(kv * pairs + p + 1) * LANES]
                           for p in range(pairs)], axis=0) for t, kv in inst]
    scores = [[_dot_nt(q_, band(k_ext[kv][half], t)) for half in range(2)]
              for q_, (t, kv) in zip(q2, inst)]
    bias_first = jnp.where(n > 0, bias, bias_seq_start)
    for (t, kv), sc in zip(inst, scores):
        vband = jnp.concatenate([band(v_ext[kv][0], t), band(v_ext[kv][1], t)], axis=0)
        for p in range(pairs):
            exps, sink_terms = [], []
            for half in range(2):
                sink = sink_ref[kv * group + 2 * p + half]
                s = sc[half][p * w:(p + 1) * w] + (bias_first if t == 0 else bias)
                m = jnp.maximum(jnp.max(s, axis=-1, keepdims=True), sink)
                exps.append(jnp.exp(s - m).astype(BF16))
                sink_terms.append(jnp.exp(sink - m))
            o = _dot(jnp.concatenate(exps, axis=1), vband)
            den = o[:, LANES:] + jnp.where(lo_out, sink_terms[0], sink_terms[1])
            cs = slice((kv * pairs + p) * LANES, (kv * pairs + p + 1) * LANES)
            o_ref[t * w:(t + 1) * w, cs] = (o[:, :LANES] / den).astype(o_ref.dtype)


def _swa_prompt(sinks, q, kv, *, batch, kv_heads, head_dim):
    t, dq = q.shape
    seq = t // batch
    assert seq % WINDOW == 0
    group = dq // (kv_heads * head_dim)
    assert 2 * head_dim == LANES and kv_heads == 2 and group % 2 == 0
    blocks = SWA_BLOCKS_PER_STEP if (seq // WINDOW) % SWA_BLOCKS_PER_STEP == 0 else 1
    nb = seq // (blocks * WINDOW)
    rows = blocks * WINDOW
    return pl.pallas_call(
        functools.partial(_swa_prompt_kernel, kv_heads=kv_heads, group=group, head_dim=head_dim),
        grid=(batch, nb),
        in_specs=[pl.BlockSpec(memory_space=pltpu.SMEM),
                  pl.BlockSpec((rows, dq), lambda b, n: (b * nb + n, 0)),
                  pl.BlockSpec((WINDOW, kv.shape[1]),
                               lambda b, n: ((b * nb + n) * blocks - jnp.minimum(n, 1), 0)),
                  pl.BlockSpec((rows, kv.shape[1]), lambda b, n: (b * nb + n, 0))],
        out_specs=[pl.BlockSpec((rows, dq), lambda b, n: (b * nb + n, 0)),
                   pl.BlockSpec((1, WINDOW, kv.shape[1]), lambda b, n: (b, 0, 0))],
        out_shape=[jax.ShapeDtypeStruct((t, dq), BF16),
                   jax.ShapeDtypeStruct((batch, WINDOW, kv.shape[1]), F32)],
        compiler_params=_params(("parallel", "arbitrary")),
        name="swa_prompt",
    )(sinks, q, kv, kv)


def _swa_sample_kernel(sink_ref, q_ref, kvn_ref, ck_ref, cv_ref, o_ref, nk_ref, nv_ref, *,
                       kv_heads, group, head_dim):
    bb, t, _ = q_ref.shape
    wb = ck_ref.shape[1]
    dkv = kv_heads * head_dim
    q = q_ref[...]
    kc = jnp.concatenate([ck_ref[...], kvn_ref[:, :, :dkv]], axis=1)
    vc = jnp.concatenate([cv_ref[...], kvn_ref[:, :, dkv:]], axis=1)
    nk_ref[...] = kc[:, t:, :]
    nv_ref[...] = vc[:, t:, :]
    kcb = kc.astype(BF16)
    vcb = vc.astype(BF16)
    assert t & (t - 1) == 0
    i = lax.broadcasted_iota(jnp.int32, (group * t, wb + t), 0) & (t - 1)
    j = lax.broadcasted_iota(jnp.int32, (group * t, wb + t), 1)
    diff = wb + i - j
    mask = ((diff >= 0) & (diff < WINDOW))[None]
    outs = []
    for kv in range(kv_heads):
        hs = slice(kv * head_dim, (kv + 1) * head_dim)
        heads = [kv * group + g for g in range(group)]
        qs = jnp.concatenate([q[:, :, h * head_dim:(h + 1) * head_dim] for h in heads], axis=1)
        sink = jnp.concatenate([jnp.full((1, t, 1), sink_ref[h], F32) for h in heads], axis=1)
        s = jnp.einsum('bqd,bkd->bqk', qs, kcb[:, :, hs], preferred_element_type=F32)
        p = _sink_softmax(s, mask, sink)
        o = jnp.einsum('bqk,bkd->bqd', p.astype(BF16), vcb[:, :, hs], preferred_element_type=F32)
        outs.extend(o[:, g * t:(g + 1) * t, :] for g in range(group))
    o_ref[...] = jnp.concatenate(outs, axis=2).astype(o_ref.dtype)


def _swa_sample(sinks, q, kvn, ck, cv, *, kv_heads, head_dim):
    nbatch, t, dq = q.shape
    wb = ck.shape[1]
    group = dq // (kv_heads * head_dim)
    bb = SUBLANES if nbatch % SUBLANES == 0 else 1
    blk = lambda a: pl.BlockSpec((bb,) + a.shape[1:], lambda i: (i, 0, 0))
    return pl.pallas_call(
        functools.partial(_swa_sample_kernel, kv_heads=kv_heads, group=group, head_dim=head_dim),
        grid=(nbatch // bb,),
        in_specs=[pl.BlockSpec(memory_space=pltpu.SMEM), blk(q), blk(kvn), blk(ck), blk(cv)],
        out_specs=[blk(q), blk(ck), blk(cv)],
        out_shape=[jax.ShapeDtypeStruct(q.shape, BF16), jax.ShapeDtypeStruct(ck.shape, F32),
                   jax.ShapeDtypeStruct(cv.shape, F32)],
        compiler_params=_params(("parallel",)),
        name="swa_sample",
    )(sinks, q, kvn, ck, cv)


def _chunk_masks(r, c):
    assert r & (r - 1) == 0 and c & (c - 1) == 0 and LANES % r == 0
    width = 2 * LANES
    ri = lax.broadcasted_iota(jnp.int32, (r, width), 0)
    cj = lax.broadcasted_iota(jnp.int32, (r, width), 1)
    ci = cj & (r - 1)
    incl, strict = ci <= ri, ci < ri
    if r != c:
        shift = c.bit_length() - 1
        same = (ri >> shift) == (ci >> shift)
        incl, strict = same & incl, same & strict
    f01 = lambda mask: jnp.where(mask, 1.0, 0.0)
    block = cj >> (r.bit_length() - 1)
    first_in_tile = lax.broadcasted_iota(jnp.int32, (r, LANES), 1) < r
    return dict(incl=incl, strict=strict, eye=f01(ri == ci), first_in_tile=first_in_tile,
                keep=[f01(block == x).astype(BF16) for x in range(width // r)],
                incl_sel=f01(incl)[:, :r].astype(BF16),
                same_sel=None if r == c else f01(same)[:, :r].astype(BF16))


def _delta_gates(gb, masks, heads, dk):
    r = gb.shape[0]
    gc_all = _dot_sel(masks["incl_sel"], gb)
    if masks["same_sel"] is None:
        gl_all = jnp.broadcast_to(gc_all[r - 1:r, :], gc_all.shape)
    else:
        gl_all = _dot_sel(masks["same_sel"], gb)
    bc = lambda a, lane: jnp.broadcast_to(a[:, lane:lane + 1], (r, dk))
    return [(bc(gc_all, h), bc(gl_all, h), bc(gb, heads + h)) for h in range(heads)]


def _split2(x):
    hi = x.astype(BF16)
    return hi, (x - hi.astype(F32)).astype(BF16)


def _rows(xs):
    return jnp.concatenate(xs, axis=0)


def _lanes(xs):
    return xs[0] if len(xs) == 1 else jnp.concatenate(xs, axis=1)


def _block_diag(blocks):
    n = len(blocks)
    z = jnp.zeros(blocks[0].shape, BF16)
    return _rows([_lanes([blocks[i] if j == i else z for j in range(n)]) for i in range(n)])


def _delta_intra(insts, masks, c):
    r, dk = insts[0][0].shape
    dv = insts[0][2].shape[1]
    m = LANES // r
    n = 2 * m
    assert dk == LANES and len(insts) % n == 0
    incl, strict, eye, keep = masks["incl"], masks["strict"], masks["eye"], masks["keep"]
    tiles = [list(range(i, i + m)) for i in range(0, len(insts), m)]
    groups = [(2 * i, 2 * i + 1) for i in range(len(tiles) // 2)]
    qs, ks, vs, gcs, gls, betas = ([i[j] for i in insts] for j in range(6))
    bf = lambda x: x.astype(BF16)

    def lane_tile(xs):
        return xs[0] if m == 1 else jnp.where(masks["first_in_tile"], xs[0], xs[1])

    def block_diag_x(x):
        return _rows([x * keep[i] for i in range(n)])

    g_cols = [_lanes([lane_tile([gcs[i] for i in tiles[t]]) for t in g]) for g in groups]
    g_rows = [jnp.transpose(_rows([gcs[i] for t in g for i in tiles[t]]))[:r, :] for g in groups]
    decays = [jnp.where(incl, jnp.exp(gc - gr), 0.0) for gc, gr in zip(g_cols, g_rows)]
    egcs = [jnp.exp(gc) for gc in gcs]
    kbs = [k * beta for k, beta in zip(ks, betas)]
    kq = [_dot_nt(_rows([_lanes([bf(kbs[i]) for i in t]), _lanes([bf(qs[i]) for i in t])]),
                  _block_diag([bf(ks[i]) for i in t])) for t in tiles]
    ms = [jnp.where(strict, _lanes([kq[t][:r] for t in g]) * decay, 0.0) for g, decay in zip(groups, decays)]
    qks = [_lanes([kq[t][r:] for t in g]) * decay for g, decay in zip(groups, decays)]
    rounds = max((c - 1).bit_length(), 1)
    ps = [-m_ for m_ in ms]
    tinvs = [None] * len(groups)
    for j in range(rounds):
        square = j + 1 < rounds
        if not square and tinvs[0] is None:
            tinvs = [eye + p for p in ps]
            break
        p_hl = [_split2(p) for p in ps]
        weights = [block_diag_x(hi) for hi, _ in p_hl]
        lhs = [_rows((list(hl) if square else []) + ([] if t is None else list(_split2(t))))
               for hl, t in zip(p_hl, tinvs)]
        res = [_dot(a, w) for a, w in zip(lhs, weights)]
        new_ps, new_ts = [], []
        for p, t, a in zip(ps, tinvs, res):
            o = 0
            if square:
                new_ps.append(a[:r] + a[r:2 * r])
                o = 2 * r
            new_ts.append(eye + p if t is None else t + (a[o:o + r] + a[o + r:o + 2 * r]))
        ps, tinvs = new_ps if square else ps, new_ts
    rhs = [jnp.concatenate([kb * egc, v * beta], axis=1).astype(BF16)
           for kb, egc, v, beta in zip(kbs, egcs, vs, betas)]
    t_hl = [_split2(t) for t in tinvs]
    sols = []
    for ti, t in enumerate(tiles):
        hi, lo = (x[:, (ti % 2) * LANES:(ti % 2 + 1) * LANES] for x in t_hl[ti // 2])
        a = _dot(_rows([hi, lo]), _block_diag([rhs[i] for i in t]))
        sols.append(a[:r] + a[r:])
    out = []
    for ti, t in enumerate(tiles):
        per_inst = []
        for slot, i in enumerate(t):
            s_ = sols[ti][:, slot * (dk + dv):(slot + 1) * (dk + dv)]
            per_inst.append((s_[:, :dk], s_[:, dk:], qs[i] * egcs[i], ks[i] * jnp.exp(gls[i] - gcs[i]),
                             jnp.exp(gls[i])))
        out.append((qks[ti // 2][:, (ti % 2) * LANES:(ti % 2 + 1) * LANES], per_inst))
    return out


def _gated_out_norm(o, w, z):
    return (_rms(o, w) * _silu(z)).astype(BF16)


def _head_cols(xb, h, heads, dk, dv):
    q = xb[:, h * dk:(h + 1) * dk]
    k = xb[:, (heads + h) * dk:(heads + h + 1) * dk]
    v = xb[:, 2 * heads * dk + h * dv:2 * heads * dk + (h + 1) * dv]
    return q, k, v


def _delta_prompt_intra_kernel(x_ref, gb_ref, kq_ref, vu_ref, kd_ref, qk_ref, gt_ref, *,
                               heads, dk, dv, chunk):
    rows = x_ref.shape[0]
    xb = x_ref[...]
    gb = gb_ref[...]
    masks = _chunk_masks(chunk, chunk)
    insts = []
    for g in range(rows // chunk):
        rs = slice(g * chunk, (g + 1) * chunk)
        gates = _delta_gates(gb[rs], masks, heads, dk)
        insts.extend(_head_cols(xb[rs], h, heads, dk, dv) + gates[h] for h in range(heads))
    intra = _delta_intra(insts, masks, chunk)

    tiles_per_chunk = len(intra) // (rows // chunk)
    for ti, (qk, per_inst) in enumerate(intra):
        g, tc = divmod(ti, tiles_per_chunk)
        qk_ref[0, g, tc] = qk.astype(BF16)
        for x, (k_cum, v_u, q_dec, k_dec, g_tot) in enumerate(per_inst):
            h = tc * len(per_inst) + x
            kq_ref[0, g, h] = _rows([k_cum, q_dec]).astype(BF16)
            vu_ref[0, g, h] = v_u
            kd_ref[0, g, h] = k_dec.astype(BF16)
            gt_ref[0, g, h] = g_tot[0:SUBLANES, :]


def _delta_prompt_state_kernel(kq_ref, vu_ref, kd_ref, qk_ref, gt_ref, z_ref, nw_ref, o_ref, s_out_ref,
                               s_ref):
    n = pl.program_id(0)
    batch, _, heads, c, dv = vu_ref.shape
    tiles = qk_ref.shape[2]
    m = heads // tiles

    @pl.when(n == 0)
    def _():
        s_ref[...] = jnp.zeros(s_ref.shape, F32)

    inst = [(b, h) for b in range(batch) for h in range(heads)]
    state = [s_ref[b, h] for b, h in inst]
    res = [_dot(kq_ref[b, 0, h], s.astype(BF16)) for (b, h), s in zip(inst, state)]
    v_new = [vu_ref[b, 0, h] - r_[:c] for (b, h), r_ in zip(inst, res)]
    vn_bf = [vn.astype(BF16) for vn in v_new]
    o_intra = [_dot(qk_ref[b, 0, t], _block_diag(vn_bf[(b * tiles + t) * m:(b * tiles + t + 1) * m]))
               for b in range(batch) for t in range(tiles)]
    for i, (b, h) in enumerate(inst):
        s_ref[b, h] = state[i] * gt_ref[b, 0, h][0:1, :] + _dot_tn(kd_ref[b, 0, h], vn_bf[i])
    for i, (b, h) in enumerate(inst):
        o = res[i][c:] + o_intra[i // m][:, (i % m) * dv:(i % m + 1) * dv]
        cs = slice(h * dv, (h + 1) * dv)
        o_ref[b, 0, :, cs] = _gated_out_norm(o, nw_ref[...], z_ref[b, 0, :, cs])

    @pl.when(n == pl.num_programs(0) - 1)
    def _():
        s_out_ref[...] = s_ref[...]


def _delta_prompt(x, z, gb, norm_w, *, batch, heads, dk, dv):
    t, cdim = x.shape
    seq = t // batch
    c = DN_CHUNK if seq % DN_CHUNK == 0 else seq
    assert c % SUBLANES == 0 and dk == dv == LANES
    nc = seq // c
    group = PROMPT_CHUNKS_PER_STEP if nc % PROMPT_CHUNKS_PER_STEP == 0 else 1
    rows = group * c
    steps = seq // rows
    tiles = heads * c // LANES
    blk = lambda n_: pl.BlockSpec((rows, n_), lambda b, n: (b * steps + n, 0))
    per_chunk = lambda per, r_, dt: (
        pl.BlockSpec((1, group, per, r_, LANES), lambda b, n: (b, n, 0, 0, 0)),
        jax.ShapeDtypeStruct((batch, nc, per, r_, LANES), dt))
    outs = [per_chunk(heads, 2 * c, BF16), per_chunk(heads, c, F32), per_chunk(heads, c, BF16),
            per_chunk(tiles, c, BF16), per_chunk(heads, SUBLANES, F32)]
    kq, vu, kd, qk, gt = pl.pallas_call(
        functools.partial(_delta_prompt_intra_kernel, heads=heads, dk=dk, dv=dv, chunk=c),
        grid=(batch, steps),
        in_specs=[blk(cdim), blk(gb.shape[1])],
        out_specs=[o[0] for o in outs],
        out_shape=[o[1] for o in outs],
        compiler_params=_params(("parallel", "parallel")),
        name="delta_prompt_intra",
    )(x, gb)
    all_seq = lambda a: pl.BlockSpec((batch, 1) + a.shape[2:], lambda n: (0, n) + (0,) * (a.ndim - 2))
    z4 = z.reshape(batch, nc, c, heads * dv)
    o, s_out = pl.pallas_call(
        _delta_prompt_state_kernel,
        grid=(nc,),
        in_specs=[all_seq(kq), all_seq(vu), all_seq(kd), all_seq(qk), all_seq(gt), all_seq(z4),
                  _const_spec(norm_w.shape)],
        out_specs=[all_seq(z4), _const_spec((batch, heads, dk, dv))],
        out_shape=[jax.ShapeDtypeStruct(z4.shape, BF16), jax.ShapeDtypeStruct((batch, heads, dk, dv), F32)],
        scratch_shapes=[pltpu.VMEM((batch, heads, dk, dv), F32)],
        compiler_params=_params(("arbitrary",)),
        name="delta_prompt_state",
    )(kq, vu, kd, qk, gt, z4, norm_w)
    return o.reshape(t, heads * dv), s_out


def _delta_sample_kernel(x_ref, cbuf_ref, z_ref, gb_ref, s_in_ref, cw_ref, nw_ref, o_ref, s_out_ref,
                         xc_ref, *, heads, dk, dv, conv_w):
    nb, hist, cdim = cbuf_ref.shape
    r = x_ref.shape[0]
    t = r // nb
    xc_ref[:, SUBLANES:SUBLANES + t, :] = x_ref[...].reshape(nb, t, cdim)
    xc_ref[:, SUBLANES - hist:SUBLANES, :] = cbuf_ref[...]
    acc = None
    for jj in range(conv_w):
        off = SUBLANES - (conv_w - 1) + jj
        term = xc_ref[:, off:off + t, :] * cw_ref[jj:jj + 1, :][None]
        acc = term if acc is None else acc + term
    xb = _silu(acc).reshape(r, cdim)

    masks = _chunk_masks(r, t)
    gates = _delta_gates(gb_ref[...], masks, heads, dk)
    insts = []
    for h in range(heads):
        q, k, v = _head_cols(xb, h, heads, dk, dv)
        insts.append((_unit(q) * (dk ** -0.5), _unit(k), v) + gates[h])
    intra = _delta_intra(insts, masks, t)
    rows = [slice(b * t, (b + 1) * t) for b in range(nb)]
    for h in range(heads):
        qk, ((k_cum, v_u, q_dec, k_dec, g_tot),) = intra[h]
        state = [s_in_ref[b, h] for b in range(nb)]
        res = [_dot(jnp.concatenate([k_cum[rs], q_dec[rs]], axis=0).astype(BF16), s.astype(BF16))
               for rs, s in zip(rows, state)]
        v_new = [v_u[rs] - r_[:t] for rs, r_ in zip(rows, res)]
        for b in range(nb):
            s_out_ref[b, h] = (state[b] * g_tot[b * t:b * t + 1, :]
                               + _dot_tn(k_dec[rows[b]].astype(BF16), v_new[b].astype(BF16)))
        o = (jnp.concatenate([r_[t:] for r_ in res], axis=0)
             + _dot(qk.astype(BF16), jnp.concatenate(v_new, axis=0).astype(BF16)))
        o_ref[:, h * dv:(h + 1) * dv] = _gated_out_norm(o, nw_ref[...], z_ref[:, h * dv:(h + 1) * dv])


def _delta_sample(x, cbuf, z, gb, s0, conv_w, norm_w, *, heads, dk, dv):
    rows, cdim = x.shape
    nbatch, hist, _ = cbuf.shape
    t = rows // nbatch
    assert t == SUBLANES and hist == conv_w.shape[0] - 1 and hist <= t and dk == dv
    nb = SAMPLE_GROUP if nbatch % SAMPLE_GROUP == 0 else 1
    r = nb * t
    blk = lambda n_: pl.BlockSpec((r, n_), lambda i: (i, 0))
    sblk = pl.BlockSpec((nb, heads, dk, dv), lambda i: (i, 0, 0, 0))
    return pl.pallas_call(
        functools.partial(_delta_sample_kernel, heads=heads, dk=dk, dv=dv, conv_w=conv_w.shape[0]),
        grid=(nbatch // nb,),
        in_specs=[blk(cdim), pl.BlockSpec((nb, hist, cdim), lambda i: (i, 0, 0)), blk(z.shape[1]),
                  blk(gb.shape[1]), sblk, _const_spec(conv_w.shape), _const_spec(norm_w.shape)],
        out_specs=[blk(heads * dv), sblk],
        out_shape=[jax.ShapeDtypeStruct((rows, heads * dv), BF16), jax.ShapeDtypeStruct(s0.shape, F32)],
        scratch_shapes=[pltpu.VMEM((nb, 2 * SUBLANES, cdim), F32)],
        compiler_params=_params(("parallel",)),
        name="delta_sample",
    )(x, cbuf, z, gb, s0, conv_w, norm_w)


def _layer(xp, xs, ck, cv, cconv, sdelta, lw, dims):
    (f1_pre, f1_post, f1_up, f1_down, mix_pre, mix_post, w_in, sinks, conv_w, a_log, dt_bias,
     out_norm, w_a, w_b, w_o, f2_pre, f2_post, f2_up, f2_down) = lw
    batch, seq, d = xp.shape
    dbatch, dseq, _ = xs.shape
    kvh, hd = dims["kv_heads"], dims["head_dim"]
    hb, dk, dv = dims["b_heads"], dims["dk"], dims["dv"]
    a_heads = w_a.shape[0] // hd
    cdim = conv_w.shape[1]
    d_ff = f1_down.shape[0]
    bf = lambda a: a.astype(BF16)
    row = lambda a: a.reshape(1, -1).astype(F32)

    sizes = (a_heads * hd, kvh * hd, kvh * hd, cdim, hb * dv, hb, hb, d, d)
    offs = [0]
    for s_ in sizes:
        offs.append(offs[-1] + s_)
    w_in_bf = bf(w_in)
    wga, wgb = w_in_bf[:, offs[7]:offs[8]], w_in_bf[:, offs[8]:offs[9]]
    proj_offs = (offs[0], offs[1], offs[3], offs[4], offs[5])
    alog = jnp.pad(row(a_log), ((0, 0), (0, LANES - hb)))
    dtb = jnp.pad(row(dt_bias), ((0, 0), (0, LANES - hb)))

    def ffn(x, pre, post, up, down):
        return _ffn(x, row(pre), row(post), bf(up), bf(down))

    def inproj(x, **conv):
        return _inproj(x, row(mix_pre), w_in_bf, alog, dtb, q_scale=hd ** -0.5, n_heads_b=hb,
                       offs=proj_offs, **conv)

    def merge(x, oa, ob):
        return _merge(x, oa, ob, row(mix_pre), row(mix_post), wga, wgb, bf(w_a), bf(w_b), bf(w_o))

    sinks = sinks.astype(F32)
    norm_w = row(out_norm)
    conv_w = conv_w.astype(F32)
    hist = conv_w.shape[0] - 1

    x = ffn(xp.reshape(batch * seq, d), f1_pre, f1_post, f1_up, f1_down)
    q, kv, xb, z, gb, tail = inproj(x, conv_w=conv_w, seq_rows=seq, dk=dk)
    oa, kv_last = _swa_prompt(sinks, q, kv, batch=batch, kv_heads=kvh, head_dim=hd)
    ob, p_state = _delta_prompt(xb, z, gb, norm_w, batch=batch, heads=hb, dk=dk, dv=dv)
    x = merge(x, oa, ob)
    yp = ffn(x, f2_pre, f2_post, f2_up, f2_down).reshape(batch, seq, d)
    kv_last = kv_last.reshape(batch, WINDOW, 2, kvh, hd)
    p_k, p_v = kv_last[:, :, 0], kv_last[:, :, 1]
    p_conv = tail.reshape(batch, -1, SUBLANES, cdim)[:, -1, SUBLANES - hist:]

    x = ffn(xs.reshape(dbatch * dseq, d), f1_pre, f1_post, f1_up, f1_down)
    q, kv, xb, z, gb = inproj(x)
    wb = ck.shape[1]
    oa, s_k, s_v = _swa_sample(sinks, q.reshape(dbatch, dseq, -1), kv.reshape(dbatch, dseq, -1),
                               ck.reshape(dbatch, wb, kvh * hd), cv.reshape(dbatch, wb, kvh * hd),
                               kv_heads=kvh, head_dim=hd)
    ob, s_state = _delta_sample(xb, cconv, z, gb, sdelta, conv_w, norm_w, heads=hb, dk=dk, dv=dv)
    x = merge(x, oa.reshape(dbatch * dseq, -1), ob)
    ys = ffn(x, f2_pre, f2_post, f2_up, f2_down).reshape(dbatch, dseq, d)
    s_k = s_k.reshape(dbatch, wb, kvh, hd)
    s_v = s_v.reshape(dbatch, wb, kvh, hd)
    s_conv = xb.reshape(dbatch, dseq, cdim)[:, dseq - hist:]
    return yp, ys, (p_k, p_v, p_conv, p_state), (s_k, s_v, s_conv, s_state)


def kernel(x_prompt, x_sample, cache_swa_k, cache_swa_v, state_conv, state_delta, ffn1_norm_pre, ffn1_norm_post, ffn1_w_up, ffn1_w_down, mix_norm_pre, mix_norm_post, w_in, attn_sinks, conv_w, dn_a_log, dn_dt_bias, dn_out_norm, w_branch_a, w_branch_b, w_out, ffn2_norm_pre, ffn2_norm_post, ffn2_w_up, ffn2_w_down):
    weights = (ffn1_norm_pre, ffn1_norm_post, ffn1_w_up, ffn1_w_down, mix_norm_pre, mix_norm_post,
               w_in, attn_sinks, conv_w, dn_a_log, dn_dt_bias, dn_out_norm, w_branch_a, w_branch_b,
               w_out, ffn2_norm_pre, ffn2_norm_post, ffn2_w_up, ffn2_w_down)
    depth = w_in.shape[0]
    dims = dict(kv_heads=cache_swa_k.shape[3], head_dim=cache_swa_k.shape[4],
                b_heads=state_delta.shape[2], dk=state_delta.shape[3], dv=state_delta.shape[4])
    yp, ys = x_prompt, x_sample
    p_out, s_out = [], []
    for l in range(depth):
        yp, ys, p_new, s_new = _layer(yp, ys, cache_swa_k[l], cache_swa_v[l], state_conv[l],
                                      state_delta[l], tuple(w[l] for w in weights), dims)
        p_out.append(p_new)
        s_out.append(s_new)
    stack = lambda outs, i: jnp.stack([o[i] for o in outs])
    return (yp, ys,
            stack(p_out, 0), stack(p_out, 1), stack(p_out, 2), stack(p_out, 3),
            stack(s_out, 0), stack(s_out, 1), stack(s_out, 2), stack(s_out, 3))
```

```python
import functools

import jax
import jax.numpy as jnp
from jax import lax
from jax.experimental import pallas as pl
from jax.experimental.pallas import tpu as pltpu

WINDOW = 128
DN_CHUNK = 64
EPS = 1e-6
F32 = jnp.float32
BF16 = jnp.bfloat16
NEG_BIG = -1e30
LANES = 128
SUBLANES = 8
VMEM_LIMIT_BYTES = 56 * 1024 * 1024
FF_CHUNK = 256
MXU_COLS = 256
CONV_ROW_BLOCK = 128
SAMPLE_GROUP = 16
SWA_BLOCKS_PER_STEP = 4
PROMPT_CHUNKS_PER_STEP = 4


def _rms(x, w):
    return x * lax.rsqrt(jnp.mean(x * x, axis=-1, keepdims=True) + EPS) * w


def _silu(x):
    return x * jax.nn.sigmoid(x)


def _unit(x):
    return x * lax.rsqrt(jnp.sum(x * x, axis=-1, keepdims=True) + EPS)


def _softplus(x):
    return jnp.maximum(x, 0.0) + jnp.log1p(jnp.exp(-jnp.abs(x)))


def _dot(a, b):
    return jnp.dot(a, b, preferred_element_type=F32)


def _dot_nt(a, b):
    return lax.dot_general(a, b, (((1,), (1,)), ((), ())), preferred_element_type=F32)


def _dot_tn(a, b):
    return lax.dot_general(a, b, (((0,), (0,)), ((), ())), preferred_element_type=F32)


def _split3(b):
    hi = b.astype(BF16)
    r = b - hi.astype(F32)
    mid = r.astype(BF16)
    lo = (r - mid.astype(F32)).astype(BF16)
    return hi, mid, lo


def _dot_sel(sel_bf16, b):
    hi, mid, lo = _split3(b)
    return _dot(sel_bf16, hi) + _dot(sel_bf16, mid) + _dot(sel_bf16, lo)


def _ffn_kernel(x_ref, pre_ref, post_ref, wup_ref, wd_ref, o_ref):
    x = x_ref[...]
    xn = _rms(x, pre_ref[...]).astype(BF16)
    d_ff = wd_ref.shape[0]
    acc = jnp.zeros(x.shape, F32)
    for c in range(d_ff // FF_CHUNK):
        sl = slice(c * FF_CHUNK, (c + 1) * FF_CHUNK)
        g = _dot(xn, wup_ref[:, sl])
        u = _dot(xn, wup_ref[:, d_ff + c * FF_CHUNK:d_ff + (c + 1) * FF_CHUNK])
        acc = acc + _dot((_silu(g) * u).astype(BF16), wd_ref[sl, :])
    o_ref[...] = x + 0.5 * _rms(acc, post_ref[...])


def _const_spec(shape):
    nd = len(shape)
    return pl.BlockSpec(shape, lambda *_: (0,) * nd)


def _weight_spec(shape):
    nd = len(shape)
    return pl.BlockSpec(shape, lambda *_: (0,) * nd, pipeline_mode=pl.Buffered(1))


def _row_tile(t, largest=512):
    for tm in (1024, 512, 256, 128, 64, 32, 16, 8):
        if tm <= largest and t % tm == 0:
            return tm
    raise ValueError(f"token count {t} is not a multiple of 8")


def _params(sem):
    return pltpu.CompilerParams(dimension_semantics=sem, vmem_limit_bytes=VMEM_LIMIT_BYTES)


def _ffn(x, pre, post, wup, wd):
    t, d = x.shape
    f = wd.shape[0]
    assert f % FF_CHUNK == 0 and wup.shape == (d, 2 * f)
    tm = _row_tile(t, largest=1024)
    row = pl.BlockSpec((tm, d), lambda i: (i, 0))
    return pl.pallas_call(
        _ffn_kernel,
        grid=(t // tm,),
        in_specs=[row, _const_spec((1, d)), _const_spec((1, d)),
                  _weight_spec((d, 2 * f)), _weight_spec((f, d))],
        out_specs=row,
        out_shape=jax.ShapeDtypeStruct((t, d), F32),
        compiler_params=_params(("parallel",)),
        name="ffn",
    )(x, pre, post, wup, wd)


def _conv_silu_unit(xc_ref, cv_ref, cw_ref, r0, rows, cb, heads, dk):
    taps = cw_ref.shape[0]
    rs = slice(r0, r0 + rows)
    cs = slice(cb * dk, (cb + 1) * dk)
    acc = None
    for jj in range(taps):
        off = r0 + SUBLANES - (taps - 1) + jj
        term = xc_ref[off:off + rows, cs] * cw_ref[jj:jj + 1, cs]
        acc = term if acc is None else acc + term
    cv_ref[rs, cs] = acc
    y = _silu(cv_ref[rs, cs])
    if cb < 2 * heads:
        y = _unit(y) * (dk ** -0.5) if cb < heads else _unit(y)
    cv_ref[rs, cs] = y


def _inproj_kernel(x_ref, pre_ref, w_ref, alog_ref, dtb_ref, *rest, q_scale, n_heads_b, offs, conv):
    tm = x_ref.shape[0]
    h = _rms(x_ref[...], pre_ref[...]).astype(BF16)
    if conv is None:
        q_ref, kv_ref, b_ref, z_ref, gb_ref = rest
        sub = tm
    else:
        cw_ref, q_ref, kv_ref, b_ref, z_ref, gb_ref, tail_ref, xc_ref = rest
        seq_rows, dk = conv
        sub = min(tm, CONV_ROW_BLOCK)

        @pl.when((pl.program_id(0) * tm) % seq_rows == 0)
        def _():
            xc_ref[0:SUBLANES, :] = jnp.zeros((SUBLANES, xc_ref.shape[1]), F32)

    def project(r0, i, c0, width):
        rs = slice(r0, r0 + sub)
        cs = slice(c0, c0 + width)
        y = _dot(h[rs], w_ref[:, offs[i] + c0:offs[i] + c0 + width])
        if i == 0:
            q_ref[rs, cs] = (y * q_scale).astype(BF16)
        elif i == 1:
            kv_ref[rs, cs] = y
        elif i == 2 and conv is None:
            b_ref[rs, cs] = y
        elif i == 2:
            xc_ref[SUBLANES + r0:SUBLANES + r0 + sub, cs] = y
        elif i == 3:
            z_ref[rs, cs] = y
        else:
            g = -jnp.exp(alog_ref[...]) * _softplus(y + dtb_ref[...])
            lane = lax.broadcasted_iota(jnp.int32, y.shape, 1)
            gb_ref[rs, :] = jnp.where(lane < n_heads_b, g, jax.nn.sigmoid(y))

    widths = [offs[i + 1] - offs[i] for i in range(4)] + [LANES]
    pieces = [(i, c0, min(MXU_COLS, widths[i] - c0))
              for i in (2, 0, 1, 3, 4) for c0 in range(0, widths[i], MXU_COLS)]
    for r0 in range(0, tm + (sub if conv is not None else 0), sub):
        mm = [functools.partial(project, r0, *p) for p in pieces] if r0 < tm else []
        cv = []
        if conv is not None and r0 > 0:
            cv = [functools.partial(_conv_silu_unit, xc_ref, b_ref, cw_ref, r0 - sub, sub, cb, n_heads_b, dk)
                  for cb in range(xc_ref.shape[1] // dk)]
        for k in range(max(len(mm), len(cv))):
            if k < len(mm):
                mm[k]()
            if k < len(cv):
                cv[k]()
    if conv is not None:
        tail_ref[0] = xc_ref[tm:tm + SUBLANES, :]
        xc_ref[0:SUBLANES, :] = xc_ref[tm:tm + SUBLANES, :]


def _inproj(x, pre, w, alog, dtb, *, q_scale, n_heads_b, offs, conv_w=None, seq_rows=None, dk=None):
    t, d = x.shape
    tm = _row_tile(t)
    assert all(o % LANES == 0 for o in offs[:5]) and offs[4] + LANES <= w.shape[1]
    row = lambda n: pl.BlockSpec((tm, n), lambda i: (i, 0))
    cdim = offs[3] - offs[2]
    outs = [(offs[1] - offs[0], BF16), (offs[2] - offs[1], F32), (cdim, F32), (offs[4] - offs[3], F32),
            (LANES, F32)]
    in_specs = [row(d), _const_spec((1, d)), _weight_spec(w.shape),
                _const_spec(alog.shape), _const_spec(dtb.shape)]
    out_specs = [row(n) for n, _ in outs]
    out_shape = [jax.ShapeDtypeStruct((t, n), dt) for n, dt in outs]
    args, scratch, conv = [x, pre, w, alog, dtb], [], None
    if conv_w is not None:
        assert seq_rows % tm == 0 and conv_w.shape[0] - 1 <= SUBLANES
        in_specs.append(_const_spec(conv_w.shape))
        args.append(conv_w)
        out_specs.append(pl.BlockSpec((1, SUBLANES, cdim), lambda i: (i, 0, 0)))
        out_shape.append(jax.ShapeDtypeStruct((t // tm, SUBLANES, cdim), F32))
        scratch = [pltpu.VMEM((tm + SUBLANES, cdim), F32)]
        conv = (seq_rows, dk)
    return pl.pallas_call(
        functools.partial(_inproj_kernel, q_scale=q_scale, n_heads_b=n_heads_b, offs=offs, conv=conv),
        grid=(t // tm,),
        in_specs=in_specs,
        out_specs=out_specs,
        out_shape=out_shape,
        scratch_shapes=scratch,
        compiler_params=_params(("arbitrary",)),
        name="inproj",
    )(*args)


def _merge_kernel(x_ref, oa_ref, ob_ref, pre_ref, post_ref, wga_ref, wgb_ref, wa_ref, wb_ref,
                  wo_ref, o_ref):
    x = x_ref[...]
    h = _rms(x, pre_ref[...]).astype(BF16)
    oa, ob = oa_ref[...], ob_ref[...]
    m = jnp.zeros(x.shape, F32)
    for c0 in range(0, wo_ref.shape[0], MXU_COLS):
        cs = slice(c0, c0 + MXU_COLS)
        merged = (jax.nn.sigmoid(_dot(h, wga_ref[:, cs])) * _dot(oa, wa_ref[:, cs])
                  + jax.nn.sigmoid(_dot(h, wgb_ref[:, cs])) * _dot(ob, wb_ref[:, cs]))
        m = m + _dot(merged.astype(BF16), wo_ref[cs, :])
    o_ref[...] = x + _rms(m, post_ref[...])


def _merge(x, oa, ob, pre, post, wga, wgb, wa, wb, wo):
    t, d = x.shape
    tm = _row_tile(t, largest=1024)
    row = lambda n: pl.BlockSpec((tm, n), lambda i: (i, 0))
    return pl.pallas_call(
        _merge_kernel,
        grid=(t // tm,),
        in_specs=[row(d), row(oa.shape[1]), row(ob.shape[1]), _const_spec((1, d)),
                  _const_spec((1, d)), _weight_spec(wga.shape), _weight_spec(wgb.shape),
                  _weight_spec(wa.shape), _weight_spec(wb.shape), _weight_spec(wo.shape)],
        out_specs=row(d),
        out_shape=jax.ShapeDtypeStruct((t, d), F32),
        compiler_params=_params(("parallel",)),
        name="merge",
    )(x, oa, ob, pre, post, wga, wgb, wa, wb, wo)


def _sink_softmax(s, mask, sink):
    s = jnp.where(mask, s, NEG_BIG)
    m = jnp.maximum(jnp.max(s, axis=-1, keepdims=True), sink)
    e = jnp.exp(s - m)
    return e / (jnp.sum(e, axis=-1, keepdims=True) + jnp.exp(sink - m))


def _swa_prompt_kernel(sink_ref, q_ref, kvp_ref, kvc_ref, o_ref, kv_last_ref, *, kv_heads, group,
                       head_dim):
    n = pl.program_id(1)
    w = WINDOW
    blocks = q_ref.shape[0] // w
    dkv = kv_heads * head_dim
    pairs = group // 2
    kv_all = jnp.concatenate([kvp_ref[...], kvc_ref[...]], axis=0)

    @pl.when(n == pl.num_programs(1) - 1)
    def _():
        kv_last_ref[0] = kvc_ref[(blocks - 1) * w:blocks * w, :]
    lo = lax.broadcasted_iota(jnp.int32, (kv_all.shape[0], dkv), 1) < head_dim

    def lane_halves(x):
        x_rot = pltpu.roll(x, head_dim, axis=1)
        pick = lambda kv, half: jnp.where(lo if half == 0 else ~lo, x if kv == half else x_rot, 0.0)
        return [[pick(kv, half).astype(BF16) for half in range(2)] for kv in range(kv_heads)]

    k_ext = lane_halves(kv_all[:, :dkv])
    v_ext = lane_halves(kv_all[:, dkv:])

    i = lax.broadcasted_iota(jnp.int32, (w, 2 * w), 0)
    j = lax.broadcasted_iota(jnp.int32, (w, 2 * w), 1)
    bias = jnp.where(j > i, jnp.where(j <= i + w, 0.0, NEG_BIG), NEG_BIG)
    bias_seq_start = jnp.where(j >= w, bias, NEG_BIG)
    lo_out = lax.broadcasted_iota(jnp.int32, (w, 2 * head_dim), 1) < head_dim

    ones_half = [jnp.where(lo if half == 0 else ~lo, 1.0, 0.0).astype(BF16) for half in range(2)]
    v_ext = [[jnp.concatenate([v_ext[kv][half], ones_half[half]], axis=1) for half in range(2)]
             for kv in range(kv_heads)]

    inst = [(t, kv) for t in range(blocks) for kv in range(kv_heads)]
    band = lambda x, t: x[t * w:(t + 2) * w]
    q2 = [jnp.concatenate([q_ref[t * w:(t + 1) * w, (kv * pairs + p) * LANES:(kv * pairs + p + 1) * LANES]
                           for p in range(pairs)], axis=0) for t, kv in inst]
    scores = [[_dot_nt(q_, band(k_ext[kv][half], t)) for half in range(2)]
              for q_, (t, kv) in zip(q2, inst)]
    bias_first = jnp.where(n > 0, bias, bias_seq_start)
    for (t, kv), sc in zip(inst, scores):
        vband = jnp.concatenate([band(v_ext[kv][0], t), band(v_ext[kv][1], t)], axis=0)
        for p in range(pairs):
            exps, sink_terms = [], []
            for half in range(2):
                sink = sink_ref[kv * group + 2 * p + half]
                s = sc[half][p * w:(p + 1) * w] + (bias_first if t == 0 else bias)
                m = jnp.maximum(jnp.max(s, axis=-1, keepdims=True), sink)
                exps.append(jnp.exp(s - m).astype(BF16))
                sink_terms.append(jnp.exp(sink - m))
            o = _dot(jnp.concatenate(exps, axis=1), vband)
            den = o[:, LANES:] + jnp.where(lo_out, sink_terms[0], sink_terms[1])
            cs = slice((kv * pairs + p) * LANES, (kv * pairs + p + 1) * LANES)
            o_ref[t * w:(t + 1) * w, cs] = (o[:, :LANES] / den).astype(o_ref.dtype)


def _swa_prompt(sinks, q, kv, *, batch, kv_heads, head_dim):
    t, dq = q.shape
    seq = t // batch
    assert seq % WINDOW == 0
    group = dq // (kv_heads * head_dim)
    assert 2 * head_dim == LANES and kv_heads == 2 and group % 2 == 0
    blocks = SWA_BLOCKS_PER_STEP if (seq // WINDOW) % SWA_BLOCKS_PER_STEP == 0 else 1
    nb = seq // (blocks * WINDOW)
    rows = blocks * WINDOW
    return pl.pallas_call(
        functools.partial(_swa_prompt_kernel, kv_heads=kv_heads, group=group, head_dim=head_dim),
        grid=(batch, nb),
        in_specs=[pl.BlockSpec(memory_space=pltpu.SMEM),
                  pl.BlockSpec((rows, dq), lambda b, n: (b * nb + n, 0)),
                  pl.BlockSpec((WINDOW, kv.shape[1]),
                               lambda b, n: ((b * nb + n) * blocks - jnp.minimum(n, 1), 0)),
                  pl.BlockSpec((rows, kv.shape[1]), lambda b, n: (b * nb + n, 0))],
        out_specs=[pl.BlockSpec((rows, dq), lambda b, n: (b * nb + n, 0)),
                   pl.BlockSpec((1, WINDOW, kv.shape[1]), lambda b, n: (b, 0, 0))],
        out_shape=[jax.ShapeDtypeStruct((t, dq), BF16),
                   jax.ShapeDtypeStruct((batch, WINDOW, kv.shape[1]), F32)],
        compiler_params=_params(("parallel", "arbitrary")),
        name="swa_prompt",
    )(sinks, q, kv, kv)


def _swa_sample_kernel(sink_ref, q_ref, kvn_ref, ck_ref, cv_ref, o_ref, nk_ref, nv_ref, *,
                       kv_heads, group, head_dim):
    bb, t, _ = q_ref.shape
    wb = ck_ref.shape[1]
    dkv = kv_heads * head_dim
    q = q_ref[...]
    kc = jnp.concatenate([ck_ref[...], kvn_ref[:, :, :dkv]], axis=1)
    vc = jnp.concatenate([cv_ref[...], kvn_ref[:, :, dkv:]], axis=1)
    nk_ref[...] = kc[:, t:, :]
    nv_ref[...] = vc[:, t:, :]
    kcb = kc.astype(BF16)
    vcb = vc.astype(BF16)
    assert t & (t - 1) == 0
    i = lax.broadcasted_iota(jnp.int32, (group * t, wb + t), 0) & (t - 1)
    j = lax.broadcasted_iota(jnp.int32, (group * t, wb + t), 1)
    diff = wb + i - j
    mask = ((diff >= 0) & (diff < WINDOW))[None]
    outs = []
    for kv in range(kv_heads):
        hs = slice(kv * head_dim, (kv + 1) * head_dim)
        heads = [kv * group + g for g in range(group)]
        qs = jnp.concatenate([q[:, :, h * head_dim:(h + 1) * head_dim] for h in heads], axis=1)
        sink = jnp.concatenate([jnp.full((1, t, 1), sink_ref[h], F32) for h in heads], axis=1)
        s = jnp.einsum('bqd,bkd->bqk', qs, kcb[:, :, hs], preferred_element_type=F32)
        p = _sink_softmax(s, mask, sink)
        o = jnp.einsum('bqk,bkd->bqd', p.astype(BF16), vcb[:, :, hs], preferred_element_type=F32)
        outs.extend(o[:, g * t:(g + 1) * t, :] for g in range(group))
    o_ref[...] = jnp.concatenate(outs, axis=2).astype(o_ref.dtype)


def _swa_sample(sinks, q, kvn, ck, cv, *, kv_heads, head_dim):
    nbatch, t, dq = q.shape
    wb = ck.shape[1]
    group = dq // (kv_heads * head_dim)
    bb = SUBLANES if nbatch % SUBLANES == 0 else 1
    blk = lambda a: pl.BlockSpec((bb,) + a.shape[1:], lambda i: (i, 0, 0))
    return pl.pallas_call(
        functools.partial(_swa_sample_kernel, kv_heads=kv_heads, group=group, head_dim=head_dim),
        grid=(nbatch // bb,),
        in_specs=[pl.BlockSpec(memory_space=pltpu.SMEM), blk(q), blk(kvn), blk(ck), blk(cv)],
        out_specs=[blk(q), blk(ck), blk(cv)],
        out_shape=[jax.ShapeDtypeStruct(q.shape, BF16), jax.ShapeDtypeStruct(ck.shape, F32),
                   jax.ShapeDtypeStruct(cv.shape, F32)],
        compiler_params=_params(("parallel",)),
        name="swa_sample",
    )(sinks, q, kvn, ck, cv)


def _chunk_masks(r, c):
    assert r & (r - 1) == 0 and c & (c - 1) == 0 and LANES % r == 0
    width = 2 * LANES
    ri = lax.broadcasted_iota(jnp.int32, (r, width), 0)
    cj = lax.broadcasted_iota(jnp.int32, (r, width), 1)
    ci = cj & (r - 1)
    incl, strict = ci <= ri, ci < ri
    if r != c:
        shift = c.bit_length() - 1
        same = (ri >> shift) == (ci >> shift)
        incl, strict = same & incl, same & strict
    f01 = lambda mask: jnp.where(mask, 1.0, 0.0)
    block = cj >> (r.bit_length() - 1)
    first_in_tile = lax.broadcasted_iota(jnp.int32, (r, LANES), 1) < r
    return dict(incl=incl, strict=strict, eye=f01(ri == ci), first_in_tile=first_in_tile,
                keep=[f01(block == x).astype(BF16) for x in range(width // r)],
                incl_sel=f01(incl)[:, :r].astype(BF16),
                same_sel=None if r == c else f01(same)[:, :r].astype(BF16))


def _delta_gates(gb, masks, heads, dk):
    r = gb.shape[0]
    gc_all = _dot_sel(masks["incl_sel"], gb)
    if masks["same_sel"] is None:
        gl_all = jnp.broadcast_to(gc_all[r - 1:r, :], gc_all.shape)
    else:
        gl_all = _dot_sel(masks["same_sel"], gb)
    bc = lambda a, lane: jnp.broadcast_to(a[:, lane:lane + 1], (r, dk))
    return [(bc(gc_all, h), bc(gl_all, h), bc(gb, heads + h)) for h in range(heads)]


def _split2(x):
    hi = x.astype(BF16)
    return hi, (x - hi.astype(F32)).astype(BF16)


def _rows(xs):
    return jnp.concatenate(xs, axis=0)


def _lanes(xs):
    return xs[0] if len(xs) == 1 else jnp.concatenate(xs, axis=1)


def _block_diag(blocks):
    n = len(blocks)
    z = jnp.zeros(blocks[0].shape, BF16)
    return _rows([_lanes([blocks[i] if j == i else z for j in range(n)]) for i in range(n)])


def _delta_intra(insts, masks, c):
    r, dk = insts[0][0].shape
    dv = insts[0][2].shape[1]
    m = LANES // r
    n = 2 * m
    assert dk == LANES and len(insts) % n == 0
    incl, strict, eye, keep = masks["incl"], masks["strict"], masks["eye"], masks["keep"]
    tiles = [list(range(i, i + m)) for i in range(0, len(insts), m)]
    groups = [(2 * i, 2 * i + 1) for i in range(len(tiles) // 2)]
    qs, ks, vs, gcs, gls, betas = ([i[j] for i in insts] for j in range(6))
    bf = lambda x: x.astype(BF16)

    def lane_tile(xs):
        return xs[0] if m == 1 else jnp.where(masks["first_in_tile"], xs[0], xs[1])

    def block_diag_x(x):
        return _rows([x * keep[i] for i in range(n)])

    g_cols = [_lanes([lane_tile([gcs[i] for i in tiles[t]]) for t in g]) for g in groups]
    g_rows = [jnp.transpose(_rows([gcs[i] for t in g for i in tiles[t]]))[:r, :] for g in groups]
    decays = [jnp.where(incl, jnp.exp(gc - gr), 0.0) for gc, gr in zip(g_cols, g_rows)]
    egcs = [jnp.exp(gc) for gc in gcs]
    kbs = [k * beta for k, beta in zip(ks, betas)]
    kq = [_dot_nt(_rows([_lanes([bf(kbs[i]) for i in t]), _lanes([bf(qs[i]) for i in t])]),
                  _block_diag([bf(ks[i]) for i in t])) for t in tiles]
    ms = [jnp.where(strict, _lanes([kq[t][:r] for t in g]) * decay, 0.0) for g, decay in zip(groups, decays)]
    qks = [_lanes([kq[t][r:] for t in g]) * decay for g, decay in zip(groups, decays)]
    rounds = max((c - 1).bit_length(), 1)
    ps = [-m_ for m_ in ms]
    tinvs = [None] * len(groups)
    for j in range(rounds):
        square = j + 1 < rounds
        if not square and tinvs[0] is None:
            tinvs = [eye + p for p in ps]
            break
        p_hl = [_split2(p) for p in ps]
        weights = [block_diag_x(hi) for hi, _ in p_hl]
        lhs = [_rows((list(hl) if square else []) + ([] if t is None else list(_split2(t))))
               for hl, t in zip(p_hl, tinvs)]
        res = [_dot(a, w) for a, w in zip(lhs, weights)]
        new_ps, new_ts = [], []
        for p, t, a in zip(ps, tinvs, res):
            o = 0
            if square:
                new_ps.append(a[:r] + a[r:2 * r])
                o = 2 * r
            new_ts.append(eye + p if t is None else t + (a[o:o + r] + a[o + r:o + 2 * r]))
        ps, tinvs = new_ps if square else ps, new_ts
    rhs = [jnp.concatenate([kb * egc, v * beta], axis=1).astype(BF16)
           for kb, egc, v, beta in zip(kbs, egcs, vs, betas)]
    t_hl = [_split2(t) for t in tinvs]
    sols = []
    for ti, t in enumerate(tiles):
        hi, lo = (x[:, (ti % 2) * LANES:(ti % 2 + 1) * LANES] for x in t_hl[ti // 2])
        a = _dot(_rows([hi, lo]), _block_diag([rhs[i] for i in t]))
        sols.append(a[:r] + a[r:])
    out = []
    for ti, t in enumerate(tiles):
        per_inst = []
        for slot, i in enumerate(t):
            s_ = sols[ti][:, slot * (dk + dv):(slot + 1) * (dk + dv)]
            per_inst.append((s_[:, :dk], s_[:, dk:], qs[i] * egcs[i], ks[i] * jnp.exp(gls[i] - gcs[i]),
                             jnp.exp(gls[i])))
        out.append((qks[ti // 2][:, (ti % 2) * LANES:(ti % 2 + 1) * LANES], per_inst))
    return out


def _gated_out_norm(o, w, z):
    return (_rms(o, w) * _silu(z)).astype(BF16)


def _head_cols(xb, h, heads, dk, dv):
    q = xb[:, h * dk:(h + 1) * dk]
    k = xb[:, (heads + h) * dk:(heads + h + 1) * dk]
    v = xb[:, 2 * heads * dk + h * dv:2 * heads * dk + (h + 1) * dv]
    return q, k, v


def _delta_prompt_intra_kernel(x_ref, gb_ref, kq_ref, vu_ref, kd_ref, qk_ref, gt_ref, *,
                               heads, dk, dv, chunk):
    rows = x_ref.shape[0]
    xb = x_ref[...]
    gb = gb_ref[...]
    masks = _chunk_masks(chunk, chunk)
    insts = []
    for g in range(rows // chunk):
        rs = slice(g * chunk, (g + 1) * chunk)
        gates = _delta_gates(gb[rs], masks, heads, dk)
        insts.extend(_head_cols(xb[rs], h, heads, dk, dv) + gates[h] for h in range(heads))
    intra = _delta_intra(insts, masks, chunk)

    tiles_per_chunk = len(intra) // (rows // chunk)
    for ti, (qk, per_inst) in enumerate(intra):
        g, tc = divmod(ti, tiles_per_chunk)
        qk_ref[0, g, tc] = qk.astype(BF16)
        for x, (k_cum, v_u, q_dec, k_dec, g_tot) in enumerate(per_inst):
            h = tc * len(per_inst) + x
            kq_ref[0, g, h] = _rows([k_cum, q_dec]).astype(BF16)
            vu_ref[0, g, h] = v_u
            kd_ref[0, g, h] = k_dec.astype(BF16)
            gt_ref[0, g, h] = g_tot[0:SUBLANES, :]


def _delta_prompt_state_kernel(kq_ref, vu_ref, kd_ref, qk_ref, gt_ref, z_ref, nw_ref, o_ref, s_out_ref,
                               s_ref):
    n = pl.program_id(0)
    batch, _, heads, c, dv = vu_ref.shape
    tiles = qk_ref.shape[2]
    m = heads // tiles

    @pl.when(n == 0)
    def _():
        s_ref[...] = jnp.zeros(s_ref.shape, F32)

    inst = [(b, h) for b in range(batch) for h in range(heads)]
    state = [s_ref[b, h] for b, h in inst]
    res = [_dot(kq_ref[b, 0, h], s.astype(BF16)) for (b, h), s in zip(inst, state)]
    v_new = [vu_ref[b, 0, h] - r_[:c] for (b, h), r_ in zip(inst, res)]
    vn_bf = [vn.astype(BF16) for vn in v_new]
    o_intra = [_dot(qk_ref[b, 0, t], _block_diag(vn_bf[(b * tiles + t) * m:(b * tiles + t + 1) * m]))
               for b in range(batch) for t in range(tiles)]
    for i, (b, h) in enumerate(inst):
        s_ref[b, h] = state[i] * gt_ref[b, 0, h][0:1, :] + _dot_tn(kd_ref[b, 0, h], vn_bf[i])
    for i, (b, h) in enumerate(inst):
        o = res[i][c:] + o_intra[i // m][:, (i % m) * dv:(i % m + 1) * dv]
        cs = slice(h * dv, (h + 1) * dv)
        o_ref[b, 0, :, cs] = _gated_out_norm(o, nw_ref[...], z_ref[b, 0, :, cs])

    @pl.when(n == pl.num_programs(0) - 1)
    def _():
        s_out_ref[...] = s_ref[...]


def _delta_prompt(x, z, gb, norm_w, *, batch, heads, dk, dv):
    t, cdim = x.shape
    seq = t // batch
    c = DN_CHUNK if seq % DN_CHUNK == 0 else seq
    assert c % SUBLANES == 0 and dk == dv == LANES
    nc = seq // c
    group = PROMPT_CHUNKS_PER_STEP if nc % PROMPT_CHUNKS_PER_STEP == 0 else 1
    rows = group * c
    steps = seq // rows
    tiles = heads * c // LANES
    blk = lambda n_: pl.BlockSpec((rows, n_), lambda b, n: (b * steps + n, 0))
    per_chunk = lambda per, r_, dt: (
        pl.BlockSpec((1, group, per, r_, LANES), lambda b, n: (b, n, 0, 0, 0)),
        jax.ShapeDtypeStruct((batch, nc, per, r_, LANES), dt))
    outs = [per_chunk(heads, 2 * c, BF16), per_chunk(heads, c, F32), per_chunk(heads, c, BF16),
            per_chunk(tiles, c, BF16), per_chunk(heads, SUBLANES, F32)]
    kq, vu, kd, qk, gt = pl.pallas_call(
        functools.partial(_delta_prompt_intra_kernel, heads=heads, dk=dk, dv=dv, chunk=c),
        grid=(batch, steps),
        in_specs=[blk(cdim), blk(gb.shape[1])],
        out_specs=[o[0] for o in outs],
        out_shape=[o[1] for o in outs],
        compiler_params=_params(("parallel", "parallel")),
        name="delta_prompt_intra",
    )(x, gb)
    all_seq = lambda a: pl.BlockSpec((batch, 1) + a.shape[2:], lambda n: (0, n) + (0,) * (a.ndim - 2))
    z4 = z.reshape(batch, nc, c, heads * dv)
    o, s_out = pl.pallas_call(
        _delta_prompt_state_kernel,
        grid=(nc,),
        in_specs=[all_seq(kq), all_seq(vu), all_seq(kd), all_seq(qk), all_seq(gt), all_seq(z4),
                  _const_spec(norm_w.shape)],
        out_specs=[all_seq(z4), _const_spec((batch, heads, dk, dv))],
        out_shape=[jax.ShapeDtypeStruct(z4.shape, BF16), jax.ShapeDtypeStruct((batch, heads, dk, dv), F32)],
        scratch_shapes=[pltpu.VMEM((batch, heads, dk, dv), F32)],
        compiler_params=_params(("arbitrary",)),
        name="delta_prompt_state",
    )(kq, vu, kd, qk, gt, z4, norm_w)
    return o.reshape(t, heads * dv), s_out


def _delta_sample_kernel(x_ref, cbuf_ref, z_ref, gb_ref, s_in_ref, cw_ref, nw_ref, o_ref, s_out_ref,
                         xc_ref, *, heads, dk, dv, conv_w):
    nb, hist, cdim = cbuf_ref.shape
    r = x_ref.shape[0]
    t = r // nb
    xc_ref[:, SUBLANES:SUBLANES + t, :] = x_ref[...].reshape(nb, t, cdim)
    xc_ref[:, SUBLANES - hist:SUBLANES, :] = cbuf_ref[...]
    acc = None
    for jj in range(conv_w):
        off = SUBLANES - (conv_w - 1) + jj
        term = xc_ref[:, off:off + t, :] * cw_ref[jj:jj + 1, :][None]
        acc = term if acc is None else acc + term
    xb = _silu(acc).reshape(r, cdim)

    masks = _chunk_masks(r, t)
    gates = _delta_gates(gb_ref[...], masks, heads, dk)
    insts = []
    for h in range(heads):
        q, k, v = _head_cols(xb, h, heads, dk, dv)
        insts.append((_unit(q) * (dk ** -0.5), _unit(k), v) + gates[h])
    intra = _delta_intra(insts, masks, t)
    rows = [slice(b * t, (b + 1) * t) for b in range(nb)]
    for h in range(heads):
        qk, ((k_cum, v_u, q_dec, k_dec, g_tot),) = intra[h]
        state = [s_in_ref[b, h] for b in range(nb)]
        res = [_dot(jnp.concatenate([k_cum[rs], q_dec[rs]], axis=0).astype(BF16), s.astype(BF16))
               for rs, s in zip(rows, state)]
        v_new = [v_u[rs] - r_[:t] for rs, r_ in zip(rows, res)]
        for b in range(nb):
            s_out_ref[b, h] = (state[b] * g_tot[b * t:b * t + 1, :]
                               + _dot_tn(k_dec[rows[b]].astype(BF16), v_new[b].astype(BF16)))
        o = (jnp.concatenate([r_[t:] for r_ in res], axis=0)
             + _dot(qk.astype(BF16), jnp.concatenate(v_new, axis=0).astype(BF16)))
        o_ref[:, h * dv:(h + 1) * dv] = _gated_out_norm(o, nw_ref[...], z_ref[:, h * dv:(h + 1) * dv])


def _delta_sample(x, cbuf, z, gb, s0, conv_w, norm_w, *, heads, dk, dv):
    rows, cdim = x.shape
    nbatch, hist, _ = cbuf.shape
    t = rows // nbatch
    assert t == SUBLANES and hist == conv_w.shape[0] - 1 and hist <= t and dk == dv
    nb = SAMPLE_GROUP if nbatch % SAMPLE_GROUP == 0 else 1
    r = nb * t
    blk = lambda n_: pl.BlockSpec((r, n_), lambda i: (i, 0))
    sblk = pl.BlockSpec((nb, heads, dk, dv), lambda i: (i, 0, 0, 0))
    return pl.pallas_call(
        functools.partial(_delta_sample_kernel, heads=heads, dk=dk, dv=dv, conv_w=conv_w.shape[0]),
        grid=(nbatch // nb,),
        in_specs=[blk(cdim), pl.BlockSpec((nb, hist, cdim), lambda i: (i, 0, 0)), blk(z.shape[1]),
                  blk(gb.shape[1]), sblk, _const_spec(conv_w.shape), _const_spec(norm_w.shape)],
        out_specs=[blk(heads * dv), sblk],
        out_shape=[jax.ShapeDtypeStruct((rows, heads * dv), BF16), jax.ShapeDtypeStruct(s0.shape, F32)],
        scratch_shapes=[pltpu.VMEM((nb, 2 * SUBLANES, cdim), F32)],
        compiler_params=_params(("parallel",)),
        name="delta_sample",
    )(x, cbuf, z, gb, s0, conv_w, norm_w)


def _layer(xp, xs, ck, cv, cconv, sdelta, lw, dims):
    (f1_pre, f1_post, f1_up, f1_down, mix_pre, mix_post, w_in, sinks, conv_w, a_log, dt_bias,
     out_norm, w_a, w_b, w_o, f2_pre, f2_post, f2_up, f2_down) = lw
    batch, seq, d = xp.shape
    dbatch, dseq, _ = xs.shape
    kvh, hd = dims["kv_heads"], dims["head_dim"]
    hb, dk, dv = dims["b_heads"], dims["dk"], dims["dv"]
    a_heads = w_a.shape[0] // hd
    cdim = conv_w.shape[1]
    d_ff = f1_down.shape[0]
    bf = lambda a: a.astype(BF16)
    row = lambda a: a.reshape(1, -1).astype(F32)

    sizes = (a_heads * hd, kvh * hd, kvh * hd, cdim, hb * dv, hb, hb, d, d)
    offs = [0]
    for s_ in sizes:
        offs.append(offs[-1] + s_)
    w_in_bf = bf(w_in)
    wga, wgb = w_in_bf[:, offs[7]:offs[8]], w_in_bf[:, offs[8]:offs[9]]
    proj_offs = (offs[0], offs[1], offs[3], offs[4], offs[5])
    alog = jnp.pad(row(a_log), ((0, 0), (0, LANES - hb)))
    dtb = jnp.pad(row(dt_bias), ((0, 0), (0, LANES - hb)))

    def ffn(x, pre, post, up, down):
        return _ffn(x, row(pre), row(post), bf(up), bf(down))

    def inproj(x, **conv):
        return _inproj(x, row(mix_pre), w_in_bf, alog, dtb, q_scale=hd ** -0.5, n_heads_b=hb,
                       offs=proj_offs, **conv)

    def merge(x, oa, ob):
        return _merge(x, oa, ob, row(mix_pre), row(mix_post), wga, wgb, bf(w_a), bf(w_b), bf(w_o))

    sinks = sinks.astype(F32)
    norm_w = row(out_norm)
    conv_w = conv_w.astype(F32)
    hist = conv_w.shape[0] - 1

    x = ffn(xp.reshape(batch * seq, d), f1_pre, f1_post, f1_up, f1_down)
    q, kv, xb, z, gb, tail = inproj(x, conv_w=conv_w, seq_rows=seq, dk=dk)
    oa, kv_last = _swa_prompt(sinks, q, kv, batch=batch, kv_heads=kvh, head_dim=hd)
    ob, p_state = _delta_prompt(xb, z, gb, norm_w, batch=batch, heads=hb, dk=dk, dv=dv)
    x = merge(x, oa, ob)
    yp = ffn(x, f2_pre, f2_post, f2_up, f2_down).reshape(batch, seq, d)
    kv_last = kv_last.reshape(batch, WINDOW, 2, kvh, hd)
    p_k, p_v = kv_last[:, :, 0], kv_last[:, :, 1]
    p_conv = tail.reshape(batch, -1, SUBLANES, cdim)[:, -1, SUBLANES - hist:]

    x = ffn(xs.reshape(dbatch * dseq, d), f1_pre, f1_post, f1_up, f1_down)
    q, kv, xb, z, gb = inproj(x)
    wb = ck.shape[1]
    oa, s_k, s_v = _swa_sample(sinks, q.reshape(dbatch, dseq, -1), kv.reshape(dbatch, dseq, -1),
                               ck.reshape(dbatch, wb, kvh * hd), cv.reshape(dbatch, wb, kvh * hd),
                               kv_heads=kvh, head_dim=hd)
    ob, s_state = _delta_sample(xb, cconv, z, gb, sdelta, conv_w, norm_w, heads=hb, dk=dk, dv=dv)
    x = merge(x, oa.reshape(dbatch * dseq, -1), ob)
    ys = ffn(x, f2_pre, f2_post, f2_up, f2_down).reshape(dbatch, dseq, d)
    s_k = s_k.reshape(dbatch, wb, kvh, hd)
    s_v = s_v.reshape(dbatch, wb, kvh, hd)
    s_conv = xb.reshape(dbatch, dseq, cdim)[:, dseq - hist:]
    return yp, ys, (p_k, p_v, p_conv, p_state), (s_k, s_v, s_conv, s_state)


def kernel(x_prompt, x_sample, cache_swa_k, cache_swa_v, state_conv, state_delta, ffn1_norm_pre, ffn1_norm_post, ffn1_w_up, ffn1_w_down, mix_norm_pre, mix_norm_post, w_in, attn_sinks, conv_w, dn_a_log, dn_dt_bias, dn_out_norm, w_branch_a, w_branch_b, w_out, ffn2_norm_pre, ffn2_norm_post, ffn2_w_up, ffn2_w_down):
    weights = (ffn1_norm_pre, ffn1_norm_post, ffn1_w_up, ffn1_w_down, mix_norm_pre, mix_norm_post,
               w_in, attn_sinks, conv_w, dn_a_log, dn_dt_bias, dn_out_norm, w_branch_a, w_branch_b,
               w_out, ffn2_norm_pre, ffn2_norm_post, ffn2_w_up, ffn2_w_down)
    depth = w_in.shape[0]
    dims = dict(kv_heads=cache_swa_k.shape[3], head_dim=cache_swa_k.shape[4],
                b_heads=state_delta.shape[2], dk=state_delta.shape[3], dv=state_delta.shape[4])
    yp, ys = x_prompt, x_sample
    p_out, s_out = [], []
    for l in range(depth):
        yp, ys, p_new, s_new = _layer(yp, ys, cache_swa_k[l], cache_swa_v[l], state_conv[l],
                                      state_delta[l], tuple(w[l] for w in weights), dims)
        p_out.append(p_new)
        s_out.append(s_new)
    stack = lambda outs, i: jnp.stack([o[i] for o in outs])
    return (yp, ys,
            stack(p_out, 0), stack(p_out, 1), stack(p_out, 2), stack(p_out, 3),
            stack(s_out, 0), stack(s_out, 1), stack(s_out, 2), stack(s_out, 3))
```

```python
import functools

import jax
import jax.numpy as jnp
from jax import lax
from jax.experimental import pallas as pl
from jax.experimental.pallas import tpu as pltpu

WINDOW = 128
DN_CHUNK = 64
EPS = 1e-6
F32 = jnp.float32
BF16 = jnp.bfloat16
NEG_BIG = -1e30
LANES = 128
SUBLANES = 8
VMEM_LIMIT_BYTES = 56 * 1024 * 1024
FF_CHUNK = 256
MXU_COLS = 256
CONV_ROW_BLOCK = 128
SAMPLE_GROUP = 16
SWA_BLOCKS_PER_STEP = 4
PROMPT_CHUNKS_PER_STEP = 4


def _rms(x, w):
    return x * lax.rsqrt(jnp.mean(x * x, axis=-1, keepdims=True) + EPS) * w


def _silu(x):
    return x * jax.nn.sigmoid(x)


def _unit(x):
    return x * lax.rsqrt(jnp.sum(x * x, axis=-1, keepdims=True) + EPS)


def _softplus(x):
    return jnp.maximum(x, 0.0) + jnp.log1p(jnp.exp(-jnp.abs(x)))


def _dot(a, b):
    return jnp.dot(a, b, preferred_element_type=F32)


def _dot_nt(a, b):
    return lax.dot_general(a, b, (((1,), (1,)), ((), ())), preferred_element_type=F32)


def _dot_tn(a, b):
    return lax.dot_general(a, b, (((0,), (0,)), ((), ())), preferred_element_type=F32)


def _split3(b):
    hi = b.astype(BF16)
    r = b - hi.astype(F32)
    mid = r.astype(BF16)
    lo = (r - mid.astype(F32)).astype(BF16)
    return hi, mid, lo


def _dot_sel(sel_bf16, b):
    hi, mid, lo = _split3(b)
    return _dot(sel_bf16, hi) + _dot(sel_bf16, mid) + _dot(sel_bf16, lo)


def _ffn_tile(x_ref, pre_ref, post_ref, wup_ref, wd_ref, o_ref):
    x = x_ref[...]
    xn = _rms(x, pre_ref[...]).astype(BF16)
    d_ff = wd_ref.shape[0]
    acc = jnp.zeros(x.shape, F32)
    for c in range(d_ff // FF_CHUNK):
        sl = slice(c * FF_CHUNK, (c + 1) * FF_CHUNK)
        g = _dot(xn, wup_ref[:, sl])
        u = _dot(xn, wup_ref[:, d_ff + c * FF_CHUNK:d_ff + (c + 1) * FF_CHUNK])
        acc = acc + _dot((_silu(g) * u).astype(BF16), wd_ref[sl, :])
    o_ref[...] = x + 0.5 * _rms(acc, post_ref[...])


def _ffn_kernel(xa_ref, xb_ref, pre_ref, post_ref, wup_ref, wd_ref, oa_ref, ob_ref, *, tiles_a):
    i = pl.program_id(0)

    @pl.when(i < tiles_a)
    def _():
        _ffn_tile(xa_ref, pre_ref, post_ref, wup_ref, wd_ref, oa_ref)

    @pl.when(i >= tiles_a)
    def _():
        _ffn_tile(xb_ref, pre_ref, post_ref, wup_ref, wd_ref, ob_ref)


def _const_spec(shape):
    nd = len(shape)
    return pl.BlockSpec(shape, lambda *_: (0,) * nd)


def _weight_spec(shape):
    nd = len(shape)
    return pl.BlockSpec(shape, lambda *_: (0,) * nd, pipeline_mode=pl.Buffered(1))


def _row_tile(t, largest=512):
    for tm in (1024, 512, 256, 128, 64, 32, 16, 8):
        if tm <= largest and t % tm == 0:
            return tm
    raise ValueError(f"token count {t} is not a multiple of 8")


def _params(sem):
    return pltpu.CompilerParams(dimension_semantics=sem, vmem_limit_bytes=VMEM_LIMIT_BYTES)


def _ffn(xa, xb, pre, post, wup, wd):
    (ta, d), tb = xa.shape, xb.shape[0]
    f = wd.shape[0]
    assert f % FF_CHUNK == 0 and wup.shape == (d, 2 * f)
    tm = min(_row_tile(ta), _row_tile(tb))
    na, nb = ta // tm, tb // tm
    row_a = pl.BlockSpec((tm, d), lambda i: (jnp.minimum(i, na - 1), 0))
    row_b = pl.BlockSpec((tm, d), lambda i: (jnp.maximum(i - na, 0), 0))
    return pl.pallas_call(
        functools.partial(_ffn_kernel, tiles_a=na),
        grid=(na + nb,),
        in_specs=[row_a, row_b, _const_spec((1, d)), _const_spec((1, d)),
                  _weight_spec((d, 2 * f)), _weight_spec((f, d))],
        out_specs=[row_a, row_b],
        out_shape=[jax.ShapeDtypeStruct((ta, d), F32), jax.ShapeDtypeStruct((tb, d), F32)],
        compiler_params=_params(("arbitrary",)),
        name="ffn",
    )(xa, xb, pre, post, wup, wd)


def _conv_silu_unit(xc_ref, cv_ref, cw_ref, r0, rows, cb, heads, dk):
    taps = cw_ref.shape[0]
    rs = slice(r0, r0 + rows)
    cs = slice(cb * dk, (cb + 1) * dk)
    acc = None
    for jj in range(taps):
        off = r0 + SUBLANES - (taps - 1) + jj
        term = xc_ref[off:off + rows, cs] * cw_ref[jj:jj + 1, cs]
        acc = term if acc is None else acc + term
    cv_ref[rs, cs] = acc
    y = _silu(cv_ref[rs, cs])
    if cb < 2 * heads:
        y = _unit(y) * (dk ** -0.5) if cb < heads else _unit(y)
    cv_ref[rs, cs] = y


def _inproj_kernel(x_ref, pre_ref, w_ref, alog_ref, dtb_ref, *rest, q_scale, n_heads_b, offs, conv):
    tm = x_ref.shape[0]
    h = _rms(x_ref[...], pre_ref[...]).astype(BF16)
    if conv is None:
        q_ref, kv_ref, b_ref, z_ref, gb_ref = rest
        sub = tm
    else:
        cw_ref, q_ref, kv_ref, b_ref, z_ref, gb_ref, tail_ref, xc_ref = rest
        seq_rows, dk = conv
        sub = min(tm, CONV_ROW_BLOCK)

        @pl.when((pl.program_id(0) * tm) % seq_rows == 0)
        def _():
            xc_ref[0:SUBLANES, :] = jnp.zeros((SUBLANES, xc_ref.shape[1]), F32)

    def project(r0, i, c0, width):
        rs = slice(r0, r0 + sub)
        cs = slice(c0, c0 + width)
        y = _dot(h[rs], w_ref[:, offs[i] + c0:offs[i] + c0 + width])
        if i == 0:
            q_ref[rs, cs] = (y * q_scale).astype(BF16)
        elif i == 1:
            kv_ref[rs, cs] = y
        elif i == 2 and conv is None:
            b_ref[rs, cs] = y
        elif i == 2:
            xc_ref[SUBLANES + r0:SUBLANES + r0 + sub, cs] = y
        elif i == 3:
            z_ref[rs, cs] = y
        else:
            g = -jnp.exp(alog_ref[...]) * _softplus(y + dtb_ref[...])
            lane = lax.broadcasted_iota(jnp.int32, y.shape, 1)
            gb_ref[rs, :] = jnp.where(lane < n_heads_b, g, jax.nn.sigmoid(y))

    widths = [offs[i + 1] - offs[i] for i in range(4)] + [LANES]
    pieces = [(i, c0, min(MXU_COLS, widths[i] - c0))
              for i in (2, 0, 1, 3, 4) for c0 in range(0, widths[i], MXU_COLS)]
    for r0 in range(0, tm + (sub if conv is not None else 0), sub):
        mm = [functools.partial(project, r0, *p) for p in pieces] if r0 < tm else []
        cv = []
        if conv is not None and r0 > 0:
            cv = [functools.partial(_conv_silu_unit, xc_ref, b_ref, cw_ref, r0 - sub, sub, cb, n_heads_b, dk)
                  for cb in range(xc_ref.shape[1] // dk)]
        for k in range(max(len(mm), len(cv))):
            if k < len(mm):
                mm[k]()
            if k < len(cv):
                cv[k]()
    if conv is not None:
        tail_ref[0] = xc_ref[tm:tm + SUBLANES, :]
        xc_ref[0:SUBLANES, :] = xc_ref[tm:tm + SUBLANES, :]


def _inproj(x, pre, w, alog, dtb, *, q_scale, n_heads_b, offs, conv_w=None, seq_rows=None, dk=None):
    t, d = x.shape
    tm = _row_tile(t)
    assert all(o % LANES == 0 for o in offs[:5]) and offs[4] + LANES <= w.shape[1]
    row = lambda n: pl.BlockSpec((tm, n), lambda i: (i, 0))
    cdim = offs[3] - offs[2]
    outs = [(offs[1] - offs[0], BF16), (offs[2] - offs[1], F32), (cdim, F32), (offs[4] - offs[3], F32),
            (LANES, F32)]
    in_specs = [row(d), _const_spec((1, d)), _weight_spec(w.shape),
                _const_spec(alog.shape), _const_spec(dtb.shape)]
    out_specs = [row(n) for n, _ in outs]
    out_shape = [jax.ShapeDtypeStruct((t, n), dt) for n, dt in outs]
    args, scratch, conv = [x, pre, w, alog, dtb], [], None
    if conv_w is not None:
        assert seq_rows % tm == 0 and conv_w.shape[0] - 1 <= SUBLANES
        in_specs.append(_const_spec(conv_w.shape))
        args.append(conv_w)
        out_specs.append(pl.BlockSpec((1, SUBLANES, cdim), lambda i: (i, 0, 0)))
        out_shape.append(jax.ShapeDtypeStruct((t // tm, SUBLANES, cdim), F32))
        scratch = [pltpu.VMEM((tm + SUBLANES, cdim), F32)]
        conv = (seq_rows, dk)
    return pl.pallas_call(
        functools.partial(_inproj_kernel, q_scale=q_scale, n_heads_b=n_heads_b, offs=offs, conv=conv),
        grid=(t // tm,),
        in_specs=in_specs,
        out_specs=out_specs,
        out_shape=out_shape,
        scratch_shapes=scratch,
        compiler_params=_params(("arbitrary",)),
        name="inproj",
    )(*args)


def _merge_tile(x_ref, oa_ref, ob_ref, pre_ref, post_ref, wga_ref, wgb_ref, wa_ref, wb_ref, wo_ref,
                o_ref):
    x = x_ref[...]
    h = _rms(x, pre_ref[...]).astype(BF16)
    oa, ob = oa_ref[...], ob_ref[...]
    m = jnp.zeros(x.shape, F32)
    for c0 in range(0, wo_ref.shape[0], MXU_COLS):
        cs = slice(c0, c0 + MXU_COLS)
        merged = (jax.nn.sigmoid(_dot(h, wga_ref[:, cs])) * _dot(oa, wa_ref[:, cs])
                  + jax.nn.sigmoid(_dot(h, wgb_ref[:, cs])) * _dot(ob, wb_ref[:, cs]))
        m = m + _dot(merged.astype(BF16), wo_ref[cs, :])
    o_ref[...] = x + _rms(m, post_ref[...])


def _merge_kernel(xa_ref, oaa_ref, oba_ref, xb_ref, oab_ref, obb_ref, *rest, tiles_a):
    *shared, oa_ref, ob_ref = rest
    i = pl.program_id(0)

    @pl.when(i < tiles_a)
    def _():
        _merge_tile(xa_ref, oaa_ref, oba_ref, *shared, oa_ref)

    @pl.when(i >= tiles_a)
    def _():
        _merge_tile(xb_ref, oab_ref, obb_ref, *shared, ob_ref)


def _merge(group_a, group_b, pre, post, wga, wgb, wa, wb, wo):
    (ta, d), tb = group_a[0].shape, group_b[0].shape[0]
    tm = min(_row_tile(ta, largest=1024), _row_tile(tb, largest=1024))
    na, nb = ta // tm, tb // tm
    rows_a = [pl.BlockSpec((tm, a.shape[1]), lambda i: (jnp.minimum(i, na - 1), 0)) for a in group_a]
    rows_b = [pl.BlockSpec((tm, a.shape[1]), lambda i: (jnp.maximum(i - na, 0), 0)) for a in group_b]
    weights = (wga, wgb, wa, wb, wo)
    return pl.pallas_call(
        functools.partial(_merge_kernel, tiles_a=na),
        grid=(na + nb,),
        in_specs=rows_a + rows_b + [_const_spec((1, d)), _const_spec((1, d))]
        + [_weight_spec(w.shape) for w in weights],
        out_specs=[rows_a[0], rows_b[0]],
        out_shape=[jax.ShapeDtypeStruct((ta, d), F32), jax.ShapeDtypeStruct((tb, d), F32)],
        compiler_params=_params(("arbitrary",)),
        name="merge",
    )(*group_a, *group_b, pre, post, *weights)


def _sink_softmax(s, mask, sink):
    s = jnp.where(mask, s, NEG_BIG)
    m = jnp.maximum(jnp.max(s, axis=-1, keepdims=True), sink)
    e = jnp.exp(s - m)
    return e / (jnp.sum(e, axis=-1, keepdims=True) + jnp.exp(sink - m))


def _swa_prompt_kernel(sink_ref, q_ref, kvp_ref, kvc_ref, o_ref, kv_last_ref, *, kv_heads, group,
                       head_dim):
    n = pl.program_id(1)
    w = WINDOW
    blocks = q_ref.shape[0] // w
    dkv = kv_heads * head_dim
    pairs = group // 2
    kv_all = jnp.concatenate([kvp_ref[...], kvc_ref[...]], axis=0)

    @pl.when(n == pl.num_programs(1) - 1)
    def _():
        kv_last_ref[0] = kvc_ref[(blocks - 1) * w:blocks * w, :]
    lo = lax.broadcasted_iota(jnp.int32, (kv_all.shape[0], dkv), 1) < head_dim

    def lane_halves(x):
        x_rot = pltpu.roll(x, head_dim, axis=1)
        pick = lambda kv, half: jnp.where(lo if half == 0 else ~lo, x if kv == half else x_rot, 0.0)
        return [[pick(kv, half).astype(BF16) for half in range(2)] for kv in range(kv_heads)]

    k_ext = lane_halves(kv_all[:, :dkv])
    v_ext = lane_halves(kv_all[:, dkv:])

    i = lax.broadcasted_iota(jnp.int32, (w, 2 * w), 0)
    j = lax.broadcasted_iota(jnp.int32, (w, 2 * w), 1)
    bias = jnp.where(j > i, jnp.where(j <= i + w, 0.0, NEG_BIG), NEG_BIG)
    bias_seq_start = jnp.where(j >= w, bias, NEG_BIG)
    lo_out = lax.broadcasted_iota(jnp.int32, (w, 2 * head_dim), 1) < head_dim

    ones_half = [jnp.where(lo if half == 0 else ~lo, 1.0, 0.0).astype(BF16) for half in range(2)]
    v_ext = [[jnp.concatenate([v_ext[kv][half], ones_half[half]], axis=1) for half in range(2)]
             for kv in range(kv_heads)]

    inst = [(t, kv) for t in range(blocks) for kv in range(kv_heads)]
    band = lambda x, t: x[t * w:(t + 2) * w]
    q2 = [jnp.concatenate([q_ref[t * w:(t + 1) * w, (kv * pairs + p) * LANES:(kv * pairs + p + 1) * LANES]
                           for p in range(pairs)], axis=0) for t, kv in inst]
    scores = [[_dot_nt(q_, band(k_ext[kv][half], t)) for half in range(2)]
              for q_, (t, kv) in zip(q2, inst)]
    bias_first = jnp.where(n > 0, bias, bias_seq_start)
    for (t, kv), sc in zip(inst, scores):
        vband = jnp.concatenate([band(v_ext[kv][0], t), band(v_ext[kv][1], t)], axis=0)
        for p in range(pairs):
            exps, sink_terms = [], []
            for half in range(2):
                sink = sink_ref[kv * group + 2 * p + half]
                s = sc[half][p * w:(p + 1) * w] + (bias_first if t == 0 else bias)
                m = jnp.maximum(jnp.max(s, axis=-1, keepdims=True), sink)
                exps.append(jnp.exp(s - m).astype(BF16))
                sink_terms.append(jnp.exp(sink - m))
            o = _dot(jnp.concatenate(exps, axis=1), vband)
            den = o[:, LANES:] + jnp.where(lo_out, sink_terms[0], sink_terms[1])
            cs = slice((kv * pairs + p) * LANES, (kv * pairs + p + 1) * LANES)
            o_ref[t * w:(t + 1) * w, cs] = (o[:, :LANES] / den).astype(o_ref.dtype)


def _swa_prompt(sinks, q, kv, *, batch, kv_heads, head_dim):
    t, dq = q.shape
    seq = t // batch
    assert seq % WINDOW == 0
    group = dq // (kv_heads * head_dim)
    assert 2 * head_dim == LANES and kv_heads == 2 and group % 2 == 0
    blocks = SWA_BLOCKS_PER_STEP if (seq // WINDOW) % SWA_BLOCKS_PER_STEP == 0 else 1
    nb = seq // (blocks * WINDOW)
    rows = blocks * WINDOW
    return pl.pallas_call(
        functools.partial(_swa_prompt_kernel, kv_heads=kv_heads, group=group, head_dim=head_dim),
        grid=(batch, nb),
        in_specs=[pl.BlockSpec(memory_space=pltpu.SMEM),
                  pl.BlockSpec((rows, dq), lambda b, n: (b * nb + n, 0)),
                  pl.BlockSpec((WINDOW, kv.shape[1]),
                               lambda b, n: ((b * nb + n) * blocks - jnp.minimum(n, 1), 0)),
                  pl.BlockSpec((rows, kv.shape[1]), lambda b, n: (b * nb + n, 0))],
        out_specs=[pl.BlockSpec((rows, dq), lambda b, n: (b * nb + n, 0)),
                   pl.BlockSpec((1, WINDOW, kv.shape[1]), lambda b, n: (b, 0, 0))],
        out_shape=[jax.ShapeDtypeStruct((t, dq), BF16),
                   jax.ShapeDtypeStruct((batch, WINDOW, kv.shape[1]), F32)],
        compiler_params=_params(("parallel", "arbitrary")),
        name="swa_prompt",
    )(sinks, q, kv, kv)


def _swa_sample_kernel(sink_ref, q_ref, kvn_ref, ck_ref, cv_ref, o_ref, nk_ref, nv_ref, *,
                       kv_heads, group, head_dim):
    bb, t, _ = q_ref.shape
    wb = ck_ref.shape[1]
    dkv = kv_heads * head_dim
    q = q_ref[...]
    kc = jnp.concatenate([ck_ref[...], kvn_ref[:, :, :dkv]], axis=1)
    vc = jnp.concatenate([cv_ref[...], kvn_ref[:, :, dkv:]], axis=1)
    nk_ref[...] = kc[:, t:, :]
    nv_ref[...] = vc[:, t:, :]
    kcb = kc.astype(BF16)
    vcb = vc.astype(BF16)
    assert t & (t - 1) == 0
    i = lax.broadcasted_iota(jnp.int32, (group * t, wb + t), 0) & (t - 1)
    j = lax.broadcasted_iota(jnp.int32, (group * t, wb + t), 1)
    diff = wb + i - j
    mask = ((diff >= 0) & (diff < WINDOW))[None]
    outs = []
    for kv in range(kv_heads):
        hs = slice(kv * head_dim, (kv + 1) * head_dim)
        heads = [kv * group + g for g in range(group)]
        qs = jnp.concatenate([q[:, :, h * head_dim:(h + 1) * head_dim] for h in heads], axis=1)
        sink = jnp.concatenate([jnp.full((1, t, 1), sink_ref[h], F32) for h in heads], axis=1)
        s = jnp.einsum('bqd,bkd->bqk', qs, kcb[:, :, hs], preferred_element_type=F32)
        p = _sink_softmax(s, mask, sink)
        o = jnp.einsum('bqk,bkd->bqd', p.astype(BF16), vcb[:, :, hs], preferred_element_type=F32)
        outs.extend(o[:, g * t:(g + 1) * t, :] for g in range(group))
    o_ref[...] = jnp.concatenate(outs, axis=2).astype(o_ref.dtype)


def _swa_sample(sinks, q, kvn, ck, cv, *, kv_heads, head_dim):
    nbatch, t, dq = q.shape
    wb = ck.shape[1]
    group = dq // (kv_heads * head_dim)
    bb = SUBLANES if nbatch % SUBLANES == 0 else 1
    blk = lambda a: pl.BlockSpec((bb,) + a.shape[1:], lambda i: (i, 0, 0))
    return pl.pallas_call(
        functools.partial(_swa_sample_kernel, kv_heads=kv_heads, group=group, head_dim=head_dim),
        grid=(nbatch // bb,),
        in_specs=[pl.BlockSpec(memory_space=pltpu.SMEM), blk(q), blk(kvn), blk(ck), blk(cv)],
        out_specs=[blk(q), blk(ck), blk(cv)],
        out_shape=[jax.ShapeDtypeStruct(q.shape, BF16), jax.ShapeDtypeStruct(ck.shape, F32),
                   jax.ShapeDtypeStruct(cv.shape, F32)],
        compiler_params=_params(("parallel",)),
        name="swa_sample",
    )(sinks, q, kvn, ck, cv)


def _chunk_masks(r, c):
    assert r & (r - 1) == 0 and c & (c - 1) == 0 and LANES % r == 0
    width = 2 * LANES
    ri = lax.broadcasted_iota(jnp.int32, (r, width), 0)
    cj = lax.broadcasted_iota(jnp.int32, (r, width), 1)
    ci = cj & (r - 1)
    incl, strict = ci <= ri, ci < ri
    if r != c:
        shift = c.bit_length() - 1
        same = (ri >> shift) == (ci >> shift)
        incl, strict = same & incl, same & strict
    f01 = lambda mask: jnp.where(mask, 1.0, 0.0)
    block = cj >> (r.bit_length() - 1)
    first_in_tile = lax.broadcasted_iota(jnp.int32, (r, LANES), 1) < r
    return dict(incl=incl, strict=strict, eye=f01(ri == ci), first_in_tile=first_in_tile,
                keep=[f01(block == x).astype(BF16) for x in range(width // r)],
                incl_sel=f01(incl)[:, :r].astype(BF16),
                same_sel=None if r == c else f01(same)[:, :r].astype(BF16))


def _delta_gates(gb, masks, heads, dk):
    r = gb.shape[0]
    gc_all = _dot_sel(masks["incl_sel"], gb)
    if masks["same_sel"] is None:
        gl_all = jnp.broadcast_to(gc_all[r - 1:r, :], gc_all.shape)
    else:
        gl_all = _dot_sel(masks["same_sel"], gb)
    bc = lambda a, lane: jnp.broadcast_to(a[:, lane:lane + 1], (r, dk))
    return [(bc(gc_all, h), bc(gl_all, h), bc(gb, heads + h)) for h in range(heads)]


def _split2(x):
    hi = x.astype(BF16)
    return hi, (x - hi.astype(F32)).astype(BF16)


def _rows(xs):
    return jnp.concatenate(xs, axis=0)


def _lanes(xs):
    return xs[0] if len(xs) == 1 else jnp.concatenate(xs, axis=1)


def _block_diag(blocks):
    n = len(blocks)
    z = jnp.zeros(blocks[0].shape, BF16)
    return _rows([_lanes([blocks[i] if j == i else z for j in range(n)]) for i in range(n)])


def _delta_intra(insts, masks, c):
    r, dk = insts[0][0].shape
    dv = insts[0][2].shape[1]
    m = LANES // r
    n = 2 * m
    assert dk == LANES and len(insts) % n == 0
    incl, strict, eye, keep = masks["incl"], masks["strict"], masks["eye"], masks["keep"]
    tiles = [list(range(i, i + m)) for i in range(0, len(insts), m)]
    groups = [(2 * i, 2 * i + 1) for i in range(len(tiles) // 2)]
    qs, ks, vs, gcs, gls, betas = ([i[j] for i in insts] for j in range(6))
    bf = lambda x: x.astype(BF16)

    def lane_tile(xs):
        return xs[0] if m == 1 else jnp.where(masks["first_in_tile"], xs[0], xs[1])

    def block_diag_x(x):
        return _rows([x * keep[i] for i in range(n)])

    g_cols = [_lanes([lane_tile([gcs[i] for i in tiles[t]]) for t in g]) for g in groups]
    g_rows = [jnp.transpose(_rows([gcs[i] for t in g for i in tiles[t]]))[:r, :] for g in groups]
    decays = [jnp.where(incl, jnp.exp(gc - gr), 0.0) for gc, gr in zip(g_cols, g_rows)]
    egcs = [jnp.exp(gc) for gc in gcs]
    kbs = [k * beta for k, beta in zip(ks, betas)]
    kq = [_dot_nt(_rows([_lanes([bf(kbs[i]) for i in t]), _lanes([bf(qs[i]) for i in t])]),
                  _block_diag([bf(ks[i]) for i in t])) for t in tiles]
    ms = [jnp.where(strict, _lanes([kq[t][:r] for t in g]) * decay, 0.0) for g, decay in zip(groups, decays)]
    qks = [_lanes([kq[t][r:] for t in g]) * decay for g, decay in zip(groups, decays)]
    rounds = max((c - 1).bit_length(), 1)
    ps = [-m_ for m_ in ms]
    tinvs = [None] * len(groups)
    for j in range(rounds):
        square = j + 1 < rounds
        if not square and tinvs[0] is None:
            tinvs = [eye + p for p in ps]
            break
        p_hl = [_split2(p) for p in ps]
        weights = [block_diag_x(hi) for hi, _ in p_hl]
        lhs = [_rows((list(hl) if square else []) + ([] if t is None else list(_split2(t))))
               for hl, t in zip(p_hl, tinvs)]
        res = [_dot(a, w) for a, w in zip(lhs, weights)]
        new_ps, new_ts = [], []
        for p, t, a in zip(ps, tinvs, res):
            o = 0
            if square:
                new_ps.append(a[:r] + a[r:2 * r])
                o = 2 * r
            new_ts.append(eye + p if t is None else t + (a[o:o + r] + a[o + r:o + 2 * r]))
        ps, tinvs = new_ps if square else ps, new_ts
    rhs = [jnp.concatenate([kb * egc, v * beta], axis=1).astype(BF16)
           for kb, egc, v, beta in zip(kbs, egcs, vs, betas)]
    t_hl = [_split2(t) for t in tinvs]
    sols = []
    for ti, t in enumerate(tiles):
        hi, lo = (x[:, (ti % 2) * LANES:(ti % 2 + 1) * LANES] for x in t_hl[ti // 2])
        a = _dot(_rows([hi, lo]), _block_diag([rhs[i] for i in t]))
        sols.append(a[:r] + a[r:])
    out = []
    for ti, t in enumerate(tiles):
        per_inst = []
        for slot, i in enumerate(t):
            s_ = sols[ti][:, slot * (dk + dv):(slot + 1) * (dk + dv)]
            per_inst.append((s_[:, :dk], s_[:, dk:], qs[i] * egcs[i], ks[i] * jnp.exp(gls[i] - gcs[i]),
                             jnp.exp(gls[i])))
        out.append((qks[ti // 2][:, (ti % 2) * LANES:(ti % 2 + 1) * LANES], per_inst))
    return out


def _gated_out_norm(o, w, z):
    return (_rms(o, w) * _silu(z)).astype(BF16)


def _head_cols(xb, h, heads, dk, dv):
    q = xb[:, h * dk:(h + 1) * dk]
    k = xb[:, (heads + h) * dk:(heads + h + 1) * dk]
    v = xb[:, 2 * heads * dk + h * dv:2 * heads * dk + (h + 1) * dv]
    return q, k, v


def _delta_prompt_intra_kernel(x_ref, gb_ref, kq_ref, vu_ref, kd_ref, qk_ref, gt_ref, *,
                               heads, dk, dv, chunk):
    rows = x_ref.shape[0]
    xb = x_ref[...]
    gb = gb_ref[...]
    masks = _chunk_masks(chunk, chunk)
    insts = []
    for g in range(rows // chunk):
        rs = slice(g * chunk, (g + 1) * chunk)
        gates = _delta_gates(gb[rs], masks, heads, dk)
        insts.extend(_head_cols(xb[rs], h, heads, dk, dv) + gates[h] for h in range(heads))
    intra = _delta_intra(insts, masks, chunk)

    tiles_per_chunk = len(intra) // (rows // chunk)
    for ti, (qk, per_inst) in enumerate(intra):
        g, tc = divmod(ti, tiles_per_chunk)
        qk_ref[0, g, tc] = qk.astype(BF16)
        for x, (k_cum, v_u, q_dec, k_dec, g_tot) in enumerate(per_inst):
            h = tc * len(per_inst) + x
            kq_ref[0, g, h] = _rows([k_cum, q_dec]).astype(BF16)
            vu_ref[0, g, h] = v_u
            kd_ref[0, g, h] = k_dec.astype(BF16)
            gt_ref[0, g, h] = g_tot[0:SUBLANES, :]


def _delta_prompt_state_kernel(kq_ref, vu_ref, kd_ref, qk_ref, gt_ref, z_ref, nw_ref, o_ref, s_out_ref,
                               s_ref):
    n = pl.program_id(0)
    batch, _, heads, c, dv = vu_ref.shape
    tiles = qk_ref.shape[2]
    m = heads // tiles

    @pl.when(n == 0)
    def _():
        s_ref[...] = jnp.zeros(s_ref.shape, F32)

    inst = [(b, h) for b in range(batch) for h in range(heads)]
    state = [s_ref[b, h] for b, h in inst]
    res = [_dot(kq_ref[b, 0, h], s.astype(BF16)) for (b, h), s in zip(inst, state)]
    v_new = [vu_ref[b, 0, h] - r_[:c] for (b, h), r_ in zip(inst, res)]
    vn_bf = [vn.astype(BF16) for vn in v_new]
    o_intra = [_dot(qk_ref[b, 0, t], _block_diag(vn_bf[(b * tiles + t) * m:(b * tiles + t + 1) * m]))
               for b in range(batch) for t in range(tiles)]
    for i, (b, h) in enumerate(inst):
        s_ref[b, h] = state[i] * gt_ref[b, 0, h][0:1, :] + _dot_tn(kd_ref[b, 0, h], vn_bf[i])
    for i, (b, h) in enumerate(inst):
        o = res[i][c:] + o_intra[i // m][:, (i % m) * dv:(i % m + 1) * dv]
        cs = slice(h * dv, (h + 1) * dv)
        o_ref[b, 0, :, cs] = _gated_out_norm(o, nw_ref[...], z_ref[b, 0, :, cs])

    @pl.when(n == pl.num_programs(0) - 1)
    def _():
        s_out_ref[...] = s_ref[...]


def _delta_prompt(x, z, gb, norm_w, *, batch, heads, dk, dv):
    t, cdim = x.shape
    seq = t // batch
    c = DN_CHUNK if seq % DN_CHUNK == 0 else seq
    assert c % SUBLANES == 0 and dk == dv == LANES
    nc = seq // c
    group = PROMPT_CHUNKS_PER_STEP if nc % PROMPT_CHUNKS_PER_STEP == 0 else 1
    rows = group * c
    steps = seq // rows
    tiles = heads * c // LANES
    blk = lambda n_: pl.BlockSpec((rows, n_), lambda b, n: (b * steps + n, 0))
    per_chunk = lambda per, r_, dt: (
        pl.BlockSpec((1, group, per, r_, LANES), lambda b, n: (b, n, 0, 0, 0)),
        jax.ShapeDtypeStruct((batch, nc, per, r_, LANES), dt))
    outs = [per_chunk(heads, 2 * c, BF16), per_chunk(heads, c, F32), per_chunk(heads, c, BF16),
            per_chunk(tiles, c, BF16), per_chunk(heads, SUBLANES, F32)]
    kq, vu, kd, qk, gt = pl.pallas_call(
        functools.partial(_delta_prompt_intra_kernel, heads=heads, dk=dk, dv=dv, chunk=c),
        grid=(batch, steps),
        in_specs=[blk(cdim), blk(gb.shape[1])],
        out_specs=[o[0] for o in outs],
        out_shape=[o[1] for o in outs],
        compiler_params=_params(("parallel", "parallel")),
        name="delta_prompt_intra",
    )(x, gb)
    all_seq = lambda a: pl.BlockSpec((batch, 1) + a.shape[2:], lambda n: (0, n) + (0,) * (a.ndim - 2))
    z4 = z.reshape(batch, nc, c, heads * dv)
    o, s_out = pl.pallas_call(
        _delta_prompt_state_kernel,
        grid=(nc,),
        in_specs=[all_seq(kq), all_seq(vu), all_seq(kd), all_seq(qk), all_seq(gt), all_seq(z4),
                  _const_spec(norm_w.shape)],
        out_specs=[all_seq(z4), _const_spec((batch, heads, dk, dv))],
        out_shape=[jax.ShapeDtypeStruct(z4.shape, BF16), jax.ShapeDtypeStruct((batch, heads, dk, dv), F32)],
        scratch_shapes=[pltpu.VMEM((batch, heads, dk, dv), F32)],
        compiler_params=_params(("arbitrary",)),
        name="delta_prompt_state",
    )(kq, vu, kd, qk, gt, z4, norm_w)
    return o.reshape(t, heads * dv), s_out


def _delta_sample_kernel(x_ref, cbuf_ref, z_ref, gb_ref, s_in_ref, cw_ref, nw_ref, o_ref, s_out_ref,
                         xc_ref, *, heads, dk, dv, conv_w):
    nb, hist, cdim = cbuf_ref.shape
    r = x_ref.shape[0]
    t = r // nb
    xc_ref[:, SUBLANES:SUBLANES + t, :] = x_ref[...].reshape(nb, t, cdim)
    xc_ref[:, SUBLANES - hist:SUBLANES, :] = cbuf_ref[...]
    acc = None
    for jj in range(conv_w):
        off = SUBLANES - (conv_w - 1) + jj
        term = xc_ref[:, off:off + t, :] * cw_ref[jj:jj + 1, :][None]
        acc = term if acc is None else acc + term
    xb = _silu(acc).reshape(r, cdim)

    masks = _chunk_masks(r, t)
    gates = _delta_gates(gb_ref[...], masks, heads, dk)
    insts = []
    for h in range(heads):
        q, k, v = _head_cols(xb, h, heads, dk, dv)
        insts.append((_unit(q) * (dk ** -0.5), _unit(k), v) + gates[h])
    intra = _delta_intra(insts, masks, t)
    rows = [slice(b * t, (b + 1) * t) for b in range(nb)]
    for h in range(heads):
        qk, ((k_cum, v_u, q_dec, k_dec, g_tot),) = intra[h]
        state = [s_in_ref[b, h] for b in range(nb)]
        res = [_dot(jnp.concatenate([k_cum[rs], q_dec[rs]], axis=0).astype(BF16), s.astype(BF16))
               for rs, s in zip(rows, state)]
        v_new = [v_u[rs] - r_[:t] for rs, r_ in zip(rows, res)]
        for b in range(nb):
            s_out_ref[b, h] = (state[b] * g_tot[b * t:b * t + 1, :]
                               + _dot_tn(k_dec[rows[b]].astype(BF16), v_new[b].astype(BF16)))
        o = (jnp.concatenate([r_[t:] for r_ in res], axis=0)
             + _dot(qk.astype(BF16), jnp.concatenate(v_new, axis=0).astype(BF16)))
        o_ref[:, h * dv:(h + 1) * dv] = _gated_out_norm(o, nw_ref[...], z_ref[:, h * dv:(h + 1) * dv])


def _delta_sample(x, cbuf, z, gb, s0, conv_w, norm_w, *, heads, dk, dv):
    rows, cdim = x.shape
    nbatch, hist, _ = cbuf.shape
    t = rows // nbatch
    assert t == SUBLANES and hist == conv_w.shape[0] - 1 and hist <= t and dk == dv
    nb = SAMPLE_GROUP if nbatch % SAMPLE_GROUP == 0 else 1
    r = nb * t
    blk = lambda n_: pl.BlockSpec((r, n_), lambda i: (i, 0))
    sblk = pl.BlockSpec((nb, heads, dk, dv), lambda i: (i, 0, 0, 0))
    return pl.pallas_call(
        functools.partial(_delta_sample_kernel, heads=heads, dk=dk, dv=dv, conv_w=conv_w.shape[0]),
        grid=(nbatch // nb,),
        in_specs=[blk(cdim), pl.BlockSpec((nb, hist, cdim), lambda i: (i, 0, 0)), blk(z.shape[1]),
                  blk(gb.shape[1]), sblk, _const_spec(conv_w.shape), _const_spec(norm_w.shape)],
        out_specs=[blk(heads * dv), sblk],
        out_shape=[jax.ShapeDtypeStruct((rows, heads * dv), BF16), jax.ShapeDtypeStruct(s0.shape, F32)],
        scratch_shapes=[pltpu.VMEM((nb, 2 * SUBLANES, cdim), F32)],
        compiler_params=_params(("parallel",)),
        name="delta_sample",
    )(x, cbuf, z, gb, s0, conv_w, norm_w)


def _layer(xp, xs, ck, cv, cconv, sdelta, lw, dims):
    (f1_pre, f1_post, f1_up, f1_down, mix_pre, mix_post, w_in, sinks, conv_w, a_log, dt_bias,
     out_norm, w_a, w_b, w_o, f2_pre, f2_post, f2_up, f2_down) = lw
    batch, seq, d = xp.shape
    dbatch, dseq, _ = xs.shape
    kvh, hd = dims["kv_heads"], dims["head_dim"]
    hb, dk, dv = dims["b_heads"], dims["dk"], dims["dv"]
    a_heads = w_a.shape[0] // hd
    cdim = conv_w.shape[1]
    d_ff = f1_down.shape[0]
    bf = lambda a: a.astype(BF16)
    row = lambda a: a.reshape(1, -1).astype(F32)

    sizes = (a_heads * hd, kvh * hd, kvh * hd, cdim, hb * dv, hb, hb, d, d)
    offs = [0]
    for s_ in sizes:
        offs.append(offs[-1] + s_)
    w_in_bf = bf(w_in)
    wga, wgb = w_in_bf[:, offs[7]:offs[8]], w_in_bf[:, offs[8]:offs[9]]
    proj_offs = (offs[0], offs[1], offs[3], offs[4], offs[5])
    alog = jnp.pad(row(a_log), ((0, 0), (0, LANES - hb)))
    dtb = jnp.pad(row(dt_bias), ((0, 0), (0, LANES - hb)))

    def ffn(xa, xb, pre, post, up, down):
        return _ffn(xa, xb, row(pre), row(post), bf(up), bf(down))

    def inproj(x, **conv):
        return _inproj(x, row(mix_pre), w_in_bf, alog, dtb, q_scale=hd ** -0.5, n_heads_b=hb,
                       offs=proj_offs, **conv)

    def merge(group_a, group_b):
        return _merge(group_a, group_b, row(mix_pre), row(mix_post), wga, wgb, bf(w_a), bf(w_b), bf(w_o))

    sinks = sinks.astype(F32)
    norm_w = row(out_norm)
    conv_w = conv_w.astype(F32)
    hist = conv_w.shape[0] - 1

    x, x_s = ffn(xp.reshape(batch * seq, d), xs.reshape(dbatch * dseq, d), f1_pre, f1_post, f1_up, f1_down)
    q, kv, xb, z, gb, tail = inproj(x, conv_w=conv_w, seq_rows=seq, dk=dk)
    oa, kv_last = _swa_prompt(sinks, q, kv, batch=batch, kv_heads=kvh, head_dim=hd)
    ob, p_state = _delta_prompt(xb, z, gb, norm_w, batch=batch, heads=hb, dk=dk, dv=dv)
    prompt_mix = (x, oa, ob)
    kv_last = kv_last.reshape(batch, WINDOW, 2, kvh, hd)
    p_k, p_v = kv_last[:, :, 0], kv_last[:, :, 1]
    p_conv = tail.reshape(batch, -1, SUBLANES, cdim)[:, -1, SUBLANES - hist:]

    x = x_s
    q, kv, xb, z, gb = inproj(x)
    wb = ck.shape[1]
    oa, s_k, s_v = _swa_sample(sinks, q.reshape(dbatch, dseq, -1), kv.reshape(dbatch, dseq, -1),
                               ck.reshape(dbatch, wb, kvh * hd), cv.reshape(dbatch, wb, kvh * hd),
                               kv_heads=kvh, head_dim=hd)
    ob, s_state = _delta_sample(xb, cconv, z, gb, sdelta, conv_w, norm_w, heads=hb, dk=dk, dv=dv)
    x_p, x_s = merge(prompt_mix, (x, oa.reshape(dbatch * dseq, -1), ob))
    yp, ys = ffn(x_p, x_s, f2_pre, f2_post, f2_up, f2_down)
    yp, ys = yp.reshape(batch, seq, d), ys.reshape(dbatch, dseq, d)
    s_k = s_k.reshape(dbatch, wb, kvh, hd)
    s_v = s_v.reshape(dbatch, wb, kvh, hd)
    s_conv = xb.reshape(dbatch, dseq, cdim)[:, dseq - hist:]
    return yp, ys, (p_k, p_v, p_conv, p_state), (s_k, s_v, s_conv, s_state)


def kernel(x_prompt, x_sample, cache_swa_k, cache_swa_v, state_conv, state_delta, ffn1_norm_pre, ffn1_norm_post, ffn1_w_up, ffn1_w_down, mix_norm_pre, mix_norm_post, w_in, attn_sinks, conv_w, dn_a_log, dn_dt_bias, dn_out_norm, w_branch_a, w_branch_b, w_out, ffn2_norm_pre, ffn2_norm_post, ffn2_w_up, ffn2_w_down):
    weights = (ffn1_norm_pre, ffn1_norm_post, ffn1_w_up, ffn1_w_down, mix_norm_pre, mix_norm_post,
               w_in, attn_sinks, conv_w, dn_a_log, dn_dt_bias, dn_out_norm, w_branch_a, w_branch_b,
               w_out, ffn2_norm_pre, ffn2_norm_post, ffn2_w_up, ffn2_w_down)
    depth = w_in.shape[0]
    dims = dict(kv_heads=cache_swa_k.shape[3], head_dim=cache_swa_k.shape[4],
                b_heads=state_delta.shape[2], dk=state_delta.shape[3], dv=state_delta.shape[4])
    yp, ys = x_prompt, x_sample
    p_out, s_out = [], []
    for l in range(depth):
        yp, ys, p_new, s_new = _layer(yp, ys, cache_swa_k[l], cache_swa_v[l], state_conv[l],
                                      state_delta[l], tuple(w[l] for w in weights), dims)
        p_out.append(p_new)
        s_out.append(s_new)
    stack = lambda outs, i: jnp.stack([o[i] for o in outs])
    return (yp, ys,
            stack(p_out, 0), stack(p_out, 1), stack(p_out, 2), stack(p_out, 3),
            stack(s_out, 0), stack(s_out, 1), stack(s_out, 2), stack(s_out, 3))
```

```python
import functools

import jax
import jax.numpy as jnp
from jax import lax
from jax.experimental import pallas as pl
from jax.experimental.pallas import tpu as pltpu

WINDOW = 128
DN_CHUNK = 64
EPS = 1e-6
F32 = jnp.float32
BF16 = jnp.bfloat16
NEG_BIG = -1e30
LANES = 128
SUBLANES = 8
VMEM_LIMIT_BYTES = 56 * 1024 * 1024
FF_CHUNK = 256
MXU_COLS = 256
CONV_ROW_BLOCK = 128
SAMPLE_GROUP = 16
SWA_SAMPLE_GROUP = 16
SWA_BLOCKS_PER_STEP = 4
STATE_CHUNKS_PER_STEP = 2
PROMPT_CHUNKS_PER_STEP = 4


def _rms(x, w):
    return x * lax.rsqrt(jnp.mean(x * x, axis=-1, keepdims=True) + EPS) * w


def _silu(x):
    return x * jax.nn.sigmoid(x)


def _unit(x):
    return x * lax.rsqrt(jnp.sum(x * x, axis=-1, keepdims=True) + EPS)


def _softplus(x):
    return jnp.maximum(x, 0.0) + jnp.log1p(jnp.exp(-jnp.abs(x)))


def _dot(a, b):
    return jnp.dot(a, b, preferred_element_type=F32)


def _dot_nt(a, b):
    return lax.dot_general(a, b, (((1,), (1,)), ((), ())), preferred_element_type=F32)


def _dot_tn(a, b):
    return lax.dot_general(a, b, (((0,), (0,)), ((), ())), preferred_element_type=F32)


def _split3(b):
    hi = b.astype(BF16)
    r = b - hi.astype(F32)
    mid = r.astype(BF16)
    lo = (r - mid.astype(F32)).astype(BF16)
    return hi, mid, lo


def _dot_sel(sel_bf16, b):
    hi, mid, lo = _split3(b)
    return _dot(sel_bf16, hi) + _dot(sel_bf16, mid) + _dot(sel_bf16, lo)


def _ffn_tile(x_ref, pre_ref, post_ref, wup_ref, wd_ref, o_ref):
    x = x_ref[...]
    xn = _rms(x, pre_ref[...]).astype(BF16)
    d_ff = wd_ref.shape[0]
    acc = jnp.zeros(x.shape, F32)
    for c in range(d_ff // FF_CHUNK):
        sl = slice(c * FF_CHUNK, (c + 1) * FF_CHUNK)
        g = _dot(xn, wup_ref[:, sl])
        u = _dot(xn, wup_ref[:, d_ff + c * FF_CHUNK:d_ff + (c + 1) * FF_CHUNK])
        acc = acc + _dot((_silu(g) * u).astype(BF16), wd_ref[sl, :])
    o_ref[...] = x + 0.5 * _rms(acc, post_ref[...])


def _ffn_kernel(xa_ref, xb_ref, pre_ref, post_ref, wup_ref, wd_ref, oa_ref, ob_ref, *, tiles_a):
    i = pl.program_id(0)

    @pl.when(i < tiles_a)
    def _():
        _ffn_tile(xa_ref, pre_ref, post_ref, wup_ref, wd_ref, oa_ref)

    @pl.when(i >= tiles_a)
    def _():
        _ffn_tile(xb_ref, pre_ref, post_ref, wup_ref, wd_ref, ob_ref)


def _const_spec(shape):
    nd = len(shape)
    return pl.BlockSpec(shape, lambda *_: (0,) * nd)


def _weight_spec(shape):
    nd = len(shape)
    return pl.BlockSpec(shape, lambda *_: (0,) * nd, pipeline_mode=pl.Buffered(1))


def _row_tile(t, largest=512):
    for tm in (1024, 512, 256, 128, 64, 32, 16, 8):
        if tm <= largest and t % tm == 0:
            return tm
    raise ValueError(f"token count {t} is not a multiple of 8")


def _params(sem):
    return pltpu.CompilerParams(dimension_semantics=sem, vmem_limit_bytes=VMEM_LIMIT_BYTES)


def _ffn(xa, xb, pre, post, wup, wd):
    (ta, d), tb = xa.shape, xb.shape[0]
    f = wd.shape[0]
    assert f % FF_CHUNK == 0 and wup.shape == (d, 2 * f)
    tm = min(_row_tile(ta), _row_tile(tb))
    na, nb = ta // tm, tb // tm
    row_a = pl.BlockSpec((tm, d), lambda i: (jnp.minimum(i, na - 1), 0))
    row_b = pl.BlockSpec((tm, d), lambda i: (jnp.maximum(i - na, 0), 0))
    return pl.pallas_call(
        functools.partial(_ffn_kernel, tiles_a=na),
        grid=(na + nb,),
        in_specs=[row_a, row_b, _const_spec((1, d)), _const_spec((1, d)),
                  _weight_spec((d, 2 * f)), _weight_spec((f, d))],
        out_specs=[row_a, row_b],
        out_shape=[jax.ShapeDtypeStruct((ta, d), F32), jax.ShapeDtypeStruct((tb, d), F32)],
        compiler_params=_params(("arbitrary",)),
        name="ffn",
    )(xa, xb, pre, post, wup, wd)


def _conv_silu_unit(xc_ref, cv_ref, cw_ref, r0, rows, cb, heads, dk):
    taps = cw_ref.shape[0]
    rs = slice(r0, r0 + rows)
    cs = slice(cb * dk, (cb + 1) * dk)
    acc = None
    for jj in range(taps):
        off = r0 + SUBLANES - (taps - 1) + jj
        term = xc_ref[off:off + rows, cs] * cw_ref[jj:jj + 1, cs]
        acc = term if acc is None else acc + term
    cv_ref[rs, cs] = acc
    y = _silu(cv_ref[rs, cs])
    if cb < 2 * heads:
        y = _unit(y) * (dk ** -0.5) if cb < heads else _unit(y)
    cv_ref[rs, cs] = y


def _inproj_kernel(x_ref, pre_ref, w_ref, alog_ref, dtb_ref, *rest, q_scale, n_heads_b, offs, conv):
    tm = x_ref.shape[0]
    h = _rms(x_ref[...], pre_ref[...]).astype(BF16)
    if conv is None:
        q_ref, kv_ref, b_ref, z_ref, gb_ref = rest
        sub = tm
    else:
        cw_ref, q_ref, kv_ref, b_ref, z_ref, gb_ref, tail_ref, xc_ref = rest
        seq_rows, dk = conv
        sub = min(tm, CONV_ROW_BLOCK)

        @pl.when((pl.program_id(0) * tm) % seq_rows == 0)
        def _():
            xc_ref[0:SUBLANES, :] = jnp.zeros((SUBLANES, xc_ref.shape[1]), F32)

    def project(r0, i, c0, width):
        rs = slice(r0, r0 + sub)
        cs = slice(c0, c0 + width)
        y = _dot(h[rs], w_ref[:, offs[i] + c0:offs[i] + c0 + width])
        if i == 0:
            q_ref[rs, cs] = (y * q_scale).astype(BF16)
        elif i == 1:
            kv_ref[rs, cs] = y
        elif i == 2 and conv is None:
            b_ref[rs, cs] = y
        elif i == 2:
            xc_ref[SUBLANES + r0:SUBLANES + r0 + sub, cs] = y
        elif i == 3:
            z_ref[rs, cs] = y
        else:
            g = -jnp.exp(alog_ref[...]) * _softplus(y + dtb_ref[...])
            lane = lax.broadcasted_iota(jnp.int32, y.shape, 1)
            gb_ref[rs, :] = jnp.where(lane < n_heads_b, g, jax.nn.sigmoid(y))

    widths = [offs[i + 1] - offs[i] for i in range(4)] + [LANES]
    pieces = [(i, c0, min(MXU_COLS, widths[i] - c0))
              for i in (2, 0, 1, 3, 4) for c0 in range(0, widths[i], MXU_COLS)]
    for r0 in range(0, tm + (sub if conv is not None else 0), sub):
        mm = [functools.partial(project, r0, *p) for p in pieces] if r0 < tm else []
        cv = []
        if conv is not None and r0 > 0:
            cv = [functools.partial(_conv_silu_unit, xc_ref, b_ref, cw_ref, r0 - sub, sub, cb, n_heads_b, dk)
                  for cb in range(xc_ref.shape[1] // dk)]
        for k in range(max(len(mm), len(cv))):
            if k < len(mm):
                mm[k]()
            if k < len(cv):
                cv[k]()
    if conv is not None:
        tail_ref[0] = xc_ref[tm:tm + SUBLANES, :]
        xc_ref[0:SUBLANES, :] = xc_ref[tm:tm + SUBLANES, :]


def _inproj(x, pre, w, alog, dtb, *, q_scale, n_heads_b, offs, conv_w=None, seq_rows=None, dk=None):
    t, d = x.shape
    tm = _row_tile(t)
    assert all(o % LANES == 0 for o in offs[:5]) and offs[4] + LANES <= w.shape[1]
    row = lambda n: pl.BlockSpec((tm, n), lambda i: (i, 0))
    cdim = offs[3] - offs[2]
    outs = [(offs[1] - offs[0], BF16), (offs[2] - offs[1], F32), (cdim, F32), (offs[4] - offs[3], F32),
            (LANES, F32)]
    in_specs = [row(d), _const_spec((1, d)), _weight_spec(w.shape),
                _const_spec(alog.shape), _const_spec(dtb.shape)]
    out_specs = [row(n) for n, _ in outs]
    out_shape = [jax.ShapeDtypeStruct((t, n), dt) for n, dt in outs]
    args, scratch, conv = [x, pre, w, alog, dtb], [], None
    if conv_w is not None:
        assert seq_rows % tm == 0 and conv_w.shape[0] - 1 <= SUBLANES
        in_specs.append(_const_spec(conv_w.shape))
        args.append(conv_w)
        out_specs.append(pl.BlockSpec((1, SUBLANES, cdim), lambda i: (i, 0, 0)))
        out_shape.append(jax.ShapeDtypeStruct((t // tm, SUBLANES, cdim), F32))
        scratch = [pltpu.VMEM((tm + SUBLANES, cdim), F32)]
        conv = (seq_rows, dk)
    return pl.pallas_call(
        functools.partial(_inproj_kernel, q_scale=q_scale, n_heads_b=n_heads_b, offs=offs, conv=conv),
        grid=(t // tm,),
        in_specs=in_specs,
        out_specs=out_specs,
        out_shape=out_shape,
        scratch_shapes=scratch,
        compiler_params=_params(("arbitrary",)),
        name="inproj",
    )(*args)


def _merge_tile(x_ref, oa_ref, ob_ref, pre_ref, post_ref, wga_ref, wgb_ref, wa_ref, wb_ref, wo_ref,
                o_ref):
    x = x_ref[...]
    h = _rms(x, pre_ref[...]).astype(BF16)
    oa, ob = oa_ref[...], ob_ref[...]
    m = jnp.zeros(x.shape, F32)
    for c0 in range(0, wo_ref.shape[0], MXU_COLS):
        cs = slice(c0, c0 + MXU_COLS)
        merged = (jax.nn.sigmoid(_dot(h, wga_ref[:, cs])) * _dot(oa, wa_ref[:, cs])
                  + jax.nn.sigmoid(_dot(h, wgb_ref[:, cs])) * _dot(ob, wb_ref[:, cs]))
        m = m + _dot(merged.astype(BF16), wo_ref[cs, :])
    o_ref[...] = x + _rms(m, post_ref[...])


def _merge_kernel(xa_ref, oaa_ref, oba_ref, xb_ref, oab_ref, obb_ref, *rest, tiles_a):
    *shared, oa_ref, ob_ref = rest
    i = pl.program_id(0)

    @pl.when(i < tiles_a)
    def _():
        _merge_tile(xa_ref, oaa_ref, oba_ref, *shared, oa_ref)

    @pl.when(i >= tiles_a)
    def _():
        _merge_tile(xb_ref, oab_ref, obb_ref, *shared, ob_ref)


def _merge(group_a, group_b, pre, post, wga, wgb, wa, wb, wo):
    (ta, d), tb = group_a[0].shape, group_b[0].shape[0]
    tm = min(_row_tile(ta, largest=1024), _row_tile(tb, largest=1024))
    na, nb = ta // tm, tb // tm
    rows_a = [pl.BlockSpec((tm, a.shape[1]), lambda i: (jnp.minimum(i, na - 1), 0)) for a in group_a]
    rows_b = [pl.BlockSpec((tm, a.shape[1]), lambda i: (jnp.maximum(i - na, 0), 0)) for a in group_b]
    weights = (wga, wgb, wa, wb, wo)
    return pl.pallas_call(
        functools.partial(_merge_kernel, tiles_a=na),
        grid=(na + nb,),
        in_specs=rows_a + rows_b + [_const_spec((1, d)), _const_spec((1, d))]
        + [_weight_spec(w.shape) for w in weights],
        out_specs=[rows_a[0], rows_b[0]],
        out_shape=[jax.ShapeDtypeStruct((ta, d), F32), jax.ShapeDtypeStruct((tb, d), F32)],
        compiler_params=_params(("arbitrary",)),
        name="merge",
    )(*group_a, *group_b, pre, post, *weights)


def _sink_softmax(s, mask, sink):
    s = jnp.where(mask, s, NEG_BIG)
    m = jnp.maximum(jnp.max(s, axis=-1, keepdims=True), sink)
    e = jnp.exp(s - m)
    return e / (jnp.sum(e, axis=-1, keepdims=True) + jnp.exp(sink - m))


def _swa_prompt_kernel(sink_ref, q_ref, kvp_ref, kvc_ref, o_ref, kv_last_ref, *, kv_heads, group,
                       head_dim):
    n = pl.program_id(1)
    w = WINDOW
    blocks = q_ref.shape[0] // w
    dkv = kv_heads * head_dim
    pairs = group // 2
    kv_all = jnp.concatenate([kvp_ref[...], kvc_ref[...]], axis=0)

    @pl.when(n == pl.num_programs(1) - 1)
    def _():
        kv_last_ref[0] = kvc_ref[(blocks - 1) * w:blocks * w, :]
    lo = lax.broadcasted_iota(jnp.int32, (kv_all.shape[0], dkv), 1) < head_dim

    def lane_halves(x):
        x_rot = pltpu.roll(x, head_dim, axis=1)
        pick = lambda kv, half: jnp.where(lo if half == 0 else ~lo, x if kv == half else x_rot, 0.0)
        return [[pick(kv, half).astype(BF16) for half in range(2)] for kv in range(kv_heads)]

    k_ext = lane_halves(kv_all[:, :dkv])
    v_ext = lane_halves(kv_all[:, dkv:])

    i = lax.broadcasted_iota(jnp.int32, (w, 2 * w), 0)
    j = lax.broadcasted_iota(jnp.int32, (w, 2 * w), 1)
    bias = jnp.where(j > i, jnp.where(j <= i + w, 0.0, NEG_BIG), NEG_BIG)
    bias_seq_start = jnp.where(j >= w, bias, NEG_BIG)
    lo_out = lax.broadcasted_iota(jnp.int32, (w, 2 * head_dim), 1) < head_dim

    ones_half = [jnp.where(lo if half == 0 else ~lo, 1.0, 0.0).astype(BF16) for half in range(2)]
    v_ext = [[jnp.concatenate([v_ext[kv][half], ones_half[half]], axis=1) for half in range(2)]
             for kv in range(kv_heads)]

    inst = [(t, kv) for t in range(blocks) for kv in range(kv_heads)]
    band = lambda x, t: x[t * w:(t + 2) * w]
    q2 = [jnp.concatenate([q_ref[t * w:(t + 1) * w, (kv * pairs + p) * LANES:(kv * pairs + p + 1) * LANES]
                           for p in range(pairs)], axis=0) for t, kv in inst]
    scores = [[_dot_nt(q_, band(k_ext[kv][half], t)) for half in range(2)]
              for q_, (t, kv) in zip(q2, inst)]
    bias_first = jnp.where(n > 0, bias, bias_seq_start)
    for (t, kv), sc in zip(inst, scores):
        vband = jnp.concatenate([band(v_ext[kv][0], t), band(v_ext[kv][1], t)], axis=0)
        for p in range(pairs):
            exps, sink_terms = [], []
            for half in range(2):
                sink = sink_ref[kv * group + 2 * p + half]
                s = sc[half][p * w:(p + 1) * w] + (bias_first if t == 0 else bias)
                m = jnp.maximum(jnp.max(s, axis=-1, keepdims=True), sink)
                exps.append(jnp.exp(s - m).astype(BF16))
                sink_terms.append(jnp.exp(sink - m))
            o = _dot(jnp.concatenate(exps, axis=1), vband)
            den = o[:, LANES:] + jnp.where(lo_out, sink_terms[0], sink_terms[1])
            cs = slice((kv * pairs + p) * LANES, (kv * pairs + p + 1) * LANES)
            o_ref[t * w:(t + 1) * w, cs] = (o[:, :LANES] / den).astype(o_ref.dtype)


def _swa_prompt(sinks, q, kv, *, batch, kv_heads, head_dim):
    t, dq = q.shape
    seq = t // batch
    assert seq % WINDOW == 0
    group = dq // (kv_heads * head_dim)
    assert 2 * head_dim == LANES and kv_heads == 2 and group % 2 == 0
    blocks = SWA_BLOCKS_PER_STEP if (seq // WINDOW) % SWA_BLOCKS_PER_STEP == 0 else 1
    nb = seq // (blocks * WINDOW)
    rows = blocks * WINDOW
    return pl.pallas_call(
        functools.partial(_swa_prompt_kernel, kv_heads=kv_heads, group=group, head_dim=head_dim),
        grid=(batch, nb),
        in_specs=[pl.BlockSpec(memory_space=pltpu.SMEM),
                  pl.BlockSpec((rows, dq), lambda b, n: (b * nb + n, 0)),
                  pl.BlockSpec((WINDOW, kv.shape[1]),
                               lambda b, n: ((b * nb + n) * blocks - jnp.minimum(n, 1), 0)),
                  pl.BlockSpec((rows, kv.shape[1]), lambda b, n: (b * nb + n, 0))],
        out_specs=[pl.BlockSpec((rows, dq), lambda b, n: (b * nb + n, 0)),
                   pl.BlockSpec((1, WINDOW, kv.shape[1]), lambda b, n: (b, 0, 0))],
        out_shape=[jax.ShapeDtypeStruct((t, dq), BF16),
                   jax.ShapeDtypeStruct((batch, WINDOW, kv.shape[1]), F32)],
        compiler_params=_params(("parallel", "arbitrary")),
        name="swa_prompt",
    )(sinks, q, kv, kv)


def _swa_sample_kernel(sink_ref, q_ref, kvn_ref, ck_ref, cv_ref, o_ref, nk_ref, nv_ref, *,
                       kv_heads, group, head_dim):
    bb, t, _ = q_ref.shape
    wb = ck_ref.shape[1]
    dkv = kv_heads * head_dim
    q = q_ref[...]
    kc = jnp.concatenate([ck_ref[...], kvn_ref[:, :, :dkv]], axis=1)
    vc = jnp.concatenate([cv_ref[...], kvn_ref[:, :, dkv:]], axis=1)
    nk_ref[...] = kc[:, t:, :]
    nv_ref[...] = vc[:, t:, :]
    kcb = kc.astype(BF16)
    vcb = vc.astype(BF16)
    assert t & (t - 1) == 0
    i = lax.broadcasted_iota(jnp.int32, (group * t, wb + t), 0) & (t - 1)
    j = lax.broadcasted_iota(jnp.int32, (group * t, wb + t), 1)
    diff = wb + i - j
    mask = ((diff >= 0) & (diff < WINDOW))[None]
    hs = [slice(kv * head_dim, (kv + 1) * head_dim) for kv in range(kv_heads)]
    heads = [[kv * group + g for g in range(group)] for kv in range(kv_heads)]
    qs = [jnp.concatenate([q[:, :, h * head_dim:(h + 1) * head_dim] for h in hd_], axis=1) for hd_ in heads]
    sinks = [jnp.concatenate([jnp.full((1, t, 1), sink_ref[h], F32) for h in hd_], axis=1) for hd_ in heads]
    scores = [jnp.einsum('bqd,bkd->bqk', q_, kcb[:, :, h_], preferred_element_type=F32)
              for q_, h_ in zip(qs, hs)]
    probs = [_sink_softmax(s, mask, sink).astype(BF16) for s, sink in zip(scores, sinks)]
    outs = [jnp.einsum('bqk,bkd->bqd', p, vcb[:, :, h_], preferred_element_type=F32)
            for p, h_ in zip(probs, hs)]
    o_ref[...] = jnp.concatenate([o[:, g * t:(g + 1) * t, :] for o in outs for g in range(group)],
                                 axis=2).astype(o_ref.dtype)


def _swa_sample(sinks, q, kvn, ck, cv, *, kv_heads, head_dim):
    nbatch, t, dq = q.shape
    wb = ck.shape[1]
    group = dq // (kv_heads * head_dim)
    bb = SWA_SAMPLE_GROUP if nbatch % SWA_SAMPLE_GROUP == 0 else 1
    blk = lambda a: pl.BlockSpec((bb,) + a.shape[1:], lambda i: (i, 0, 0))
    return pl.pallas_call(
        functools.partial(_swa_sample_kernel, kv_heads=kv_heads, group=group, head_dim=head_dim),
        grid=(nbatch // bb,),
        in_specs=[pl.BlockSpec(memory_space=pltpu.SMEM), blk(q), blk(kvn), blk(ck), blk(cv)],
        out_specs=[blk(q), blk(ck), blk(cv)],
        out_shape=[jax.ShapeDtypeStruct(q.shape, BF16), jax.ShapeDtypeStruct(ck.shape, F32),
                   jax.ShapeDtypeStruct(cv.shape, F32)],
        compiler_params=_params(("parallel",)),
        name="swa_sample",
    )(sinks, q, kvn, ck, cv)


def _chunk_masks(r, c):
    assert r & (r - 1) == 0 and c & (c - 1) == 0 and LANES % r == 0
    width = 2 * LANES
    ri = lax.broadcasted_iota(jnp.int32, (r, width), 0)
    cj = lax.broadcasted_iota(jnp.int32, (r, width), 1)
    ci = cj & (r - 1)
    incl, strict = ci <= ri, ci < ri
    if r != c:
        shift = c.bit_length() - 1
        same = (ri >> shift) == (ci >> shift)
        incl, strict = same & incl, same & strict
    f01 = lambda mask: jnp.where(mask, 1.0, 0.0)
    block = cj >> (r.bit_length() - 1)
    first_in_tile = lax.broadcasted_iota(jnp.int32, (r, LANES), 1) < r
    return dict(incl=incl, strict=strict, eye=f01(ri == ci), first_in_tile=first_in_tile,
                keep=[f01(block == x).astype(BF16) for x in range(width // r)],
                incl_sel=f01(incl)[:, :r].astype(BF16),
                same_sel=None if r == c else f01(same)[:, :r].astype(BF16))


def _delta_gates(gb, masks, heads, dk):
    r = gb.shape[0]
    gc_all = _dot_sel(masks["incl_sel"], gb)
    if masks["same_sel"] is None:
        gl_all = jnp.broadcast_to(gc_all[r - 1:r, :], gc_all.shape)
    else:
        gl_all = _dot_sel(masks["same_sel"], gb)
    bc = lambda a, lane: jnp.broadcast_to(a[:, lane:lane + 1], (r, dk))
    return [(bc(gc_all, h), bc(gl_all, h), bc(gb, heads + h)) for h in range(heads)]


def _split2(x):
    hi = x.astype(BF16)
    return hi, (x - hi.astype(F32)).astype(BF16)


def _rows(xs):
    return jnp.concatenate(xs, axis=0)


def _lanes(xs):
    return xs[0] if len(xs) == 1 else jnp.concatenate(xs, axis=1)


def _block_diag(blocks):
    n = len(blocks)
    z = jnp.zeros(blocks[0].shape, BF16)
    return _rows([_lanes([blocks[i] if j == i else z for j in range(n)]) for i in range(n)])


def _delta_intra(insts, masks, c):
    r, dk = insts[0][0].shape
    dv = insts[0][2].shape[1]
    m = LANES // r
    n = 2 * m
    assert dk == LANES and len(insts) % n == 0
    incl, strict, eye, keep = masks["incl"], masks["strict"], masks["eye"], masks["keep"]
    tiles = [list(range(i, i + m)) for i in range(0, len(insts), m)]
    groups = [(2 * i, 2 * i + 1) for i in range(len(tiles) // 2)]
    qs, ks, vs, gcs, gls, betas = ([i[j] for i in insts] for j in range(6))
    bf = lambda x: x.astype(BF16)

    def lane_tile(xs):
        return xs[0] if m == 1 else jnp.where(masks["first_in_tile"], xs[0], xs[1])

    def block_diag_x(x):
        return _rows([x * keep[i] for i in range(n)])

    g_cols = [_lanes([lane_tile([gcs[i] for i in tiles[t]]) for t in g]) for g in groups]
    g_rows = [jnp.transpose(_rows([gcs[i] for t in g for i in tiles[t]]))[:r, :] for g in groups]
    decays = [jnp.where(incl, jnp.exp(gc - gr), 0.0) for gc, gr in zip(g_cols, g_rows)]
    egcs = [jnp.exp(gc) for gc in gcs]
    kbs = [k * beta for k, beta in zip(ks, betas)]
    kq = [_dot_nt(_rows([_lanes([bf(kbs[i]) for i in t]), _lanes([bf(qs[i]) for i in t])]),
                  _block_diag([bf(ks[i]) for i in t])) for t in tiles]
    ms = [jnp.where(strict, _lanes([kq[t][:r] for t in g]) * decay, 0.0) for g, decay in zip(groups, decays)]
    qks = [_lanes([kq[t][r:] for t in g]) * decay for g, decay in zip(groups, decays)]
    rounds = max((c - 1).bit_length(), 1)
    ps = [-m_ for m_ in ms]
    tinvs = [None] * len(groups)
    for j in range(rounds):
        square = j + 1 < rounds
        if not square and tinvs[0] is None:
            tinvs = [eye + p for p in ps]
            break
        p_hl = [_split2(p) for p in ps]
        weights = [block_diag_x(hi) for hi, _ in p_hl]
        lhs = [_rows((list(hl) if square else []) + ([] if t is None else list(_split2(t))))
               for hl, t in zip(p_hl, tinvs)]
        res = [_dot(a, w) for a, w in zip(lhs, weights)]
        new_ps, new_ts = [], []
        for p, t, a in zip(ps, tinvs, res):
            o = 0
            if square:
                new_ps.append(a[:r] + a[r:2 * r])
                o = 2 * r
            new_ts.append(eye + p if t is None else t + (a[o:o + r] + a[o + r:o + 2 * r]))
        ps, tinvs = new_ps if square else ps, new_ts
    rhs = [jnp.concatenate([kb * egc, v * beta], axis=1).astype(BF16)
           for kb, egc, v, beta in zip(kbs, egcs, vs, betas)]
    t_hl = [_split2(t) for t in tinvs]
    sols = []
    for ti, t in enumerate(tiles):
        hi, lo = (x[:, (ti % 2) * LANES:(ti % 2 + 1) * LANES] for x in t_hl[ti // 2])
        a = _dot(_rows([hi, lo]), _block_diag([rhs[i] for i in t]))
        sols.append(a[:r] + a[r:])
    out = []
    for ti, t in enumerate(tiles):
        per_inst = []
        for slot, i in enumerate(t):
            s_ = sols[ti][:, slot * (dk + dv):(slot + 1) * (dk + dv)]
            per_inst.append((s_[:, :dk], s_[:, dk:], qs[i] * egcs[i], ks[i] * jnp.exp(gls[i] - gcs[i]),
                             jnp.exp(gls[i])))
        out.append((qks[ti // 2][:, (ti % 2) * LANES:(ti % 2 + 1) * LANES], per_inst))
    return out


def _gated_out_norm(o, w, z):
    return (_rms(o, w) * _silu(z)).astype(BF16)


def _head_cols(xb, h, heads, dk, dv):
    q = xb[:, h * dk:(h + 1) * dk]
    k = xb[:, (heads + h) * dk:(heads + h + 1) * dk]
    v = xb[:, 2 * heads * dk + h * dv:2 * heads * dk + (h + 1) * dv]
    return q, k, v


def _delta_prompt_intra_kernel(x_ref, gb_ref, kq_ref, vu_ref, kd_ref, qk_ref, gt_ref, *,
                               heads, dk, dv, chunk):
    rows = x_ref.shape[0]
    xb = x_ref[...]
    gb = gb_ref[...]
    masks = _chunk_masks(chunk, chunk)
    insts = []
    for g in range(rows // chunk):
        rs = slice(g * chunk, (g + 1) * chunk)
        gates = _delta_gates(gb[rs], masks, heads, dk)
        insts.extend(_head_cols(xb[rs], h, heads, dk, dv) + gates[h] for h in range(heads))
    intra = _delta_intra(insts, masks, chunk)

    tiles_per_chunk = len(intra) // (rows // chunk)
    for ti, (qk, per_inst) in enumerate(intra):
        g, tc = divmod(ti, tiles_per_chunk)
        qk_ref[0, g, tc] = qk.astype(BF16)
        for x, (k_cum, v_u, q_dec, k_dec, g_tot) in enumerate(per_inst):
            h = tc * len(per_inst) + x
            kq_ref[0, g, h] = _rows([k_cum, q_dec]).astype(BF16)
            vu_ref[0, g, h] = v_u
            kd_ref[0, g, h] = k_dec.astype(BF16)
            gt_ref[0, g, h] = g_tot[0:SUBLANES, :]


def _delta_prompt_state_kernel(kq_ref, vu_ref, kd_ref, qk_ref, gt_ref, z_ref, nw_ref, o_ref, s_out_ref,
                               s_ref):
    n = pl.program_id(0)
    batch, chunks, heads, c, dv = vu_ref.shape
    tiles = qk_ref.shape[2]
    m = heads // tiles

    @pl.when(n == 0)
    def _():
        s_ref[...] = jnp.zeros(s_ref.shape, F32)

    inst = [(b, h) for b in range(batch) for h in range(heads)]
    state = [s_ref[b, h] for b, h in inst]
    for j in range(chunks):
        res = [_dot(kq_ref[b, j, h], s.astype(BF16)) for (b, h), s in zip(inst, state)]
        v_new = [vu_ref[b, j, h] - r_[:c] for (b, h), r_ in zip(inst, res)]
        vn_bf = [vn.astype(BF16) for vn in v_new]
        o_intra = [_dot(qk_ref[b, j, t], _block_diag(vn_bf[(b * tiles + t) * m:(b * tiles + t + 1) * m]))
                   for b in range(batch) for t in range(tiles)]
        state = [s * gt_ref[b, j, h][0:1, :] + _dot_tn(kd_ref[b, j, h], vn)
                 for (b, h), s, vn in zip(inst, state, vn_bf)]
        for i, (b, h) in enumerate(inst):
            o = res[i][c:] + o_intra[i // m][:, (i % m) * dv:(i % m + 1) * dv]
            cs = slice(h * dv, (h + 1) * dv)
            o_ref[b, j, :, cs] = _gated_out_norm(o, nw_ref[...], z_ref[b, j, :, cs])
    for (b, h), s in zip(inst, state):
        s_ref[b, h] = s

    @pl.when(n == pl.num_programs(0) - 1)
    def _():
        s_out_ref[...] = s_ref[...]


def _delta_prompt(x, z, gb, norm_w, *, batch, heads, dk, dv):
    t, cdim = x.shape
    seq = t // batch
    c = DN_CHUNK if seq % DN_CHUNK == 0 else seq
    assert c % SUBLANES == 0 and dk == dv == LANES
    nc = seq // c
    group = PROMPT_CHUNKS_PER_STEP if nc % PROMPT_CHUNKS_PER_STEP == 0 else 1
    rows = group * c
    steps = seq // rows
    tiles = heads * c // LANES
    blk = lambda n_: pl.BlockSpec((rows, n_), lambda b, n: (b * steps + n, 0))
    per_chunk = lambda per, r_, dt: (
        pl.BlockSpec((1, group, per, r_, LANES), lambda b, n: (b, n, 0, 0, 0)),
        jax.ShapeDtypeStruct((batch, nc, per, r_, LANES), dt))
    outs = [per_chunk(heads, 2 * c, BF16), per_chunk(heads, c, F32), per_chunk(heads, c, BF16),
            per_chunk(tiles, c, BF16), per_chunk(heads, SUBLANES, F32)]
    kq, vu, kd, qk, gt = pl.pallas_call(
        functools.partial(_delta_prompt_intra_kernel, heads=heads, dk=dk, dv=dv, chunk=c),
        grid=(batch, steps),
        in_specs=[blk(cdim), blk(gb.shape[1])],
        out_specs=[o[0] for o in outs],
        out_shape=[o[1] for o in outs],
        compiler_params=_params(("parallel", "parallel")),
        name="delta_prompt_intra",
    )(x, gb)
    per_step = STATE_CHUNKS_PER_STEP if nc % STATE_CHUNKS_PER_STEP == 0 else 1
    all_seq = lambda a: pl.BlockSpec((batch, per_step) + a.shape[2:], lambda n: (0, n) + (0,) * (a.ndim - 2))
    z4 = z.reshape(batch, nc, c, heads * dv)
    o, s_out = pl.pallas_call(
        _delta_prompt_state_kernel,
        grid=(nc // per_step,),
        in_specs=[all_seq(kq), all_seq(vu), all_seq(kd), all_seq(qk), all_seq(gt), all_seq(z4),
                  _const_spec(norm_w.shape)],
        out_specs=[all_seq(z4), _const_spec((batch, heads, dk, dv))],
        out_shape=[jax.ShapeDtypeStruct(z4.shape, BF16), jax.ShapeDtypeStruct((batch, heads, dk, dv), F32)],
        scratch_shapes=[pltpu.VMEM((batch, heads, dk, dv), F32)],
        compiler_params=_params(("arbitrary",)),
        name="delta_prompt_state",
    )(kq, vu, kd, qk, gt, z4, norm_w)
    return o.reshape(t, heads * dv), s_out


def _delta_sample_kernel(x_ref, cbuf_ref, z_ref, gb_ref, s_in_ref, cw_ref, nw_ref, o_ref, s_out_ref,
                         xc_ref, *, heads, dk, dv, conv_w):
    nb, hist, cdim = cbuf_ref.shape
    r = x_ref.shape[0]
    t = r // nb
    xc_ref[:, SUBLANES:SUBLANES + t, :] = x_ref[...].reshape(nb, t, cdim)
    xc_ref[:, SUBLANES - hist:SUBLANES, :] = cbuf_ref[...]
    acc = None
    for jj in range(conv_w):
        off = SUBLANES - (conv_w - 1) + jj
        term = xc_ref[:, off:off + t, :] * cw_ref[jj:jj + 1, :][None]
        acc = term if acc is None else acc + term
    xb = _silu(acc).reshape(r, cdim)

    masks = _chunk_masks(r, t)
    gates = _delta_gates(gb_ref[...], masks, heads, dk)
    insts = []
    for h in range(heads):
        q, k, v = _head_cols(xb, h, heads, dk, dv)
        insts.append((_unit(q) * (dk ** -0.5), _unit(k), v) + gates[h])
    intra = _delta_intra(insts, masks, t)
    rows = [slice(b * t, (b + 1) * t) for b in range(nb)]
    for h in range(heads):
        qk, ((k_cum, v_u, q_dec, k_dec, g_tot),) = intra[h]
        state = [s_in_ref[b, h] for b in range(nb)]
        res = [_dot(jnp.concatenate([k_cum[rs], q_dec[rs]], axis=0).astype(BF16), s.astype(BF16))
               for rs, s in zip(rows, state)]
        v_new = [v_u[rs] - r_[:t] for rs, r_ in zip(rows, res)]
        for b in range(nb):
            s_out_ref[b, h] = (state[b] * g_tot[b * t:b * t + 1, :]
                               + _dot_tn(k_dec[rows[b]].astype(BF16), v_new[b].astype(BF16)))
        o = (jnp.concatenate([r_[t:] for r_ in res], axis=0)
             + _dot(qk.astype(BF16), jnp.concatenate(v_new, axis=0).astype(BF16)))
        o_ref[:, h * dv:(h + 1) * dv] = _gated_out_norm(o, nw_ref[...], z_ref[:, h * dv:(h + 1) * dv])


def _delta_sample(x, cbuf, z, gb, s0, conv_w, norm_w, *, heads, dk, dv):
    rows, cdim = x.shape
    nbatch, hist, _ = cbuf.shape
    t = rows // nbatch
    assert t == SUBLANES and hist == conv_w.shape[0] - 1 and hist <= t and dk == dv
    nb = SAMPLE_GROUP if nbatch % SAMPLE_GROUP == 0 else 1
    r = nb * t
    blk = lambda n_: pl.BlockSpec((r, n_), lambda i: (i, 0))
    sblk = pl.BlockSpec((nb, heads, dk, dv), lambda i: (i, 0, 0, 0))
    return pl.pallas_call(
        functools.partial(_delta_sample_kernel, heads=heads, dk=dk, dv=dv, conv_w=conv_w.shape[0]),
        grid=(nbatch // nb,),
        in_specs=[blk(cdim), pl.BlockSpec((nb, hist, cdim), lambda i: (i, 0, 0)), blk(z.shape[1]),
                  blk(gb.shape[1]), sblk, _const_spec(conv_w.shape), _const_spec(norm_w.shape)],
        out_specs=[blk(heads * dv), sblk],
        out_shape=[jax.ShapeDtypeStruct((rows, heads * dv), BF16), jax.ShapeDtypeStruct(s0.shape, F32)],
        scratch_shapes=[pltpu.VMEM((nb, 2 * SUBLANES, cdim), F32)],
        compiler_params=_params(("parallel",)),
        name="delta_sample",
    )(x, cbuf, z, gb, s0, conv_w, norm_w)


def _layer(xp, xs, ck, cv, cconv, sdelta, lw, dims):
    (f1_pre, f1_post, f1_up, f1_down, mix_pre, mix_post, w_in, sinks, conv_w, a_log, dt_bias,
     out_norm, w_a, w_b, w_o, f2_pre, f2_post, f2_up, f2_down) = lw
    batch, seq, d = xp.shape
    dbatch, dseq, _ = xs.shape
    kvh, hd = dims["kv_heads"], dims["head_dim"]
    hb, dk, dv = dims["b_heads"], dims["dk"], dims["dv"]
    a_heads = w_a.shape[0] // hd
    cdim = conv_w.shape[1]
    d_ff = f1_down.shape[0]
    bf = lambda a: a.astype(BF16)
    row = lambda a: a.reshape(1, -1).astype(F32)

    sizes = (a_heads * hd, kvh * hd, kvh * hd, cdim, hb * dv, hb, hb, d, d)
    offs = [0]
    for s_ in sizes:
        offs.append(offs[-1] + s_)
    w_in_bf = bf(w_in)
    wga, wgb = w_in_bf[:, offs[7]:offs[8]], w_in_bf[:, offs[8]:offs[9]]
    proj_offs = (offs[0], offs[1], offs[3], offs[4], offs[5])
    alog = jnp.pad(row(a_log), ((0, 0), (0, LANES - hb)))
    dtb = jnp.pad(row(dt_bias), ((0, 0), (0, LANES - hb)))

    def ffn(xa, xb, pre, post, up, down):
        return _ffn(xa, xb, row(pre), row(post), bf(up), bf(down))

    def inproj(x, **conv):
        return _inproj(x, row(mix_pre), w_in_bf, alog, dtb, q_scale=hd ** -0.5, n_heads_b=hb,
                       offs=proj_offs, **conv)

    def merge(group_a, group_b):
        return _merge(group_a, group_b, row(mix_pre), row(mix_post), wga, wgb, bf(w_a), bf(w_b), bf(w_o))

    sinks = sinks.astype(F32)
    norm_w = row(out_norm)
    conv_w = conv_w.astype(F32)
    hist = conv_w.shape[0] - 1

    x, x_s = ffn(xp.reshape(batch * seq, d), xs.reshape(dbatch * dseq, d), f1_pre, f1_post, f1_up, f1_down)
    q, kv, xb, z, gb, tail = inproj(x, conv_w=conv_w, seq_rows=seq, dk=dk)
    oa, kv_last = _swa_prompt(sinks, q, kv, batch=batch, kv_heads=kvh, head_dim=hd)
    ob, p_state = _delta_prompt(xb, z, gb, norm_w, batch=batch, heads=hb, dk=dk, dv=dv)
    prompt_mix = (x, oa, ob)
    kv_last = kv_last.reshape(batch, WINDOW, 2, kvh, hd)
    p_k, p_v = kv_last[:, :, 0], kv_last[:, :, 1]
    p_conv = tail.reshape(batch, -1, SUBLANES, cdim)[:, -1, SUBLANES - hist:]

    x = x_s
    q, kv, xb, z, gb = inproj(x)
    wb = ck.shape[1]
    oa, s_k, s_v = _swa_sample(sinks, q.reshape(dbatch, dseq, -1), kv.reshape(dbatch, dseq, -1),
                               ck.reshape(dbatch, wb, kvh * hd), cv.reshape(dbatch, wb, kvh * hd),
                               kv_heads=kvh, head_dim=hd)
    ob, s_state = _delta_sample(xb, cconv, z, gb, sdelta, conv_w, norm_w, heads=hb, dk=dk, dv=dv)
    x_p, x_s = merge(prompt_mix, (x, oa.reshape(dbatch * dseq, -1), ob))
    yp, ys = ffn(x_p, x_s, f2_pre, f2_post, f2_up, f2_down)
    yp, ys = yp.reshape(batch, seq, d), ys.reshape(dbatch, dseq, d)
    s_k = s_k.reshape(dbatch, wb, kvh, hd)
    s_v = s_v.reshape(dbatch, wb, kvh, hd)
    s_conv = xb.reshape(dbatch, dseq, cdim)[:, dseq - hist:]
    return yp, ys, (p_k, p_v, p_conv, p_state), (s_k, s_v, s_conv, s_state)


def kernel(x_prompt, x_sample, cache_swa_k, cache_swa_v, state_conv, state_delta, ffn1_norm_pre, ffn1_norm_post, ffn1_w_up, ffn1_w_down, mix_norm_pre, mix_norm_post, w_in, attn_sinks, conv_w, dn_a_log, dn_dt_bias, dn_out_norm, w_branch_a, w_branch_b, w_out, ffn2_norm_pre, ffn2_norm_post, ffn2_w_up, ffn2_w_down):
    weights = (ffn1_norm_pre, ffn1_norm_post, ffn1_w_up, ffn1_w_down, mix_norm_pre, mix_norm_post,
               w_in, attn_sinks, conv_w, dn_a_log, dn_dt_bias, dn_out_norm, w_branch_a, w_branch_b,
               w_out, ffn2_norm_pre, ffn2_norm_post, ffn2_w_up, ffn2_w_down)
    depth = w_in.shape[0]
    dims = dict(kv_heads=cache_swa_k.shape[3], head_dim=cache_swa_k.shape[4],
                b_heads=state_delta.shape[2], dk=state_delta.shape[3], dv=state_delta.shape[4])
    yp, ys = x_prompt, x_sample
    p_out, s_out = [], []
    for l in range(depth):
        yp, ys, p_new, s_new = _layer(yp, ys, cache_swa_k[l], cache_swa_v[l], state_conv[l],
                                      state_delta[l], tuple(w[l] for w in weights), dims)
        p_out.append(p_new)
        s_out.append(s_new)
    stack = lambda outs, i: jnp.stack([o[i] for o in outs])
    return (yp, ys,
            stack(p_out, 0), stack(p_out, 1), stack(p_out, 2), stack(p_out, 3),
            stack(s_out, 0), stack(s_out, 1), stack(s_out, 2), stack(s_out, 3))
```

```python
import functools

import jax
import jax.numpy as jnp
from jax import lax
from jax.experimental import pallas as pl
from jax.experimental.pallas import tpu as pltpu

WINDOW = 128
DN_CHUNK = 64
EPS = 1e-6
F32 = jnp.float32
BF16 = jnp.bfloat16
NEG_BIG = -1e30
LANES = 128
SUBLANES = 8
VMEM_LIMIT_BYTES = 56 * 1024 * 1024
FF_CHUNK = 256
MXU_COLS = 256
CONV_ROW_BLOCK = 128
SAMPLE_GROUP = 16
SWA_SAMPLE_GROUP = 16
SWA_BLOCKS_PER_STEP = 4
STATE_CHUNKS_PER_STEP = 4
PROMPT_CHUNKS_PER_STEP = 8


def _rms(x, w):
    return x * lax.rsqrt(jnp.mean(x * x, axis=-1, keepdims=True) + EPS) * w


def _silu(x):
    return x * jax.nn.sigmoid(x)


def _unit(x):
    return x * lax.rsqrt(jnp.sum(x * x, axis=-1, keepdims=True) + EPS)


def _softplus(x):
    return jnp.maximum(x, 0.0) + jnp.log1p(jnp.exp(-jnp.abs(x)))


def _dot(a, b):
    return jnp.dot(a, b, preferred_element_type=F32)


def _dot_nt(a, b):
    return lax.dot_general(a, b, (((1,), (1,)), ((), ())), preferred_element_type=F32)


def _dot_tn(a, b):
    return lax.dot_general(a, b, (((0,), (0,)), ((), ())), preferred_element_type=F32)


def _split3(b):
    hi = b.astype(BF16)
    r = b - hi.astype(F32)
    mid = r.astype(BF16)
    lo = (r - mid.astype(F32)).astype(BF16)
    return hi, mid, lo


def _dot_sel(sel_bf16, b):
    hi, mid, lo = _split3(b)
    return _dot(sel_bf16, hi) + _dot(sel_bf16, mid) + _dot(sel_bf16, lo)


def _ffn_tile(x_ref, pre_ref, post_ref, wup_ref, wd_ref, o_ref):
    x = x_ref[...]
    xn = _rms(x, pre_ref[...]).astype(BF16)
    d_ff = wd_ref.shape[0]
    acc = jnp.zeros(x.shape, F32)
    for c in range(d_ff // FF_CHUNK):
        sl = slice(c * FF_CHUNK, (c + 1) * FF_CHUNK)
        g = _dot(xn, wup_ref[:, sl])
        u = _dot(xn, wup_ref[:, d_ff + c * FF_CHUNK:d_ff + (c + 1) * FF_CHUNK])
        acc = acc + _dot((_silu(g) * u).astype(BF16), wd_ref[sl, :])
    o_ref[...] = x + 0.5 * _rms(acc, post_ref[...])


def _ffn_kernel(xa_ref, xb_ref, pre_ref, post_ref, wup_ref, wd_ref, oa_ref, ob_ref, *, tiles_a):
    i = pl.program_id(0)

    @pl.when(i < tiles_a)
    def _():
        _ffn_tile(xa_ref, pre_ref, post_ref, wup_ref, wd_ref, oa_ref)

    @pl.when(i >= tiles_a)
    def _():
        _ffn_tile(xb_ref, pre_ref, post_ref, wup_ref, wd_ref, ob_ref)


def _const_spec(shape):
    nd = len(shape)
    return pl.BlockSpec(shape, lambda *_: (0,) * nd)


def _weight_spec(shape):
    nd = len(shape)
    return pl.BlockSpec(shape, lambda *_: (0,) * nd, pipeline_mode=pl.Buffered(1))


def _row_tile(t, largest=512):
    for tm in (1024, 512, 256, 128, 64, 32, 16, 8):
        if tm <= largest and t % tm == 0:
            return tm
    raise ValueError(f"token count {t} is not a multiple of 8")


def _params(sem):
    return pltpu.CompilerParams(dimension_semantics=sem, vmem_limit_bytes=VMEM_LIMIT_BYTES)


def _ffn(xa, xb, pre, post, wup, wd):
    (ta, d), tb = xa.shape, xb.shape[0]
    f = wd.shape[0]
    assert f % FF_CHUNK == 0 and wup.shape == (d, 2 * f)
    tm = min(_row_tile(ta), _row_tile(tb))
    na, nb = ta // tm, tb // tm
    row_a = pl.BlockSpec((tm, d), lambda i: (jnp.minimum(i, na - 1), 0))
    row_b = pl.BlockSpec((tm, d), lambda i: (jnp.maximum(i - na, 0), 0))
    return pl.pallas_call(
        functools.partial(_ffn_kernel, tiles_a=na),
        grid=(na + nb,),
        in_specs=[row_a, row_b, _const_spec((1, d)), _const_spec((1, d)),
                  _weight_spec((d, 2 * f)), _weight_spec((f, d))],
        out_specs=[row_a, row_b],
        out_shape=[jax.ShapeDtypeStruct((ta, d), F32), jax.ShapeDtypeStruct((tb, d), F32)],
        compiler_params=_params(("arbitrary",)),
        name="ffn",
    )(xa, xb, pre, post, wup, wd)


def _conv_silu_unit(xc_ref, cv_ref, cw_ref, r0, rows, cb, heads, dk):
    taps = cw_ref.shape[0]
    rs = slice(r0, r0 + rows)
    cs = slice(cb * dk, (cb + 1) * dk)
    acc = None
    for jj in range(taps):
        off = r0 + SUBLANES - (taps - 1) + jj
        term = xc_ref[off:off + rows, cs] * cw_ref[jj:jj + 1, cs]
        acc = term if acc is None else acc + term
    y = _silu(acc)
    if cb < 2 * heads:
        y = _unit(y) * (dk ** -0.5) if cb < heads else _unit(y)
    cv_ref[rs, cs] = y


def _inproj_kernel(x_ref, pre_ref, w_ref, alog_ref, dtb_ref, *rest, q_scale, n_heads_b, offs, conv):
    tm = x_ref.shape[0]
    if conv is None:
        q_ref, kv_ref, b_ref, z_ref, gb_ref, h_ref = rest
        sub = tm
    else:
        cw_ref, q_ref, kv_ref, b_ref, z_ref, gb_ref, tail_ref, h_ref, xc_ref = rest
        seq_rows, dk = conv
        sub = min(tm, CONV_ROW_BLOCK)

        @pl.when((pl.program_id(0) * tm) % seq_rows == 0)
        def _():
            xc_ref[0:SUBLANES, :] = jnp.zeros((SUBLANES, xc_ref.shape[1]), F32)

    h_ref[...] = _rms(x_ref[...], pre_ref[...]).astype(BF16)

    def project(r0, i, c0, width):
        rs = slice(r0, r0 + sub)
        cs = slice(c0, c0 + width)
        y = _dot(h_ref[rs, :], w_ref[:, offs[i] + c0:offs[i] + c0 + width])
        if i == 0:
            q_ref[rs, cs] = (y * q_scale).astype(BF16)
        elif i == 1:
            kv_ref[rs, cs] = y
        elif i == 2 and conv is None:
            b_ref[rs, cs] = y
        elif i == 2:
            xc_ref[SUBLANES + r0:SUBLANES + r0 + sub, cs] = y
        elif i == 3:
            z_ref[rs, cs] = y
        else:
            g = -jnp.exp(alog_ref[...]) * _softplus(y + dtb_ref[...])
            lane = lax.broadcasted_iota(jnp.int32, y.shape, 1)
            gb_ref[rs, :] = jnp.where(lane < n_heads_b, g, jax.nn.sigmoid(y))

    widths = [offs[i + 1] - offs[i] for i in range(4)] + [LANES]
    pieces = [(i, c0, min(MXU_COLS, widths[i] - c0))
              for i in (2, 0, 1, 3, 4) for c0 in range(0, widths[i], MXU_COLS)]
    for r0 in range(0, tm + (sub if conv is not None else 0), sub):
        mm = [functools.partial(project, r0, *p) for p in pieces] if r0 < tm else []
        cv = []
        if conv is not None and r0 > 0:
            cv = [functools.partial(_conv_silu_unit, xc_ref, b_ref, cw_ref, r0 - sub, sub, cb, n_heads_b, dk)
                  for cb in range(xc_ref.shape[1] // dk)]
        for k in range(max(len(mm), len(cv))):
            if k < len(mm):
                mm[k]()
            if k < len(cv):
                cv[k]()
    if conv is not None:
        tail_ref[0] = xc_ref[tm:tm + SUBLANES, :]
        xc_ref[0:SUBLANES, :] = xc_ref[tm:tm + SUBLANES, :]


def _inproj(x, pre, w, alog, dtb, *, q_scale, n_heads_b, offs, conv_w=None, seq_rows=None, dk=None):
    t, d = x.shape
    tm = _row_tile(t)
    assert all(o % LANES == 0 for o in offs[:5]) and offs[4] + LANES <= w.shape[1]
    row = lambda n: pl.BlockSpec((tm, n), lambda i: (i, 0))
    cdim = offs[3] - offs[2]
    outs = [(offs[1] - offs[0], BF16), (offs[2] - offs[1], F32), (cdim, F32), (offs[4] - offs[3], F32),
            (LANES, F32)]
    in_specs = [row(d), _const_spec((1, d)), _weight_spec(w.shape),
                _const_spec(alog.shape), _const_spec(dtb.shape)]
    out_specs = [row(n) for n, _ in outs]
    out_shape = [jax.ShapeDtypeStruct((t, n), dt) for n, dt in outs]
    args, scratch, conv = [x, pre, w, alog, dtb], [pltpu.VMEM((tm, d), BF16)], None
    if conv_w is not None:
        assert seq_rows % tm == 0 and conv_w.shape[0] - 1 <= SUBLANES
        in_specs.append(_const_spec(conv_w.shape))
        args.append(conv_w)
        out_specs.append(pl.BlockSpec((1, SUBLANES, cdim), lambda i: (i, 0, 0)))
        out_shape.append(jax.ShapeDtypeStruct((t // tm, SUBLANES, cdim), F32))
        scratch.append(pltpu.VMEM((tm + SUBLANES, cdim), F32))
        conv = (seq_rows, dk)
    return pl.pallas_call(
        functools.partial(_inproj_kernel, q_scale=q_scale, n_heads_b=n_heads_b, offs=offs, conv=conv),
        grid=(t // tm,),
        in_specs=in_specs,
        out_specs=out_specs,
        out_shape=out_shape,
        scratch_shapes=scratch,
        compiler_params=_params(("arbitrary",)),
        name="inproj",
    )(*args)


def _merge_tile(x_ref, oa_ref, ob_ref, pre_ref, post_ref, wga_ref, wgb_ref, wa_ref, wb_ref, wo_ref,
                o_ref):
    x = x_ref[...]
    h = _rms(x, pre_ref[...]).astype(BF16)
    oa, ob = oa_ref[...], ob_ref[...]
    m = jnp.zeros(x.shape, F32)
    for c0 in range(0, wo_ref.shape[0], MXU_COLS):
        cs = slice(c0, c0 + MXU_COLS)
        merged = (jax.nn.sigmoid(_dot(h, wga_ref[:, cs])) * _dot(oa, wa_ref[:, cs])
                  + jax.nn.sigmoid(_dot(h, wgb_ref[:, cs])) * _dot(ob, wb_ref[:, cs]))
        m = m + _dot(merged.astype(BF16), wo_ref[cs, :])
    o_ref[...] = x + _rms(m, post_ref[...])


def _merge_kernel(xa_ref, oaa_ref, oba_ref, xb_ref, oab_ref, obb_ref, *rest, tiles_a):
    *shared, oa_ref, ob_ref = rest
    i = pl.program_id(0)

    @pl.when(i < tiles_a)
    def _():
        _merge_tile(xa_ref, oaa_ref, oba_ref, *shared, oa_ref)

    @pl.when(i >= tiles_a)
    def _():
        _merge_tile(xb_ref, oab_ref, obb_ref, *shared, ob_ref)


def _merge(group_a, group_b, pre, post, wga, wgb, wa, wb, wo):
    (ta, d), tb = group_a[0].shape, group_b[0].shape[0]
    tm = min(_row_tile(ta, largest=1024), _row_tile(tb, largest=1024))
    na, nb = ta // tm, tb // tm
    rows_a = [pl.BlockSpec((tm, a.shape[1]), lambda i: (jnp.minimum(i, na - 1), 0)) for a in group_a]
    rows_b = [pl.BlockSpec((tm, a.shape[1]), lambda i: (jnp.maximum(i - na, 0), 0)) for a in group_b]
    weights = (wga, wgb, wa, wb, wo)
    return pl.pallas_call(
        functools.partial(_merge_kernel, tiles_a=na),
        grid=(na + nb,),
        in_specs=rows_a + rows_b + [_const_spec((1, d)), _const_spec((1, d))]
        + [_weight_spec(w.shape) for w in weights],
        out_specs=[rows_a[0], rows_b[0]],
        out_shape=[jax.ShapeDtypeStruct((ta, d), F32), jax.ShapeDtypeStruct((tb, d), F32)],
        compiler_params=_params(("arbitrary",)),
        name="merge",
    )(*group_a, *group_b, pre, post, *weights)


def _sink_softmax(s, mask, sink):
    s = jnp.where(mask, s, NEG_BIG)
    m = jnp.maximum(jnp.max(s, axis=-1, keepdims=True), sink)
    e = jnp.exp(s - m)
    return e / (jnp.sum(e, axis=-1, keepdims=True) + jnp.exp(sink - m))


def _swa_prompt_kernel(sink_ref, q_ref, kvp_ref, kvc_ref, o_ref, kv_last_ref, *, kv_heads, group,
                       head_dim):
    n = pl.program_id(1)
    w = WINDOW
    blocks = q_ref.shape[0] // w
    dkv = kv_heads * head_dim
    pairs = group // 2
    kv_all = jnp.concatenate([kvp_ref[...], kvc_ref[...]], axis=0)

    @pl.when(n == pl.num_programs(1) - 1)
    def _():
        kv_last_ref[0] = kvc_ref[(blocks - 1) * w:blocks * w, :]
    lo = lax.broadcasted_iota(jnp.int32, (kv_all.shape[0], dkv), 1) < head_dim

    def lane_halves(x):
        x_rot = pltpu.roll(x, head_dim, axis=1)
        pick = lambda kv, half: jnp.where(lo if half == 0 else ~lo, x if kv == half else x_rot, 0.0)
        return [[pick(kv, half).astype(BF16) for half in range(2)] for kv in range(kv_heads)]

    k_ext = lane_halves(kv_all[:, :dkv])
    v_ext = lane_halves(kv_all[:, dkv:])

    i = lax.broadcasted_iota(jnp.int32, (w, 2 * w), 0)
    j = lax.broadcasted_iota(jnp.int32, (w, 2 * w), 1)
    bias = jnp.where(j > i, jnp.where(j <= i + w, 0.0, NEG_BIG), NEG_BIG)
    bias_seq_start = jnp.where(j >= w, bias, NEG_BIG)
    lo_out = lax.broadcasted_iota(jnp.int32, (w, 2 * head_dim), 1) < head_dim

    ones_half = [jnp.where(lo if half == 0 else ~lo, 1.0, 0.0).astype(BF16) for half in range(2)]
    v_ext = [[jnp.concatenate([v_ext[kv][half], ones_half[half]], axis=1) for half in range(2)]
             for kv in range(kv_heads)]

    inst = [(t, kv) for t in range(blocks) for kv in range(kv_heads)]
    band = lambda x, t: x[t * w:(t + 2) * w]
    q2 = [jnp.concatenate([q_ref[t * w:(t + 1) * w, (kv * pairs + p) * LANES:(kv * pairs + p + 1) * LANES]
                           for p in range(pairs)], axis=0) for t, kv in inst]
    scores = [[_dot_nt(q_, band(k_ext[kv][half], t)) for half in range(2)]
              for q_, (t, kv) in zip(q2, inst)]
    bias_first = jnp.where(n > 0, bias, bias_seq_start)
    for (t, kv), sc in zip(inst, scores):
        vband = jnp.concatenate([band(v_ext[kv][0], t), band(v_ext[kv][1], t)], axis=0)
        for p in range(pairs):
            exps, sink_terms = [], []
            for half in range(2):
                sink = sink_ref[kv * group + 2 * p + half]
                s = sc[half][p * w:(p + 1) * w] + (bias_first if t == 0 else bias)
                m = jnp.maximum(jnp.max(s, axis=-1, keepdims=True), sink)
                exps.append(jnp.exp(s - m).astype(BF16))
                sink_terms.append(jnp.exp(sink - m))
            o = _dot(jnp.concatenate(exps, axis=1), vband)
            den = o[:, LANES:] + jnp.where(lo_out, sink_terms[0], sink_terms[1])
            cs = slice((kv * pairs + p) * LANES, (kv * pairs + p + 1) * LANES)
            o_ref[t * w:(t + 1) * w, cs] = (o[:, :LANES] / den).astype(o_ref.dtype)


def _swa_prompt(sinks, q, kv, *, batch, kv_heads, head_dim):
    t, dq = q.shape
    seq = t // batch
    assert seq % WINDOW == 0
    group = dq // (kv_heads * head_dim)
    assert 2 * head_dim == LANES and kv_heads == 2 and group % 2 == 0
    blocks = SWA_BLOCKS_PER_STEP if (seq // WINDOW) % SWA_BLOCKS_PER_STEP == 0 else 1
    nb = seq // (blocks * WINDOW)
    rows = blocks * WINDOW
    return pl.pallas_call(
        functools.partial(_swa_prompt_kernel, kv_heads=kv_heads, group=group, head_dim=head_dim),
        grid=(batch, nb),
        in_specs=[pl.BlockSpec(memory_space=pltpu.SMEM),
                  pl.BlockSpec((rows, dq), lambda b, n: (b * nb + n, 0)),
                  pl.BlockSpec((WINDOW, kv.shape[1]),
                               lambda b, n: ((b * nb + n) * blocks - jnp.minimum(n, 1), 0)),
                  pl.BlockSpec((rows, kv.shape[1]), lambda b, n: (b * nb + n, 0))],
        out_specs=[pl.BlockSpec((rows, dq), lambda b, n: (b * nb + n, 0)),
                   pl.BlockSpec((1, WINDOW, kv.shape[1]), lambda b, n: (b, 0, 0))],
        out_shape=[jax.ShapeDtypeStruct((t, dq), BF16),
                   jax.ShapeDtypeStruct((batch, WINDOW, kv.shape[1]), F32)],
        compiler_params=_params(("parallel", "arbitrary")),
        name="swa_prompt",
    )(sinks, q, kv, kv)


def _swa_sample_kernel(sink_ref, q_ref, kvn_ref, ck_ref, cv_ref, o_ref, nk_ref, nv_ref, *,
                       kv_heads, group, head_dim):
    bb, t, _ = q_ref.shape
    wb = ck_ref.shape[1]
    dkv = kv_heads * head_dim
    q = q_ref[...]
    kc = jnp.concatenate([ck_ref[...], kvn_ref[:, :, :dkv]], axis=1)
    vc = jnp.concatenate([cv_ref[...], kvn_ref[:, :, dkv:]], axis=1)
    nk_ref[...] = kc[:, t:, :]
    nv_ref[...] = vc[:, t:, :]
    kcb = kc.astype(BF16)
    vcb = vc.astype(BF16)
    assert t & (t - 1) == 0
    i = lax.broadcasted_iota(jnp.int32, (group * t, wb + t), 0) & (t - 1)
    j = lax.broadcasted_iota(jnp.int32, (group * t, wb + t), 1)
    diff = wb + i - j
    mask = ((diff >= 0) & (diff < WINDOW))[None]
    hs = [slice(kv * head_dim, (kv + 1) * head_dim) for kv in range(kv_heads)]
    heads = [[kv * group + g for g in range(group)] for kv in range(kv_heads)]
    qs = [jnp.concatenate([q[:, :, h * head_dim:(h + 1) * head_dim] for h in hd_], axis=1) for hd_ in heads]
    sinks = [jnp.concatenate([jnp.full((1, t, 1), sink_ref[h], F32) for h in hd_], axis=1) for hd_ in heads]
    scores = [jnp.einsum('bqd,bkd->bqk', q_, kcb[:, :, h_], preferred_element_type=F32)
              for q_, h_ in zip(qs, hs)]
    probs = [_sink_softmax(s, mask, sink).astype(BF16) for s, sink in zip(scores, sinks)]
    outs = [jnp.einsum('bqk,bkd->bqd', p, vcb[:, :, h_], preferred_element_type=F32)
            for p, h_ in zip(probs, hs)]
    o_ref[...] = jnp.concatenate([o[:, g * t:(g + 1) * t, :] for o in outs for g in range(group)],
                                 axis=2).astype(o_ref.dtype)


def _swa_sample(sinks, q, kvn, ck, cv, *, kv_heads, head_dim):
    nbatch, t, dq = q.shape
    wb = ck.shape[1]
    group = dq // (kv_heads * head_dim)
    bb = SWA_SAMPLE_GROUP if nbatch % SWA_SAMPLE_GROUP == 0 else 1
    blk = lambda a: pl.BlockSpec((bb,) + a.shape[1:], lambda i: (i, 0, 0))
    return pl.pallas_call(
        functools.partial(_swa_sample_kernel, kv_heads=kv_heads, group=group, head_dim=head_dim),
        grid=(nbatch // bb,),
        in_specs=[pl.BlockSpec(memory_space=pltpu.SMEM), blk(q), blk(kvn), blk(ck), blk(cv)],
        out_specs=[blk(q), blk(ck), blk(cv)],
        out_shape=[jax.ShapeDtypeStruct(q.shape, BF16), jax.ShapeDtypeStruct(ck.shape, F32),
                   jax.ShapeDtypeStruct(cv.shape, F32)],
        compiler_params=_params(("parallel",)),
        name="swa_sample",
    )(sinks, q, kvn, ck, cv)


def _chunk_masks(r, c):
    assert r & (r - 1) == 0 and c & (c - 1) == 0 and LANES % r == 0
    width = 2 * LANES
    ri = lax.broadcasted_iota(jnp.int32, (r, width), 0)
    cj = lax.broadcasted_iota(jnp.int32, (r, width), 1)
    ci = cj & (r - 1)
    incl, strict = ci <= ri, ci < ri
    if r != c:
        shift = c.bit_length() - 1
        same = (ri >> shift) == (ci >> shift)
        incl, strict = same & incl, same & strict
    f01 = lambda mask: jnp.where(mask, 1.0, 0.0)
    block = cj >> (r.bit_length() - 1)
    first_in_tile = lax.broadcasted_iota(jnp.int32, (r, LANES), 1) < r
    return dict(incl=incl, strict=strict, eye=f01(ri == ci), first_in_tile=first_in_tile,
                keep=[f01(block == x).astype(BF16) for x in range(width // r)],
                incl_sel=f01(incl)[:, :r].astype(BF16),
                same_sel=None if r == c else f01(same)[:, :r].astype(BF16))


def _delta_gates(gb, masks, heads, dk):
    r = gb.shape[0]
    gc_all = _dot_sel(masks["incl_sel"], gb)
    if masks["same_sel"] is None:
        gl_all = jnp.broadcast_to(gc_all[r - 1:r, :], gc_all.shape)
    else:
        gl_all = _dot_sel(masks["same_sel"], gb)
    bc = lambda a, lane: jnp.broadcast_to(a[:, lane:lane + 1], (r, dk))
    return [(bc(gc_all, h), bc(gl_all, h), bc(gb, heads + h)) for h in range(heads)]


def _split2(x):
    hi = x.astype(BF16)
    return hi, (x - hi.astype(F32)).astype(BF16)


def _rows(xs):
    return jnp.concatenate(xs, axis=0)


def _lanes(xs):
    return xs[0] if len(xs) == 1 else jnp.concatenate(xs, axis=1)


def _block_diag(blocks):
    n = len(blocks)
    z = jnp.zeros(blocks[0].shape, BF16)
    return _rows([_lanes([blocks[i] if j == i else z for j in range(n)]) for i in range(n)])


def _delta_intra(insts, masks, c):
    r, dk = insts[0][0].shape
    dv = insts[0][2].shape[1]
    m = LANES // r
    n = 2 * m
    assert dk == LANES and len(insts) % n == 0
    incl, strict, eye, keep = masks["incl"], masks["strict"], masks["eye"], masks["keep"]
    tiles = [list(range(i, i + m)) for i in range(0, len(insts), m)]
    groups = [(2 * i, 2 * i + 1) for i in range(len(tiles) // 2)]
    qs, ks, vs, gcs, gls, betas = ([i[j] for i in insts] for j in range(6))
    bf = lambda x: x.astype(BF16)

    def lane_tile(xs):
        return xs[0] if m == 1 else jnp.where(masks["first_in_tile"], xs[0], xs[1])

    def block_diag_x(x):
        return _rows([x * keep[i] for i in range(n)])

    g_cols = [_lanes([lane_tile([gcs[i] for i in tiles[t]]) for t in g]) for g in groups]
    g_rows = [jnp.transpose(_rows([gcs[i] for t in g for i in tiles[t]]))[:r, :] for g in groups]
    decays = [jnp.where(incl, jnp.exp(gc - gr), 0.0) for gc, gr in zip(g_cols, g_rows)]
    egcs = [jnp.exp(gc) for gc in gcs]
    kbs = [k * beta for k, beta in zip(ks, betas)]
    kq = [_dot_nt(_rows([_lanes([bf(kbs[i]) for i in t]), _lanes([bf(qs[i]) for i in t])]),
                  _block_diag([bf(ks[i]) for i in t])) for t in tiles]
    ms = [jnp.where(strict, _lanes([kq[t][:r] for t in g]) * decay, 0.0) for g, decay in zip(groups, decays)]
    qks = [_lanes([kq[t][r:] for t in g]) * decay for g, decay in zip(groups, decays)]
    rounds = max((c - 1).bit_length(), 1)
    ps = [-m_ for m_ in ms]
    tinvs = [None] * len(groups)
    for j in range(rounds):
        square = j + 1 < rounds
        if not square and tinvs[0] is None:
            tinvs = [eye + p for p in ps]
            break
        p_hl = [_split2(p) for p in ps]
        weights = [block_diag_x(hi) for hi, _ in p_hl]
        lhs = [_rows((list(hl) if square else []) + ([] if t is None else list(_split2(t))))
               for hl, t in zip(p_hl, tinvs)]
        res = [_dot(a, w) for a, w in zip(lhs, weights)]
        new_ps, new_ts = [], []
        for p, t, a in zip(ps, tinvs, res):
            o = 0
            if square:
                new_ps.append(a[:r] + a[r:2 * r])
                o = 2 * r
            new_ts.append(eye + p if t is None else t + (a[o:o + r] + a[o + r:o + 2 * r]))
        ps, tinvs = new_ps if square else ps, new_ts
    rhs = [jnp.concatenate([kb * egc, v * beta], axis=1).astype(BF16)
           for kb, egc, v, beta in zip(kbs, egcs, vs, betas)]
    t_hl = [_split2(t) for t in tinvs]
    sols = []
    for ti, t in enumerate(tiles):
        hi, lo = (x[:, (ti % 2) * LANES:(ti % 2 + 1) * LANES] for x in t_hl[ti // 2])
        a = _dot(_rows([hi, lo]), _block_diag([rhs[i] for i in t]))
        sols.append(a[:r] + a[r:])
    out = []
    for ti, t in enumerate(tiles):
        per_inst = []
        for slot, i in enumerate(t):
            s_ = sols[ti][:, slot * (dk + dv):(slot + 1) * (dk + dv)]
            per_inst.append((s_[:, :dk], s_[:, dk:], qs[i] * egcs[i], ks[i] * jnp.exp(gls[i] - gcs[i]),
                             jnp.exp(gls[i])))
        out.append((qks[ti // 2][:, (ti % 2) * LANES:(ti % 2 + 1) * LANES], per_inst))
    return out


def _gated_out_norm(o, w, z):
    return (_rms(o, w) * _silu(z)).astype(BF16)


def _head_cols(xb, h, heads, dk, dv):
    q = xb[:, h * dk:(h + 1) * dk]
    k = xb[:, (heads + h) * dk:(heads + h + 1) * dk]
    v = xb[:, 2 * heads * dk + h * dv:2 * heads * dk + (h + 1) * dv]
    return q, k, v


def _delta_prompt_intra_kernel(x_ref, gb_ref, kq_ref, vu_ref, kd_ref, qk_ref, gt_ref, *,
                               heads, dk, dv, chunk):
    rows = x_ref.shape[0]
    xb = x_ref[...]
    gb = gb_ref[...]
    masks = _chunk_masks(chunk, chunk)
    insts = []
    for g in range(rows // chunk):
        rs = slice(g * chunk, (g + 1) * chunk)
        gates = _delta_gates(gb[rs], masks, heads, dk)
        insts.extend(_head_cols(xb[rs], h, heads, dk, dv) + gates[h] for h in range(heads))
    intra = _delta_intra(insts, masks, chunk)

    tiles_per_chunk = len(intra) // (rows // chunk)
    for ti, (qk, per_inst) in enumerate(intra):
        g, tc = divmod(ti, tiles_per_chunk)
        qk_ref[0, g, tc] = qk.astype(BF16)
        for x, (k_cum, v_u, q_dec, k_dec, g_tot) in enumerate(per_inst):
            h = tc * len(per_inst) + x
            kq_ref[0, g, h] = _rows([k_cum, q_dec]).astype(BF16)
            vu_ref[0, g, h] = v_u
            kd_ref[0, g, h] = k_dec.astype(BF16)
            gt_ref[0, g, h] = g_tot[0:SUBLANES, :]


def _delta_prompt_state_kernel(kq_ref, vu_ref, kd_ref, qk_ref, gt_ref, z_ref, nw_ref, o_ref, s_out_ref,
                               s_ref):
    n = pl.program_id(0)
    batch, chunks, heads, c, dv = vu_ref.shape
    tiles = qk_ref.shape[2]
    m = heads // tiles

    @pl.when(n == 0)
    def _():
        s_ref[...] = jnp.zeros(s_ref.shape, F32)

    inst = [(b, h) for b in range(batch) for h in range(heads)]
    state = [s_ref[b, h] for b, h in inst]
    for j in range(chunks):
        res = [_dot(kq_ref[b, j, h], s.astype(BF16)) for (b, h), s in zip(inst, state)]
        v_new = [vu_ref[b, j, h] - r_[:c] for (b, h), r_ in zip(inst, res)]
        vn_bf = [vn.astype(BF16) for vn in v_new]
        o_intra = [_dot(qk_ref[b, j, t], _block_diag(vn_bf[(b * tiles + t) * m:(b * tiles + t + 1) * m]))
                   for b in range(batch) for t in range(tiles)]
        state = [s * gt_ref[b, j, h][0:1, :] + _dot_tn(kd_ref[b, j, h], vn)
                 for (b, h), s, vn in zip(inst, state, vn_bf)]
        for i, (b, h) in enumerate(inst):
            o = res[i][c:] + o_intra[i // m][:, (i % m) * dv:(i % m + 1) * dv]
            cs = slice(h * dv, (h + 1) * dv)
            o_ref[b, j, :, cs] = _gated_out_norm(o, nw_ref[...], z_ref[b, j, :, cs])
    for (b, h), s in zip(inst, state):
        s_ref[b, h] = s

    @pl.when(n == pl.num_programs(0) - 1)
    def _():
        s_out_ref[...] = s_ref[...]


def _delta_prompt(x, z, gb, norm_w, *, batch, heads, dk, dv):
    t, cdim = x.shape
    seq = t // batch
    c = DN_CHUNK if seq % DN_CHUNK == 0 else seq
    assert c % SUBLANES == 0 and dk == dv == LANES
    nc = seq // c
    group = PROMPT_CHUNKS_PER_STEP if nc % PROMPT_CHUNKS_PER_STEP == 0 else 1
    rows = group * c
    steps = seq // rows
    tiles = heads * c // LANES
    blk = lambda n_: pl.BlockSpec((rows, n_), lambda b, n: (b * steps + n, 0))
    per_chunk = lambda per, r_, dt: (
        pl.BlockSpec((1, group, per, r_, LANES), lambda b, n: (b, n, 0, 0, 0)),
        jax.ShapeDtypeStruct((batch, nc, per, r_, LANES), dt))
    outs = [per_chunk(heads, 2 * c, BF16), per_chunk(heads, c, F32), per_chunk(heads, c, BF16),
            per_chunk(tiles, c, BF16), per_chunk(heads, SUBLANES, F32)]
    kq, vu, kd, qk, gt = pl.pallas_call(
        functools.partial(_delta_prompt_intra_kernel, heads=heads, dk=dk, dv=dv, chunk=c),
        grid=(batch, steps),
        in_specs=[blk(cdim), blk(gb.shape[1])],
        out_specs=[o[0] for o in outs],
        out_shape=[o[1] for o in outs],
        compiler_params=_params(("parallel", "parallel")),
        name="delta_prompt_intra",
    )(x, gb)
    per_step = STATE_CHUNKS_PER_STEP if nc % STATE_CHUNKS_PER_STEP == 0 else 1
    all_seq = lambda a: pl.BlockSpec((batch, per_step) + a.shape[2:], lambda n: (0, n) + (0,) * (a.ndim - 2))
    z4 = z.reshape(batch, nc, c, heads * dv)
    o, s_out = pl.pallas_call(
        _delta_prompt_state_kernel,
        grid=(nc // per_step,),
        in_specs=[all_seq(kq), all_seq(vu), all_seq(kd), all_seq(qk), all_seq(gt), all_seq(z4),
                  _const_spec(norm_w.shape)],
        out_specs=[all_seq(z4), _const_spec((batch, heads, dk, dv))],
        out_shape=[jax.ShapeDtypeStruct(z4.shape, BF16), jax.ShapeDtypeStruct((batch, heads, dk, dv), F32)],
        scratch_shapes=[pltpu.VMEM((batch, heads, dk, dv), F32)],
        compiler_params=_params(("arbitrary",)),
        name="delta_prompt_state",
    )(kq, vu, kd, qk, gt, z4, norm_w)
    return o.reshape(t, heads * dv), s_out


def _delta_sample_kernel(x_ref, cbuf_ref, z_ref, gb_ref, s_in_ref, cw_ref, nw_ref, o_ref, s_out_ref,
                         xc_ref, *, heads, dk, dv, conv_w):
    nb, hist, cdim = cbuf_ref.shape
    r = x_ref.shape[0]
    t = r // nb
    xc_ref[:, SUBLANES:SUBLANES + t, :] = x_ref[...].reshape(nb, t, cdim)
    xc_ref[:, SUBLANES - hist:SUBLANES, :] = cbuf_ref[...]
    acc = None
    for jj in range(conv_w):
        off = SUBLANES - (conv_w - 1) + jj
        term = xc_ref[:, off:off + t, :] * cw_ref[jj:jj + 1, :][None]
        acc = term if acc is None else acc + term
    xb = _silu(acc).reshape(r, cdim)

    masks = _chunk_masks(r, t)
    gates = _delta_gates(gb_ref[...], masks, heads, dk)
    insts = []
    for h in range(heads):
        q, k, v = _head_cols(xb, h, heads, dk, dv)
        insts.append((_unit(q) * (dk ** -0.5), _unit(k), v) + gates[h])
    intra = _delta_intra(insts, masks, t)
    rows = [slice(b * t, (b + 1) * t) for b in range(nb)]
    for h in range(heads):
        qk, ((k_cum, v_u, q_dec, k_dec, g_tot),) = intra[h]
        state = [s_in_ref[b, h] for b in range(nb)]
        res = [_dot(jnp.concatenate([k_cum[rs], q_dec[rs]], axis=0).astype(BF16), s.astype(BF16))
               for rs, s in zip(rows, state)]
        v_new = [v_u[rs] - r_[:t] for rs, r_ in zip(rows, res)]
        for b in range(nb):
            s_out_ref[b, h] = (state[b] * g_tot[b * t:b * t + 1, :]
                               + _dot_tn(k_dec[rows[b]].astype(BF16), v_new[b].astype(BF16)))
        o = (jnp.concatenate([r_[t:] for r_ in res], axis=0)
             + _dot(qk.astype(BF16), jnp.concatenate(v_new, axis=0).astype(BF16)))
        o_ref[:, h * dv:(h + 1) * dv] = _gated_out_norm(o, nw_ref[...], z_ref[:, h * dv:(h + 1) * dv])


def _delta_sample(x, cbuf, z, gb, s0, conv_w, norm_w, *, heads, dk, dv):
    rows, cdim = x.shape
    nbatch, hist, _ = cbuf.shape
    t = rows // nbatch
    assert t == SUBLANES and hist == conv_w.shape[0] - 1 and hist <= t and dk == dv
    nb = SAMPLE_GROUP if nbatch % SAMPLE_GROUP == 0 else 1
    r = nb * t
    blk = lambda n_: pl.BlockSpec((r, n_), lambda i: (i, 0))
    sblk = pl.BlockSpec((nb, heads, dk, dv), lambda i: (i, 0, 0, 0))
    return pl.pallas_call(
        functools.partial(_delta_sample_kernel, heads=heads, dk=dk, dv=dv, conv_w=conv_w.shape[0]),
        grid=(nbatch // nb,),
        in_specs=[blk(cdim), pl.BlockSpec((nb, hist, cdim), lambda i: (i, 0, 0)), blk(z.shape[1]),
                  blk(gb.shape[1]), sblk, _const_spec(conv_w.shape), _const_spec(norm_w.shape)],
        out_specs=[blk(heads * dv), sblk],
        out_shape=[jax.ShapeDtypeStruct((rows, heads * dv), BF16), jax.ShapeDtypeStruct(s0.shape, F32)],
        scratch_shapes=[pltpu.VMEM((nb, 2 * SUBLANES, cdim), F32)],
        compiler_params=_params(("parallel",)),
        name="delta_sample",
    )(x, cbuf, z, gb, s0, conv_w, norm_w)


def _layer(xp, xs, ck, cv, cconv, sdelta, lw, dims):
    (f1_pre, f1_post, f1_up, f1_down, mix_pre, mix_post, w_in, sinks, conv_w, a_log, dt_bias,
     out_norm, w_a, w_b, w_o, f2_pre, f2_post, f2_up, f2_down) = lw
    batch, seq, d = xp.shape
    dbatch, dseq, _ = xs.shape
    kvh, hd = dims["kv_heads"], dims["head_dim"]
    hb, dk, dv = dims["b_heads"], dims["dk"], dims["dv"]
    a_heads = w_a.shape[0] // hd
    cdim = conv_w.shape[1]
    d_ff = f1_down.shape[0]
    bf = lambda a: a.astype(BF16)
    row = lambda a: a.reshape(1, -1).astype(F32)

    sizes = (a_heads * hd, kvh * hd, kvh * hd, cdim, hb * dv, hb, hb, d, d)
    offs = [0]
    for s_ in sizes:
        offs.append(offs[-1] + s_)
    w_in_bf = bf(w_in)
    wga, wgb = w_in_bf[:, offs[7]:offs[8]], w_in_bf[:, offs[8]:offs[9]]
    proj_offs = (offs[0], offs[1], offs[3], offs[4], offs[5])
    alog = jnp.pad(row(a_log), ((0, 0), (0, LANES - hb)))
    dtb = jnp.pad(row(dt_bias), ((0, 0), (0, LANES - hb)))

    def ffn(xa, xb, pre, post, up, down):
        return _ffn(xa, xb, row(pre), row(post), bf(up), bf(down))

    def inproj(x, **conv):
        return _inproj(x, row(mix_pre), w_in_bf, alog, dtb, q_scale=hd ** -0.5, n_heads_b=hb,
                       offs=proj_offs, **conv)

    def merge(group_a, group_b):
        return _merge(group_a, group_b, row(mix_pre), row(mix_post), wga, wgb, bf(w_a), bf(w_b), bf(w_o))

    sinks = sinks.astype(F32)
    norm_w = row(out_norm)
    conv_w = conv_w.astype(F32)
    hist = conv_w.shape[0] - 1

    x, x_s = ffn(xp.reshape(batch * seq, d), xs.reshape(dbatch * dseq, d), f1_pre, f1_post, f1_up, f1_down)
    q, kv, xb, z, gb, tail = inproj(x, conv_w=conv_w, seq_rows=seq, dk=dk)
    oa, kv_last = _swa_prompt(sinks, q, kv, batch=batch, kv_heads=kvh, head_dim=hd)
    ob, p_state = _delta_prompt(xb, z, gb, norm_w, batch=batch, heads=hb, dk=dk, dv=dv)
    prompt_mix = (x, oa, ob)
    kv_last = kv_last.reshape(batch, WINDOW, 2, kvh, hd)
    p_k, p_v = kv_last[:, :, 0], kv_last[:, :, 1]
    p_conv = tail.reshape(batch, -1, SUBLANES, cdim)[:, -1, SUBLANES - hist:]

    x = x_s
    q, kv, xb, z, gb = inproj(x)
    wb = ck.shape[1]
    oa, s_k, s_v = _swa_sample(sinks, q.reshape(dbatch, dseq, -1), kv.reshape(dbatch, dseq, -1),
                               ck.reshape(dbatch, wb, kvh * hd), cv.reshape(dbatch, wb, kvh * hd),
                               kv_heads=kvh, head_dim=hd)
    ob, s_state = _delta_sample(xb, cconv, z, gb, sdelta, conv_w, norm_w, heads=hb, dk=dk, dv=dv)
    x_p, x_s = merge(prompt_mix, (x, oa.reshape(dbatch * dseq, -1), ob))
    yp, ys = ffn(x_p, x_s, f2_pre, f2_post, f2_up, f2_down)
    yp, ys = yp.reshape(batch, seq, d), ys.reshape(dbatch, dseq, d)
    s_k = s_k.reshape(dbatch, wb, kvh, hd)
    s_v = s_v.reshape(dbatch, wb, kvh, hd)
    s_conv = xb.reshape(dbatch, dseq, cdim)[:, dseq - hist:]
    return yp, ys, (p_k, p_v, p_conv, p_state), (s_k, s_v, s_conv, s_state)


def kernel(x_prompt, x_sample, cache_swa_k, cache_swa_v, state_conv, state_delta, ffn1_norm_pre, ffn1_norm_post, ffn1_w_up, ffn1_w_down, mix_norm_pre, mix_norm_post, w_in, attn_sinks, conv_w, dn_a_log, dn_dt_bias, dn_out_norm, w_branch_a, w_branch_b, w_out, ffn2_norm_pre, ffn2_norm_post, ffn2_w_up, ffn2_w_down):
    weights = (ffn1_norm_pre, ffn1_norm_post, ffn1_w_up, ffn1_w_down, mix_norm_pre, mix_norm_post,
               w_in, attn_sinks, conv_w, dn_a_log, dn_dt_bias, dn_out_norm, w_branch_a, w_branch_b,
               w_out, ffn2_norm_pre, ffn2_norm_post, ffn2_w_up, ffn2_w_down)
    depth = w_in.shape[0]
    dims = dict(kv_heads=cache_swa_k.shape[3], head_dim=cache_swa_k.shape[4],
                b_heads=state_delta.shape[2], dk=state_delta.shape[3], dv=state_delta.shape[4])
    yp, ys = x_prompt, x_sample
    p_out, s_out = [], []
    for l in range(depth):
        yp, ys, p_new, s_new = _layer(yp, ys, cache_swa_k[l], cache_swa_v[l], state_conv[l],
                                      state_delta[l], tuple(w[l] for w in weights), dims)
        p_out.append(p_new)
        s_out.append(s_new)
    stack = lambda outs, i: jnp.stack([o[i] for o in outs])
    return (yp, ys,
            stack(p_out, 0), stack(p_out, 1), stack(p_out, 2), stack(p_out, 3),
            stack(s_out, 0), stack(s_out, 1), stack(s_out, 2), stack(s_out, 3))
```

```python
import functools

import jax
import jax.numpy as jnp
from jax import lax
from jax.experimental import pallas as pl
from jax.experimental.pallas import tpu as pltpu

WINDOW = 128
DN_CHUNK = 64
EPS = 1e-6
F32 = jnp.float32
BF16 = jnp.bfloat16
NEG_BIG = -1e30
LANES = 128
SUBLANES = 8
VMEM_LIMIT_BYTES = 56 * 1024 * 1024
FF_CHUNK = 256
MXU_COLS = 256
CONV_ROW_BLOCK = 128
SAMPLE_GROUP = 16
SWA_SAMPLE_GROUP = 16
SWA_BLOCKS_PER_STEP = 4
STATE_CHUNKS_PER_STEP = 4
PROMPT_CHUNKS_PER_STEP = 8


def _rms(x, w):
    return x * lax.rsqrt(jnp.mean(x * x, axis=-1, keepdims=True) + EPS) * w


def _silu(x):
    return x * jax.nn.sigmoid(x)


def _unit(x):
    return x * lax.rsqrt(jnp.sum(x * x, axis=-1, keepdims=True) + EPS)


def _softplus(x):
    return jnp.maximum(x, 0.0) + jnp.log1p(jnp.exp(-jnp.abs(x)))


def _dot(a, b):
    return jnp.dot(a, b, preferred_element_type=F32)


def _dot_nt(a, b):
    return lax.dot_general(a, b, (((1,), (1,)), ((), ())), preferred_element_type=F32)


def _dot_tn(a, b):
    return lax.dot_general(a, b, (((0,), (0,)), ((), ())), preferred_element_type=F32)


def _split3(b):
    hi = b.astype(BF16)
    r = b - hi.astype(F32)
    mid = r.astype(BF16)
    lo = (r - mid.astype(F32)).astype(BF16)
    return hi, mid, lo


def _dot_sel(sel_bf16, b):
    hi, mid, lo = _split3(b)
    return _dot(sel_bf16, hi) + _dot(sel_bf16, mid) + _dot(sel_bf16, lo)


def _ffn_tile(x_ref, pre_ref, post_ref, wup_ref, wd_ref, o_ref):
    x = x_ref[...]
    xn = _rms(x, pre_ref[...]).astype(BF16)
    d_ff = wd_ref.shape[0]
    acc = jnp.zeros(x.shape, F32)
    for c in range(d_ff // FF_CHUNK):
        sl = slice(c * FF_CHUNK, (c + 1) * FF_CHUNK)
        g = _dot(xn, wup_ref[:, sl])
        u = _dot(xn, wup_ref[:, d_ff + c * FF_CHUNK:d_ff + (c + 1) * FF_CHUNK])
        acc = acc + _dot((_silu(g) * u).astype(BF16), wd_ref[sl, :])
    o_ref[...] = x + 0.5 * _rms(acc, post_ref[...])


def _ffn_kernel(xa_ref, xb_ref, pre_ref, post_ref, wup_ref, wd_ref, oa_ref, ob_ref, *, tiles_a):
    i = pl.program_id(0)

    @pl.when(i < tiles_a)
    def _():
        _ffn_tile(xa_ref, pre_ref, post_ref, wup_ref, wd_ref, oa_ref)

    @pl.when(i >= tiles_a)
    def _():
        _ffn_tile(xb_ref, pre_ref, post_ref, wup_ref, wd_ref, ob_ref)


def _const_spec(shape):
    nd = len(shape)
    return pl.BlockSpec(shape, lambda *_: (0,) * nd)


def _weight_spec(shape):
    nd = len(shape)
    return pl.BlockSpec(shape, lambda *_: (0,) * nd, pipeline_mode=pl.Buffered(1))


def _row_tile(t, largest=512):
    for tm in (1024, 512, 256, 128, 64, 32, 16, 8):
        if tm <= largest and t % tm == 0:
            return tm
    raise ValueError(f"token count {t} is not a multiple of 8")


def _params(sem):
    return pltpu.CompilerParams(dimension_semantics=sem, vmem_limit_bytes=VMEM_LIMIT_BYTES)


def _ffn(xa, xb, pre, post, wup, wd):
    (ta, d), tb = xa.shape, xb.shape[0]
    f = wd.shape[0]
    assert f % FF_CHUNK == 0 and wup.shape == (d, 2 * f)
    tm = min(_row_tile(ta), _row_tile(tb))
    na, nb = ta // tm, tb // tm
    row_a = pl.BlockSpec((tm, d), lambda i: (jnp.minimum(i, na - 1), 0))
    row_b = pl.BlockSpec((tm, d), lambda i: (jnp.maximum(i - na, 0), 0))
    return pl.pallas_call(
        functools.partial(_ffn_kernel, tiles_a=na),
        grid=(na + nb,),
        in_specs=[row_a, row_b, _const_spec((1, d)), _const_spec((1, d)),
                  _weight_spec((d, 2 * f)), _weight_spec((f, d))],
        out_specs=[row_a, row_b],
        out_shape=[jax.ShapeDtypeStruct((ta, d), F32), jax.ShapeDtypeStruct((tb, d), F32)],
        compiler_params=_params(("arbitrary",)),
        name="ffn",
    )(xa, xb, pre, post, wup, wd)


def _conv_silu_unit(xc_ref, cv_ref, cw_ref, r0, rows, cb, heads, dk):
    taps = cw_ref.shape[0]
    rs = slice(r0, r0 + rows)
    cs = slice(cb * dk, (cb + 1) * dk)
    acc = None
    for jj in range(taps):
        off = r0 + SUBLANES - (taps - 1) + jj
        term = xc_ref[off:off + rows, cs] * cw_ref[jj:jj + 1, cs]
        acc = term if acc is None else acc + term
    y = _silu(acc)
    if cb < 2 * heads:
        y = _unit(y) * (dk ** -0.5) if cb < heads else _unit(y)
    cv_ref[rs, cs] = y


def _inproj_kernel(x_ref, pre_ref, w_ref, alog_ref, dtb_ref, *rest, q_scale, n_heads_b, offs, conv):
    tm = x_ref.shape[0]
    if conv is None:
        q_ref, kv_ref, b_ref, z_ref, gb_ref, h_ref = rest
        sub = tm
    else:
        cw_ref, q_ref, kv_ref, b_ref, z_ref, gb_ref, tail_ref, h_ref, xc_ref = rest
        seq_rows, dk = conv
        sub = min(tm, CONV_ROW_BLOCK)

        @pl.when((pl.program_id(0) * tm) % seq_rows == 0)
        def _():
            xc_ref[0:SUBLANES, :] = jnp.zeros((SUBLANES, xc_ref.shape[1]), F32)

    h_ref[...] = _rms(x_ref[...], pre_ref[...]).astype(BF16)

    def project(r0, i, c0, width):
        rs = slice(r0, r0 + sub)
        cs = slice(c0, c0 + width)
        y = _dot(h_ref[rs, :], w_ref[:, offs[i] + c0:offs[i] + c0 + width])
        if i == 0:
            q_ref[rs, cs] = (y * q_scale).astype(BF16)
        elif i == 1:
            kv_ref[rs, cs] = y
        elif i == 2 and conv is None:
            b_ref[rs, cs] = y
        elif i == 2:
            xc_ref[SUBLANES + r0:SUBLANES + r0 + sub, cs] = y
        elif i == 3:
            z_ref[rs, cs] = y
        else:
            g = -jnp.exp(alog_ref[...]) * _softplus(y + dtb_ref[...])
            lane = lax.broadcasted_iota(jnp.int32, y.shape, 1)
            gb_ref[rs, :] = jnp.where(lane < n_heads_b, g, jax.nn.sigmoid(y))

    widths = [offs[i + 1] - offs[i] for i in range(4)] + [LANES]
    pieces = [(i, c0, min(MXU_COLS, widths[i] - c0))
              for i in (2, 0, 1, 3, 4) for c0 in range(0, widths[i], MXU_COLS)]
    for r0 in range(0, tm + (sub if conv is not None else 0), sub):
        mm = [functools.partial(project, r0, *p) for p in pieces] if r0 < tm else []
        cv = []
        if conv is not None and r0 > 0:
            cv = [functools.partial(_conv_silu_unit, xc_ref, b_ref, cw_ref, r0 - sub, sub, cb, n_heads_b, dk)
                  for cb in range(xc_ref.shape[1] // dk)]
        for k in range(max(len(mm), len(cv))):
            if k < len(mm):
                mm[k]()
            if k < len(cv):
                cv[k]()
    if conv is not None:
        tail_ref[0] = xc_ref[tm:tm + SUBLANES, :]
        xc_ref[0:SUBLANES, :] = xc_ref[tm:tm + SUBLANES, :]


def _inproj(x, pre, w, alog, dtb, *, q_scale, n_heads_b, offs, conv_w=None, seq_rows=None, dk=None):
    t, d = x.shape
    tm = _row_tile(t)
    assert all(o % LANES == 0 for o in offs[:5]) and offs[4] + LANES <= w.shape[1]
    row = lambda n: pl.BlockSpec((tm, n), lambda i: (i, 0))
    cdim = offs[3] - offs[2]
    outs = [(offs[1] - offs[0], BF16), (offs[2] - offs[1], F32), (cdim, F32), (offs[4] - offs[3], F32),
            (LANES, F32)]
    in_specs = [row(d), _const_spec((1, d)), _weight_spec(w.shape),
                _const_spec(alog.shape), _const_spec(dtb.shape)]
    out_specs = [row(n) for n, _ in outs]
    out_shape = [jax.ShapeDtypeStruct((t, n), dt) for n, dt in outs]
    args, scratch, conv = [x, pre, w, alog, dtb], [pltpu.VMEM((tm, d), BF16)], None
    if conv_w is not None:
        assert seq_rows % tm == 0 and conv_w.shape[0] - 1 <= SUBLANES
        in_specs.append(_const_spec(conv_w.shape))
        args.append(conv_w)
        out_specs.append(pl.BlockSpec((1, SUBLANES, cdim), lambda i: (i, 0, 0)))
        out_shape.append(jax.ShapeDtypeStruct((t // tm, SUBLANES, cdim), F32))
        scratch.append(pltpu.VMEM((tm + SUBLANES, cdim), F32))
        conv = (seq_rows, dk)
    return pl.pallas_call(
        functools.partial(_inproj_kernel, q_scale=q_scale, n_heads_b=n_heads_b, offs=offs, conv=conv),
        grid=(t // tm,),
        in_specs=in_specs,
        out_specs=out_specs,
        out_shape=out_shape,
        scratch_shapes=scratch,
        compiler_params=_params(("arbitrary",)),
        name="inproj",
    )(*args)


def _merge_tile(x_ref, oa_ref, ob_ref, pre_ref, post_ref, wg_ref, wa_ref, wb_ref, wo_ref, o_ref):
    x = x_ref[...]
    h = _rms(x, pre_ref[...]).astype(BF16)
    oa, ob = oa_ref[...], ob_ref[...]
    m = jnp.zeros(x.shape, F32)
    d = wo_ref.shape[0]
    for c0 in range(0, d, MXU_COLS):
        cs = slice(c0, c0 + MXU_COLS)
        merged = (jax.nn.sigmoid(_dot(h, wg_ref[:, cs])) * _dot(oa, wa_ref[:, cs])
                  + jax.nn.sigmoid(_dot(h, wg_ref[:, d + c0:d + c0 + MXU_COLS])) * _dot(ob, wb_ref[:, cs]))
        m = m + _dot(merged.astype(BF16), wo_ref[cs, :])
    o_ref[...] = x + _rms(m, post_ref[...])


def _merge_kernel(xa_ref, oaa_ref, oba_ref, xb_ref, oab_ref, obb_ref, *rest, tiles_a):
    *shared, oa_ref, ob_ref = rest
    i = pl.program_id(0)

    @pl.when(i < tiles_a)
    def _():
        _merge_tile(xa_ref, oaa_ref, oba_ref, *shared, oa_ref)

    @pl.when(i >= tiles_a)
    def _():
        _merge_tile(xb_ref, oab_ref, obb_ref, *shared, ob_ref)


def _merge(group_a, group_b, pre, post, wg, wa, wb, wo):
    (ta, d), tb = group_a[0].shape, group_b[0].shape[0]
    tm = min(_row_tile(ta, largest=1024), _row_tile(tb, largest=1024))
    na, nb = ta // tm, tb // tm
    rows_a = [pl.BlockSpec((tm, a.shape[1]), lambda i: (jnp.minimum(i, na - 1), 0)) for a in group_a]
    rows_b = [pl.BlockSpec((tm, a.shape[1]), lambda i: (jnp.maximum(i - na, 0), 0)) for a in group_b]
    weights = (wg, wa, wb, wo)
    return pl.pallas_call(
        functools.partial(_merge_kernel, tiles_a=na),
        grid=(na + nb,),
        in_specs=rows_a + rows_b + [_const_spec((1, d)), _const_spec((1, d))]
        + [_weight_spec(w.shape) for w in weights],
        out_specs=[rows_a[0], rows_b[0]],
        out_shape=[jax.ShapeDtypeStruct((ta, d), F32), jax.ShapeDtypeStruct((tb, d), F32)],
        compiler_params=_params(("arbitrary",)),
        name="merge",
    )(*group_a, *group_b, pre, post, *weights)


def _sink_softmax(s, mask, sink):
    s = jnp.where(mask, s, NEG_BIG)
    m = jnp.maximum(jnp.max(s, axis=-1, keepdims=True), sink)
    e = jnp.exp(s - m)
    return e / (jnp.sum(e, axis=-1, keepdims=True) + jnp.exp(sink - m))


def _swa_prompt_kernel(sink_ref, q_ref, kvp_ref, kvc_ref, o_ref, kv_last_ref, *, kv_heads, group,
                       head_dim):
    n = pl.program_id(1)
    w = WINDOW
    blocks = q_ref.shape[0] // w
    dkv = kv_heads * head_dim
    pairs = group // 2
    kv_all = jnp.concatenate([kvp_ref[...], kvc_ref[...]], axis=0)

    @pl.when(n == pl.num_programs(1) - 1)
    def _():
        kv_last_ref[0] = kvc_ref[(blocks - 1) * w:blocks * w, :]
    lo = lax.broadcasted_iota(jnp.int32, (kv_all.shape[0], dkv), 1) < head_dim

    def lane_halves(x):
        x_rot = pltpu.roll(x, head_dim, axis=1)
        pick = lambda kv, half: jnp.where(lo if half == 0 else ~lo, x if kv == half else x_rot, 0.0)
        return [[pick(kv, half).astype(BF16) for half in range(2)] for kv in range(kv_heads)]

    k_ext = lane_halves(kv_all[:, :dkv])
    v_ext = lane_halves(kv_all[:, dkv:])

    i = lax.broadcasted_iota(jnp.int32, (w, 2 * w), 0)
    j = lax.broadcasted_iota(jnp.int32, (w, 2 * w), 1)
    bias = jnp.where(j > i, jnp.where(j <= i + w, 0.0, NEG_BIG), NEG_BIG)
    bias_seq_start = jnp.where(j >= w, bias, NEG_BIG)
    lo_out = lax.broadcasted_iota(jnp.int32, (w, 2 * head_dim), 1) < head_dim

    ones_half = [jnp.where(lo if half == 0 else ~lo, 1.0, 0.0).astype(BF16) for half in range(2)]
    v_ext = [[jnp.concatenate([v_ext[kv][half], ones_half[half]], axis=1) for half in range(2)]
             for kv in range(kv_heads)]

    inst = [(t, kv) for t in range(blocks) for kv in range(kv_heads)]
    band = lambda x, t: x[t * w:(t + 2) * w]
    q2 = [jnp.concatenate([q_ref[t * w:(t + 1) * w, (kv * pairs + p) * LANES:(kv * pairs + p + 1) * LANES]
                           for p in range(pairs)], axis=0) for t, kv in inst]
    scores = [[_dot_nt(q_, band(k_ext[kv][half], t)) for half in range(2)]
              for q_, (t, kv) in zip(q2, inst)]
    bias_first = jnp.where(n > 0, bias, bias_seq_start)
    for (t, kv), sc in zip(inst, scores):
        vband = jnp.concatenate([band(v_ext[kv][0], t), band(v_ext[kv][1], t)], axis=0)
        for p in range(pairs):
            exps, sink_terms = [], []
            for half in range(2):
                sink = sink_ref[kv * group + 2 * p + half]
                s = sc[half][p * w:(p + 1) * w] + (bias_first if t == 0 else bias)
                m = jnp.maximum(jnp.max(s, axis=-1, keepdims=True), sink)
                exps.append(jnp.exp(s - m).astype(BF16))
                sink_terms.append(jnp.exp(sink - m))
            o = _dot(jnp.concatenate(exps, axis=1), vband)
            den = o[:, LANES:] + jnp.where(lo_out, sink_terms[0], sink_terms[1])
            cs = slice((kv * pairs + p) * LANES, (kv * pairs + p + 1) * LANES)
            o_ref[t * w:(t + 1) * w, cs] = (o[:, :LANES] / den).astype(o_ref.dtype)


def _swa_prompt(sinks, q, kv, *, batch, kv_heads, head_dim):
    t, dq = q.shape
    seq = t // batch
    assert seq % WINDOW == 0
    group = dq // (kv_heads * head_dim)
    assert 2 * head_dim == LANES and kv_heads == 2 and group % 2 == 0
    blocks = SWA_BLOCKS_PER_STEP if (seq // WINDOW) % SWA_BLOCKS_PER_STEP == 0 else 1
    nb = seq // (blocks * WINDOW)
    rows = blocks * WINDOW
    return pl.pallas_call(
        functools.partial(_swa_prompt_kernel, kv_heads=kv_heads, group=group, head_dim=head_dim),
        grid=(batch, nb),
        in_specs=[pl.BlockSpec(memory_space=pltpu.SMEM),
                  pl.BlockSpec((rows, dq), lambda b, n: (b * nb + n, 0)),
                  pl.BlockSpec((WINDOW, kv.shape[1]),
                               lambda b, n: ((b * nb + n) * blocks - jnp.minimum(n, 1), 0)),
                  pl.BlockSpec((rows, kv.shape[1]), lambda b, n: (b * nb + n, 0))],
        out_specs=[pl.BlockSpec((rows, dq), lambda b, n: (b * nb + n, 0)),
                   pl.BlockSpec((1, WINDOW, kv.shape[1]), lambda b, n: (b, 0, 0))],
        out_shape=[jax.ShapeDtypeStruct((t, dq), BF16),
                   jax.ShapeDtypeStruct((batch, WINDOW, kv.shape[1]), F32)],
        compiler_params=_params(("parallel", "arbitrary")),
        name="swa_prompt",
    )(sinks, q, kv, kv)


def _swa_sample_kernel(sink_ref, q_ref, kvn_ref, ck_ref, cv_ref, o_ref, nk_ref, nv_ref, *,
                       kv_heads, group, head_dim):
    bb, t, _ = q_ref.shape
    wb = ck_ref.shape[1]
    dkv = kv_heads * head_dim
    q = q_ref[...]
    kc = jnp.concatenate([ck_ref[...], kvn_ref[:, :, :dkv]], axis=1)
    vc = jnp.concatenate([cv_ref[...], kvn_ref[:, :, dkv:]], axis=1)
    nk_ref[...] = kc[:, t:, :]
    nv_ref[...] = vc[:, t:, :]
    kcb = kc.astype(BF16)
    vcb = vc.astype(BF16)
    assert t & (t - 1) == 0
    i = lax.broadcasted_iota(jnp.int32, (group * t, wb + t), 0) & (t - 1)
    j = lax.broadcasted_iota(jnp.int32, (group * t, wb + t), 1)
    diff = wb + i - j
    mask = ((diff >= 0) & (diff < WINDOW))[None]
    hs = [slice(kv * head_dim, (kv + 1) * head_dim) for kv in range(kv_heads)]
    heads = [[kv * group + g for g in range(group)] for kv in range(kv_heads)]
    qs = [jnp.concatenate([q[:, :, h * head_dim:(h + 1) * head_dim] for h in hd_], axis=1) for hd_ in heads]
    sinks = [jnp.concatenate([jnp.full((1, t, 1), sink_ref[h], F32) for h in hd_], axis=1) for hd_ in heads]
    scores = [jnp.einsum('bqd,bkd->bqk', q_, kcb[:, :, h_], preferred_element_type=F32)
              for q_, h_ in zip(qs, hs)]
    probs = [_sink_softmax(s, mask, sink).astype(BF16) for s, sink in zip(scores, sinks)]
    outs = [jnp.einsum('bqk,bkd->bqd', p, vcb[:, :, h_], preferred_element_type=F32)
            for p, h_ in zip(probs, hs)]
    o_ref[...] = jnp.concatenate([o[:, g * t:(g + 1) * t, :] for o in outs for g in range(group)],
                                 axis=2).astype(o_ref.dtype)


def _swa_sample(sinks, q, kvn, ck, cv, *, kv_heads, head_dim):
    nbatch, t, dq = q.shape
    wb = ck.shape[1]
    group = dq // (kv_heads * head_dim)
    bb = SWA_SAMPLE_GROUP if nbatch % SWA_SAMPLE_GROUP == 0 else 1
    blk = lambda a: pl.BlockSpec((bb,) + a.shape[1:], lambda i: (i, 0, 0))
    return pl.pallas_call(
        functools.partial(_swa_sample_kernel, kv_heads=kv_heads, group=group, head_dim=head_dim),
        grid=(nbatch // bb,),
        in_specs=[pl.BlockSpec(memory_space=pltpu.SMEM), blk(q), blk(kvn), blk(ck), blk(cv)],
        out_specs=[blk(q), blk(ck), blk(cv)],
        out_shape=[jax.ShapeDtypeStruct(q.shape, BF16), jax.ShapeDtypeStruct(ck.shape, F32),
                   jax.ShapeDtypeStruct(cv.shape, F32)],
        compiler_params=_params(("parallel",)),
        name="swa_sample",
    )(sinks, q, kvn, ck, cv)


def _chunk_masks(r, c):
    assert r & (r - 1) == 0 and c & (c - 1) == 0 and LANES % r == 0
    width = 2 * LANES
    ri = lax.broadcasted_iota(jnp.int32, (r, width), 0)
    cj = lax.broadcasted_iota(jnp.int32, (r, width), 1)
    ci = cj & (r - 1)
    incl, strict = ci <= ri, ci < ri
    if r != c:
        shift = c.bit_length() - 1
        same = (ri >> shift) == (ci >> shift)
        incl, strict = same & incl, same & strict
    f01 = lambda mask: jnp.where(mask, 1.0, 0.0)
    block = cj >> (r.bit_length() - 1)
    first_in_tile = lax.broadcasted_iota(jnp.int32, (r, LANES), 1) < r
    return dict(incl=incl, strict=strict, eye=f01(ri == ci), first_in_tile=first_in_tile,
                keep=[f01(block == x).astype(BF16) for x in range(width // r)],
                incl_sel=f01(incl)[:, :r].astype(BF16),
                same_sel=None if r == c else f01(same)[:, :r].astype(BF16))


def _delta_gates(gb, masks, heads, dk):
    r = gb.shape[0]
    gc_all = _dot_sel(masks["incl_sel"], gb)
    if masks["same_sel"] is None:
        gl_all = jnp.broadcast_to(gc_all[r - 1:r, :], gc_all.shape)
    else:
        gl_all = _dot_sel(masks["same_sel"], gb)
    bc = lambda a, lane: jnp.broadcast_to(a[:, lane:lane + 1], (r, dk))
    return [(bc(gc_all, h), bc(gl_all, h), bc(gb, heads + h)) for h in range(heads)]


def _split2(x):
    hi = x.astype(BF16)
    return hi, (x - hi.astype(F32)).astype(BF16)


def _rows(xs):
    return jnp.concatenate(xs, axis=0)


def _lanes(xs):
    return xs[0] if len(xs) == 1 else jnp.concatenate(xs, axis=1)


def _block_diag(blocks):
    n = len(blocks)
    z = jnp.zeros(blocks[0].shape, BF16)
    return _rows([_lanes([blocks[i] if j == i else z for j in range(n)]) for i in range(n)])


def _delta_intra(insts, masks, c):
    r, dk = insts[0][0].shape
    dv = insts[0][2].shape[1]
    m = LANES // r
    n = 2 * m
    assert dk == LANES and len(insts) % n == 0
    incl, strict, eye, keep = masks["incl"], masks["strict"], masks["eye"], masks["keep"]
    tiles = [list(range(i, i + m)) for i in range(0, len(insts), m)]
    groups = [(2 * i, 2 * i + 1) for i in range(len(tiles) // 2)]
    qs, ks, vs, gcs, gls, betas = ([i[j] for i in insts] for j in range(6))
    bf = lambda x: x.astype(BF16)

    def lane_tile(xs):
        return xs[0] if m == 1 else jnp.where(masks["first_in_tile"], xs[0], xs[1])

    def block_diag_x(x):
        return _rows([x * keep[i] for i in range(n)])

    g_cols = [_lanes([lane_tile([gcs[i] for i in tiles[t]]) for t in g]) for g in groups]
    g_rows = [jnp.transpose(_rows([gcs[i] for t in g for i in tiles[t]]))[:r, :] for g in groups]
    decays = [jnp.where(incl, jnp.exp(gc - gr), 0.0) for gc, gr in zip(g_cols, g_rows)]
    egcs = [jnp.exp(gc) for gc in gcs]
    kbs = [k * beta for k, beta in zip(ks, betas)]
    kq = [_dot_nt(_rows([_lanes([bf(kbs[i]) for i in t]), _lanes([bf(qs[i]) for i in t])]),
                  _block_diag([bf(ks[i]) for i in t])) for t in tiles]
    ms = [jnp.where(strict, _lanes([kq[t][:r] for t in g]) * decay, 0.0) for g, decay in zip(groups, decays)]
    qks = [_lanes([kq[t][r:] for t in g]) * decay for g, decay in zip(groups, decays)]
    rounds = max((c - 1).bit_length(), 1)
    ps = [-m_ for m_ in ms]
    tinvs = [None] * len(groups)
    for j in range(rounds):
        square = j + 1 < rounds
        if not square and tinvs[0] is None:
            tinvs = [eye + p for p in ps]
            break
        p_hl = [_split2(p) for p in ps]
        weights = [block_diag_x(hi) for hi, _ in p_hl]
        lhs = [_rows((list(hl) if square else []) + ([] if t is None else list(_split2(t))))
               for hl, t in zip(p_hl, tinvs)]
        res = [_dot(a, w) for a, w in zip(lhs, weights)]
        new_ps, new_ts = [], []
        for p, t, a in zip(ps, tinvs, res):
            o = 0
            if square:
                new_ps.append(a[:r] + a[r:2 * r])
                o = 2 * r
            new_ts.append(eye + p if t is None else t + (a[o:o + r] + a[o + r:o + 2 * r]))
        ps, tinvs = new_ps if square else ps, new_ts
    rhs = [jnp.concatenate([kb * egc, v * beta], axis=1).astype(BF16)
           for kb, egc, v, beta in zip(kbs, egcs, vs, betas)]
    t_hl = [_split2(t) for t in tinvs]
    sols = []
    for ti, t in enumerate(tiles):
        hi, lo = (x[:, (ti % 2) * LANES:(ti % 2 + 1) * LANES] for x in t_hl[ti // 2])
        a = _dot(_rows([hi, lo]), _block_diag([rhs[i] for i in t]))
        sols.append(a[:r] + a[r:])
    out = []
    for ti, t in enumerate(tiles):
        per_inst = []
        for slot, i in enumerate(t):
            s_ = sols[ti][:, slot * (dk + dv):(slot + 1) * (dk + dv)]
            per_inst.append((s_[:, :dk], s_[:, dk:], qs[i] * egcs[i], ks[i] * jnp.exp(gls[i] - gcs[i]),
                             jnp.exp(gls[i])))
        out.append((qks[ti // 2][:, (ti % 2) * LANES:(ti % 2 + 1) * LANES], per_inst))
    return out


def _gated_out_norm(o, w, z):
    return (_rms(o, w) * _silu(z)).astype(BF16)


def _head_cols(xb, h, heads, dk, dv):
    q = xb[:, h * dk:(h + 1) * dk]
    k = xb[:, (heads + h) * dk:(heads + h + 1) * dk]
    v = xb[:, 2 * heads * dk + h * dv:2 * heads * dk + (h + 1) * dv]
    return q, k, v


def _delta_prompt_intra_kernel(x_ref, gb_ref, kq_ref, vu_ref, kd_ref, qk_ref, gt_ref, *,
                               heads, dk, dv, chunk):
    rows = x_ref.shape[0]
    xb = x_ref[...]
    gb = gb_ref[...]
    masks = _chunk_masks(chunk, chunk)
    insts = []
    for g in range(rows // chunk):
        rs = slice(g * chunk, (g + 1) * chunk)
        gates = _delta_gates(gb[rs], masks, heads, dk)
        insts.extend(_head_cols(xb[rs], h, heads, dk, dv) + gates[h] for h in range(heads))
    intra = _delta_intra(insts, masks, chunk)

    tiles_per_chunk = len(intra) // (rows // chunk)
    for ti, (qk, per_inst) in enumerate(intra):
        g, tc = divmod(ti, tiles_per_chunk)
        qk_ref[0, g, tc] = qk.astype(BF16)
        for x, (k_cum, v_u, q_dec, k_dec, g_tot) in enumerate(per_inst):
            h = tc * len(per_inst) + x
            kq_ref[0, g, h] = _rows([k_cum, q_dec]).astype(BF16)
            vu_ref[0, g, h] = v_u
            kd_ref[0, g, h] = k_dec.astype(BF16)
            gt_ref[0, g, h] = g_tot[0:SUBLANES, :]


def _delta_prompt_state_kernel(kq_ref, vu_ref, kd_ref, qk_ref, gt_ref, z_ref, nw_ref, o_ref, s_out_ref,
                               s_ref):
    n = pl.program_id(0)
    batch, chunks, heads, c, dv = vu_ref.shape
    tiles = qk_ref.shape[2]
    m = heads // tiles

    @pl.when(n == 0)
    def _():
        s_ref[...] = jnp.zeros(s_ref.shape, F32)

    inst = [(b, h) for b in range(batch) for h in range(heads)]
    state = [s_ref[b, h] for b, h in inst]
    for j in range(chunks):
        res = [_dot(kq_ref[b, j, h], s.astype(BF16)) for (b, h), s in zip(inst, state)]
        v_new = [vu_ref[b, j, h] - r_[:c] for (b, h), r_ in zip(inst, res)]
        vn_bf = [vn.astype(BF16) for vn in v_new]
        o_intra = [_dot(qk_ref[b, j, t], _block_diag(vn_bf[(b * tiles + t) * m:(b * tiles + t + 1) * m]))
                   for b in range(batch) for t in range(tiles)]
        state = [s * gt_ref[b, j, h][0:1, :] + _dot_tn(kd_ref[b, j, h], vn)
                 for (b, h), s, vn in zip(inst, state, vn_bf)]
        for i, (b, h) in enumerate(inst):
            o = res[i][c:] + o_intra[i // m][:, (i % m) * dv:(i % m + 1) * dv]
            cs = slice(h * dv, (h + 1) * dv)
            o_ref[b, j, :, cs] = _gated_out_norm(o, nw_ref[...], z_ref[b, j, :, cs])
    for (b, h), s in zip(inst, state):
        s_ref[b, h] = s

    @pl.when(n == pl.num_programs(0) - 1)
    def _():
        s_out_ref[...] = s_ref[...]


def _delta_prompt(x, z, gb, norm_w, *, batch, heads, dk, dv):
    t, cdim = x.shape
    seq = t // batch
    c = DN_CHUNK if seq % DN_CHUNK == 0 else seq
    assert c % SUBLANES == 0 and dk == dv == LANES
    nc = seq // c
    group = PROMPT_CHUNKS_PER_STEP if nc % PROMPT_CHUNKS_PER_STEP == 0 else 1
    rows = group * c
    steps = seq // rows
    tiles = heads * c // LANES
    blk = lambda n_: pl.BlockSpec((rows, n_), lambda b, n: (b * steps + n, 0))
    per_chunk = lambda per, r_, dt: (
        pl.BlockSpec((1, group, per, r_, LANES), lambda b, n: (b, n, 0, 0, 0)),
        jax.ShapeDtypeStruct((batch, nc, per, r_, LANES), dt))
    outs = [per_chunk(heads, 2 * c, BF16), per_chunk(heads, c, F32), per_chunk(heads, c, BF16),
            per_chunk(tiles, c, BF16), per_chunk(heads, SUBLANES, F32)]
    kq, vu, kd, qk, gt = pl.pallas_call(
        functools.partial(_delta_prompt_intra_kernel, heads=heads, dk=dk, dv=dv, chunk=c),
        grid=(batch, steps),
        in_specs=[blk(cdim), blk(gb.shape[1])],
        out_specs=[o[0] for o in outs],
        out_shape=[o[1] for o in outs],
        compiler_params=_params(("parallel", "parallel")),
        name="delta_prompt_intra",
    )(x, gb)
    per_step = STATE_CHUNKS_PER_STEP if nc % STATE_CHUNKS_PER_STEP == 0 else 1
    all_seq = lambda a: pl.BlockSpec((batch, per_step) + a.shape[2:], lambda n: (0, n) + (0,) * (a.ndim - 2))
    z4 = z.reshape(batch, nc, c, heads * dv)
    o, s_out = pl.pallas_call(
        _delta_prompt_state_kernel,
        grid=(nc // per_step,),
        in_specs=[all_seq(kq), all_seq(vu), all_seq(kd), all_seq(qk), all_seq(gt), all_seq(z4),
                  _const_spec(norm_w.shape)],
        out_specs=[all_seq(z4), _const_spec((batch, heads, dk, dv))],
        out_shape=[jax.ShapeDtypeStruct(z4.shape, BF16), jax.ShapeDtypeStruct((batch, heads, dk, dv), F32)],
        scratch_shapes=[pltpu.VMEM((batch, heads, dk, dv), F32)],
        compiler_params=_params(("arbitrary",)),
        name="delta_prompt_state",
    )(kq, vu, kd, qk, gt, z4, norm_w)
    return o.reshape(t, heads * dv), s_out


def _delta_sample_kernel(x_ref, cbuf_ref, z_ref, gb_ref, s_in_ref, cw_ref, nw_ref, o_ref, s_out_ref,
                         xc_ref, *, heads, dk, dv, conv_w):
    nb, hist, cdim = cbuf_ref.shape
    r = x_ref.shape[0]
    t = r // nb
    xc_ref[:, SUBLANES:SUBLANES + t, :] = x_ref[...].reshape(nb, t, cdim)
    xc_ref[:, SUBLANES - hist:SUBLANES, :] = cbuf_ref[...]
    acc = None
    for jj in range(conv_w):
        off = SUBLANES - (conv_w - 1) + jj
        term = xc_ref[:, off:off + t, :] * cw_ref[jj:jj + 1, :][None]
        acc = term if acc is None else acc + term
    xb = _silu(acc).reshape(r, cdim)

    masks = _chunk_masks(r, t)
    gates = _delta_gates(gb_ref[...], masks, heads, dk)
    insts = []
    for h in range(heads):
        q, k, v = _head_cols(xb, h, heads, dk, dv)
        insts.append((_unit(q) * (dk ** -0.5), _unit(k), v) + gates[h])
    intra = _delta_intra(insts, masks, t)
    rows = [slice(b * t, (b + 1) * t) for b in range(nb)]
    for h in range(heads):
        qk, ((k_cum, v_u, q_dec, k_dec, g_tot),) = intra[h]
        state = [s_in_ref[b, h] for b in range(nb)]
        res = [_dot(jnp.concatenate([k_cum[rs], q_dec[rs]], axis=0).astype(BF16), s.astype(BF16))
               for rs, s in zip(rows, state)]
        v_new = [v_u[rs] - r_[:t] for rs, r_ in zip(rows, res)]
        for b in range(nb):
            s_out_ref[b, h] = (state[b] * g_tot[b * t:b * t + 1, :]
                               + _dot_tn(k_dec[rows[b]].astype(BF16), v_new[b].astype(BF16)))
        o = (jnp.concatenate([r_[t:] for r_ in res], axis=0)
             + _dot(qk.astype(BF16), jnp.concatenate(v_new, axis=0).astype(BF16)))
        o_ref[:, h * dv:(h + 1) * dv] = _gated_out_norm(o, nw_ref[...], z_ref[:, h * dv:(h + 1) * dv])


def _delta_sample(x, cbuf, z, gb, s0, conv_w, norm_w, *, heads, dk, dv):
    rows, cdim = x.shape
    nbatch, hist, _ = cbuf.shape
    t = rows // nbatch
    assert t == SUBLANES and hist == conv_w.shape[0] - 1 and hist <= t and dk == dv
    nb = SAMPLE_GROUP if nbatch % SAMPLE_GROUP == 0 else 1
    r = nb * t
    blk = lambda n_: pl.BlockSpec((r, n_), lambda i: (i, 0))
    sblk = pl.BlockSpec((nb, heads, dk, dv), lambda i: (i, 0, 0, 0))
    return pl.pallas_call(
        functools.partial(_delta_sample_kernel, heads=heads, dk=dk, dv=dv, conv_w=conv_w.shape[0]),
        grid=(nbatch // nb,),
        in_specs=[blk(cdim), pl.BlockSpec((nb, hist, cdim), lambda i: (i, 0, 0)), blk(z.shape[1]),
                  blk(gb.shape[1]), sblk, _const_spec(conv_w.shape), _const_spec(norm_w.shape)],
        out_specs=[blk(heads * dv), sblk],
        out_shape=[jax.ShapeDtypeStruct((rows, heads * dv), BF16), jax.ShapeDtypeStruct(s0.shape, F32)],
        scratch_shapes=[pltpu.VMEM((nb, 2 * SUBLANES, cdim), F32)],
        compiler_params=_params(("parallel",)),
        name="delta_sample",
    )(x, cbuf, z, gb, s0, conv_w, norm_w)


def _layer(xp, xs, ck, cv, cconv, sdelta, lw, dims):
    (f1_pre, f1_post, f1_up, f1_down, mix_pre, mix_post, w_in, sinks, conv_w, a_log, dt_bias,
     out_norm, w_a, w_b, w_o, f2_pre, f2_post, f2_up, f2_down) = lw
    batch, seq, d = xp.shape
    dbatch, dseq, _ = xs.shape
    kvh, hd = dims["kv_heads"], dims["head_dim"]
    hb, dk, dv = dims["b_heads"], dims["dk"], dims["dv"]
    a_heads = w_a.shape[0] // hd
    cdim = conv_w.shape[1]
    d_ff = f1_down.shape[0]
    bf = lambda a: a.astype(BF16)
    row = lambda a: a.reshape(1, -1).astype(F32)

    sizes = (a_heads * hd, kvh * hd, kvh * hd, cdim, hb * dv, hb, hb, d, d)
    offs = [0]
    for s_ in sizes:
        offs.append(offs[-1] + s_)
    w_proj = bf(w_in[:, :offs[5] + LANES])
    w_gates = bf(w_in[:, offs[7]:offs[9]])
    proj_offs = (offs[0], offs[1], offs[3], offs[4], offs[5])
    alog = jnp.pad(row(a_log), ((0, 0), (0, LANES - hb)))
    dtb = jnp.pad(row(dt_bias), ((0, 0), (0, LANES - hb)))

    def ffn(xa, xb, pre, post, up, down):
        return _ffn(xa, xb, row(pre), row(post), bf(up), bf(down))

    def inproj(x, **conv):
        return _inproj(x, row(mix_pre), w_proj, alog, dtb, q_scale=hd ** -0.5, n_heads_b=hb,
                       offs=proj_offs, **conv)

    def merge(group_a, group_b):
        return _merge(group_a, group_b, row(mix_pre), row(mix_post), w_gates, bf(w_a), bf(w_b), bf(w_o))

    sinks = sinks.astype(F32)
    norm_w = row(out_norm)
    conv_w = conv_w.astype(F32)
    hist = conv_w.shape[0] - 1

    x, x_s = ffn(xp.reshape(batch * seq, d), xs.reshape(dbatch * dseq, d), f1_pre, f1_post, f1_up, f1_down)
    q, kv, xb, z, gb, tail = inproj(x, conv_w=conv_w, seq_rows=seq, dk=dk)
    oa, kv_last = _swa_prompt(sinks, q, kv, batch=batch, kv_heads=kvh, head_dim=hd)
    ob, p_state = _delta_prompt(xb, z, gb, norm_w, batch=batch, heads=hb, dk=dk, dv=dv)
    prompt_mix = (x, oa, ob)
    kv_last = kv_last.reshape(batch, WINDOW, 2, kvh, hd)
    p_k, p_v = kv_last[:, :, 0], kv_last[:, :, 1]
    p_conv = tail.reshape(batch, -1, SUBLANES, cdim)[:, -1, SUBLANES - hist:]

    x = x_s
    q, kv, xb, z, gb = inproj(x)
    wb = ck.shape[1]
    oa, s_k, s_v = _swa_sample(sinks, q.reshape(dbatch, dseq, -1), kv.reshape(dbatch, dseq, -1),
                               ck.reshape(dbatch, wb, kvh * hd), cv.reshape(dbatch, wb, kvh * hd),
                               kv_heads=kvh, head_dim=hd)
    ob, s_state = _delta_sample(xb, cconv, z, gb, sdelta, conv_w, norm_w, heads=hb, dk=dk, dv=dv)
    x_p, x_s = merge(prompt_mix, (x, oa.reshape(dbatch * dseq, -1), ob))
    yp, ys = ffn(x_p, x_s, f2_pre, f2_post, f2_up, f2_down)
    yp, ys = yp.reshape(batch, seq, d), ys.reshape(dbatch, dseq, d)
    s_k = s_k.reshape(dbatch, wb, kvh, hd)
    s_v = s_v.reshape(dbatch, wb, kvh, hd)
    s_conv = xb.reshape(dbatch, dseq, cdim)[:, dseq - hist:]
    return yp, ys, (p_k, p_v, p_conv, p_state), (s_k, s_v, s_conv, s_state)


def kernel(x_prompt, x_sample, cache_swa_k, cache_swa_v, state_conv, state_delta, ffn1_norm_pre, ffn1_norm_post, ffn1_w_up, ffn1_w_down, mix_norm_pre, mix_norm_post, w_in, attn_sinks, conv_w, dn_a_log, dn_dt_bias, dn_out_norm, w_branch_a, w_branch_b, w_out, ffn2_norm_pre, ffn2_norm_post, ffn2_w_up, ffn2_w_down):
    weights = (ffn1_norm_pre, ffn1_norm_post, ffn1_w_up, ffn1_w_down, mix_norm_pre, mix_norm_post,
               w_in, attn_sinks, conv_w, dn_a_log, dn_dt_bias, dn_out_norm, w_branch_a, w_branch_b,
               w_out, ffn2_norm_pre, ffn2_norm_post, ffn2_w_up, ffn2_w_down)
    depth = w_in.shape[0]
    dims = dict(kv_heads=cache_swa_k.shape[3], head_dim=cache_swa_k.shape[4],
                b_heads=state_delta.shape[2], dk=state_delta.shape[3], dv=state_delta.shape[4])
    yp, ys = x_prompt, x_sample
    p_out, s_out = [], []
    for l in range(depth):
        yp, ys, p_new, s_new = _layer(yp, ys, cache_swa_k[l], cache_swa_v[l], state_conv[l],
                                      state_delta[l], tuple(w[l] for w in weights), dims)
        p_out.append(p_new)
        s_out.append(s_new)
    stack = lambda outs, i: jnp.stack([o[i] for o in outs])
    return (yp, ys,
            stack(p_out, 0), stack(p_out, 1), stack(p_out, 2), stack(p_out, 3),
            stack(s_out, 0), stack(s_out, 1), stack(s_out, 2), stack(s_out, 3))
```

```python
import functools

import jax
import jax.numpy as jnp
from jax import lax
from jax.experimental import pallas as pl
from jax.experimental.pallas import tpu as pltpu

WINDOW = 128
DN_CHUNK = 64
EPS = 1e-6
F32 = jnp.float32
BF16 = jnp.bfloat16
NEG_BIG = -1e30
LANES = 128
SUBLANES = 8
VMEM_LIMIT_BYTES = 56 * 1024 * 1024
FF_CHUNK = 256
MXU_COLS = 256
CONV_ROW_BLOCK = 128
SAMPLE_GROUP = 16
SWA_SAMPLE_GROUP = 16
SWA_BLOCKS_PER_STEP = 4
STATE_CHUNKS_PER_STEP = 4
PROMPT_CHUNKS_PER_STEP = 16


def _rms(x, w):
    return x * lax.rsqrt(jnp.mean(x * x, axis=-1, keepdims=True) + EPS) * w


def _silu(x):
    return x * jax.nn.sigmoid(x)


def _unit(x):
    return x * lax.rsqrt(jnp.sum(x * x, axis=-1, keepdims=True) + EPS)


def _softplus(x):
    return jnp.maximum(x, 0.0) + jnp.log1p(jnp.exp(-jnp.abs(x)))


def _dot(a, b):
    return jnp.dot(a, b, preferred_element_type=F32)


def _dot_nt(a, b):
    return lax.dot_general(a, b, (((1,), (1,)), ((), ())), preferred_element_type=F32)


def _dot_tn(a, b):
    return lax.dot_general(a, b, (((0,), (0,)), ((), ())), preferred_element_type=F32)


def _split3(b):
    hi = b.astype(BF16)
    r = b - hi.astype(F32)
    mid = r.astype(BF16)
    lo = (r - mid.astype(F32)).astype(BF16)
    return hi, mid, lo


def _dot_sel(sel_bf16, b):
    hi, mid, lo = _split3(b)
    return _dot(sel_bf16, hi) + _dot(sel_bf16, mid) + _dot(sel_bf16, lo)


def _ffn_tile(x_ref, pre_ref, post_ref, wup_ref, wd_ref, o_ref):
    x = x_ref[...]
    xn = _rms(x, pre_ref[...]).astype(BF16)
    d_ff = wd_ref.shape[0]
    acc = jnp.zeros(x.shape, F32)
    for c in range(d_ff // FF_CHUNK):
        sl = slice(c * FF_CHUNK, (c + 1) * FF_CHUNK)
        g = _dot(xn, wup_ref[:, sl])
        u = _dot(xn, wup_ref[:, d_ff + c * FF_CHUNK:d_ff + (c + 1) * FF_CHUNK])
        acc = acc + _dot((_silu(g) * u).astype(BF16), wd_ref[sl, :])
    o_ref[...] = x + 0.5 * _rms(acc, post_ref[...])


def _ffn_kernel(xa_ref, xb_ref, pre_ref, post_ref, wup_ref, wd_ref, oa_ref, ob_ref, *, tiles_a):
    i = pl.program_id(0)

    @pl.when(i < tiles_a)
    def _():
        _ffn_tile(xa_ref, pre_ref, post_ref, wup_ref, wd_ref, oa_ref)

    @pl.when(i >= tiles_a)
    def _():
        _ffn_tile(xb_ref, pre_ref, post_ref, wup_ref, wd_ref, ob_ref)


def _const_spec(shape):
    nd = len(shape)
    return pl.BlockSpec(shape, lambda *_: (0,) * nd)


def _weight_spec(shape):
    nd = len(shape)
    return pl.BlockSpec(shape, lambda *_: (0,) * nd, pipeline_mode=pl.Buffered(1))


def _row_tile(t, largest=512):
    for tm in (1024, 512, 256, 128, 64, 32, 16, 8):
        if tm <= largest and t % tm == 0:
            return tm
    raise ValueError(f"token count {t} is not a multiple of 8")


def _params(sem):
    return pltpu.CompilerParams(dimension_semantics=sem, vmem_limit_bytes=VMEM_LIMIT_BYTES)


def _ffn(xa, xb, pre, post, wup, wd):
    (ta, d), tb = xa.shape, xb.shape[0]
    f = wd.shape[0]
    assert f % FF_CHUNK == 0 and wup.shape == (d, 2 * f)
    tm = min(_row_tile(ta), _row_tile(tb))
    na, nb = ta // tm, tb // tm
    row_a = pl.BlockSpec((tm, d), lambda i: (jnp.minimum(i, na - 1), 0))
    row_b = pl.BlockSpec((tm, d), lambda i: (jnp.maximum(i - na, 0), 0))
    return pl.pallas_call(
        functools.partial(_ffn_kernel, tiles_a=na),
        grid=(na + nb,),
        in_specs=[row_a, row_b, _const_spec((1, d)), _const_spec((1, d)),
                  _weight_spec((d, 2 * f)), _weight_spec((f, d))],
        out_specs=[row_a, row_b],
        out_shape=[jax.ShapeDtypeStruct((ta, d), F32), jax.ShapeDtypeStruct((tb, d), F32)],
        compiler_params=_params(("arbitrary",)),
        name="ffn",
    )(xa, xb, pre, post, wup, wd)


def _conv_silu_unit(xc_ref, cv_ref, cw_ref, r0, rows, cb, heads, dk):
    taps = cw_ref.shape[0]
    rs = slice(r0, r0 + rows)
    cs = slice(cb * dk, (cb + 1) * dk)
    acc = None
    for jj in range(taps):
        off = r0 + SUBLANES - (taps - 1) + jj
        term = xc_ref[off:off + rows, cs] * cw_ref[jj:jj + 1, cs]
        acc = term if acc is None else acc + term
    y = _silu(acc)
    if cb < 2 * heads:
        y = _unit(y) * (dk ** -0.5) if cb < heads else _unit(y)
    cv_ref[rs, cs] = y


def _inproj_kernel(x_ref, pre_ref, w_ref, alog_ref, dtb_ref, *rest, q_scale, n_heads_b, offs, conv):
    tm = x_ref.shape[0]
    if conv is None:
        q_ref, kv_ref, b_ref, z_ref, gb_ref, h_ref = rest
        sub = tm
    else:
        cw_ref, q_ref, kv_ref, b_ref, z_ref, gb_ref, tail_ref, h_ref, xc_ref = rest
        seq_rows, dk = conv
        sub = min(tm, CONV_ROW_BLOCK)

        @pl.when((pl.program_id(0) * tm) % seq_rows == 0)
        def _():
            xc_ref[0:SUBLANES, :] = jnp.zeros((SUBLANES, xc_ref.shape[1]), F32)

    h_ref[...] = _rms(x_ref[...], pre_ref[...]).astype(BF16)

    def project(r0, i, c0, width):
        rs = slice(r0, r0 + sub)
        cs = slice(c0, c0 + width)
        y = _dot(h_ref[rs, :], w_ref[:, offs[i] + c0:offs[i] + c0 + width])
        if i == 0:
            q_ref[rs, cs] = (y * q_scale).astype(BF16)
        elif i == 1:
            kv_ref[rs, cs] = y
        elif i == 2 and conv is None:
            b_ref[rs, cs] = y
        elif i == 2:
            xc_ref[SUBLANES + r0:SUBLANES + r0 + sub, cs] = y
        elif i == 3:
            z_ref[rs, cs] = y
        else:
            g = -jnp.exp(alog_ref[...]) * _softplus(y + dtb_ref[...])
            lane = lax.broadcasted_iota(jnp.int32, y.shape, 1)
            gb_ref[rs, :] = jnp.where(lane < n_heads_b, g, jax.nn.sigmoid(y))

    widths = [offs[i + 1] - offs[i] for i in range(4)] + [LANES]
    pieces = [(i, c0, min(MXU_COLS, widths[i] - c0))
              for i in (2, 0, 1, 3, 4) for c0 in range(0, widths[i], MXU_COLS)]
    for r0 in range(0, tm + (sub if conv is not None else 0), sub):
        mm = [functools.partial(project, r0, *p) for p in pieces] if r0 < tm else []
        cv = []
        if conv is not None and r0 > 0:
            cv = [functools.partial(_conv_silu_unit, xc_ref, b_ref, cw_ref, r0 - sub, sub, cb, n_heads_b, dk)
                  for cb in range(xc_ref.shape[1] // dk)]
        for k in range(max(len(mm), len(cv))):
            if k < len(mm):
                mm[k]()
            if k < len(cv):
                cv[k]()
    if conv is not None:
        tail_ref[0] = xc_ref[tm:tm + SUBLANES, :]
        xc_ref[0:SUBLANES, :] = xc_ref[tm:tm + SUBLANES, :]


def _inproj(x, pre, w, alog, dtb, *, q_scale, n_heads_b, offs, conv_w=None, seq_rows=None, dk=None):
    t, d = x.shape
    tm = _row_tile(t, largest=1024)
    assert all(o % LANES == 0 for o in offs[:5]) and offs[4] + LANES <= w.shape[1]
    row = lambda n: pl.BlockSpec((tm, n), lambda i: (i, 0))
    cdim = offs[3] - offs[2]
    outs = [(offs[1] - offs[0], BF16), (offs[2] - offs[1], F32), (cdim, F32), (offs[4] - offs[3], F32),
            (LANES, F32)]
    in_specs = [row(d), _const_spec((1, d)), _weight_spec(w.shape),
                _const_spec(alog.shape), _const_spec(dtb.shape)]
    out_specs = [row(n) for n, _ in outs]
    out_shape = [jax.ShapeDtypeStruct((t, n), dt) for n, dt in outs]
    args, scratch, conv = [x, pre, w, alog, dtb], [pltpu.VMEM((tm, d), BF16)], None
    if conv_w is not None:
        assert seq_rows % tm == 0 and conv_w.shape[0] - 1 <= SUBLANES
        in_specs.append(_const_spec(conv_w.shape))
        args.append(conv_w)
        out_specs.append(pl.BlockSpec((1, SUBLANES, cdim), lambda i: (i, 0, 0)))
        out_shape.append(jax.ShapeDtypeStruct((t // tm, SUBLANES, cdim), F32))
        scratch.append(pltpu.VMEM((tm + SUBLANES, cdim), F32))
        conv = (seq_rows, dk)
    return pl.pallas_call(
        functools.partial(_inproj_kernel, q_scale=q_scale, n_heads_b=n_heads_b, offs=offs, conv=conv),
        grid=(t // tm,),
        in_specs=in_specs,
        out_specs=out_specs,
        out_shape=out_shape,
        scratch_shapes=scratch,
        compiler_params=_params(("arbitrary",)),
        name="inproj",
    )(*args)


def _merge_tile(x_ref, oa_ref, ob_ref, pre_ref, post_ref, wg_ref, wa_ref, wb_ref, wo_ref, o_ref):
    x = x_ref[...]
    h = _rms(x, pre_ref[...]).astype(BF16)
    oa, ob = oa_ref[...], ob_ref[...]
    m = jnp.zeros(x.shape, F32)
    d = wo_ref.shape[0]
    for c0 in range(0, d, MXU_COLS):
        cs = slice(c0, c0 + MXU_COLS)
        merged = (jax.nn.sigmoid(_dot(h, wg_ref[:, cs])) * _dot(oa, wa_ref[:, cs])
                  + jax.nn.sigmoid(_dot(h, wg_ref[:, d + c0:d + c0 + MXU_COLS])) * _dot(ob, wb_ref[:, cs]))
        m = m + _dot(merged.astype(BF16), wo_ref[cs, :])
    o_ref[...] = x + _rms(m, post_ref[...])


def _merge_kernel(xa_ref, oaa_ref, oba_ref, xb_ref, oab_ref, obb_ref, *rest, tiles_a):
    *shared, oa_ref, ob_ref = rest
    i = pl.program_id(0)

    @pl.when(i < tiles_a)
    def _():
        _merge_tile(xa_ref, oaa_ref, oba_ref, *shared, oa_ref)

    @pl.when(i >= tiles_a)
    def _():
        _merge_tile(xb_ref, oab_ref, obb_ref, *shared, ob_ref)


def _merge(group_a, group_b, pre, post, wg, wa, wb, wo):
    (ta, d), tb = group_a[0].shape, group_b[0].shape[0]
    tm = min(_row_tile(ta, largest=1024), _row_tile(tb, largest=1024))
    na, nb = ta // tm, tb // tm
    rows_a = [pl.BlockSpec((tm, a.shape[1]), lambda i: (jnp.minimum(i, na - 1), 0)) for a in group_a]
    rows_b = [pl.BlockSpec((tm, a.shape[1]), lambda i: (jnp.maximum(i - na, 0), 0)) for a in group_b]
    weights = (wg, wa, wb, wo)
    return pl.pallas_call(
        functools.partial(_merge_kernel, tiles_a=na),
        grid=(na + nb,),
        in_specs=rows_a + rows_b + [_const_spec((1, d)), _const_spec((1, d))]
        + [_weight_spec(w.shape) for w in weights],
        out_specs=[rows_a[0], rows_b[0]],
        out_shape=[jax.ShapeDtypeStruct((ta, d), F32), jax.ShapeDtypeStruct((tb, d), F32)],
        compiler_params=_params(("arbitrary",)),
        name="merge",
    )(*group_a, *group_b, pre, post, *weights)


def _sink_softmax(s, mask, sink):
    s = jnp.where(mask, s, NEG_BIG)
    m = jnp.maximum(jnp.max(s, axis=-1, keepdims=True), sink)
    e = jnp.exp(s - m)
    return e / (jnp.sum(e, axis=-1, keepdims=True) + jnp.exp(sink - m))


def _swa_prompt_kernel(sink_ref, q_ref, kvp_ref, kvc_ref, o_ref, kv_last_ref, *, kv_heads, group,
                       head_dim):
    n = pl.program_id(1)
    w = WINDOW
    blocks = q_ref.shape[0] // w
    dkv = kv_heads * head_dim
    pairs = group // 2
    kv_all = jnp.concatenate([kvp_ref[...], kvc_ref[...]], axis=0)

    @pl.when(n == pl.num_programs(1) - 1)
    def _():
        kv_last_ref[0] = kvc_ref[(blocks - 1) * w:blocks * w, :]
    lo = lax.broadcasted_iota(jnp.int32, (kv_all.shape[0], dkv), 1) < head_dim

    def lane_halves(x):
        x_rot = pltpu.roll(x, head_dim, axis=1)
        pick = lambda kv, half: jnp.where(lo if half == 0 else ~lo, x if kv == half else x_rot, 0.0)
        return [[pick(kv, half).astype(BF16) for half in range(2)] for kv in range(kv_heads)]

    k_ext = lane_halves(kv_all[:, :dkv])
    v_ext = lane_halves(kv_all[:, dkv:])

    i = lax.broadcasted_iota(jnp.int32, (w, 2 * w), 0)
    j = lax.broadcasted_iota(jnp.int32, (w, 2 * w), 1)
    bias = jnp.where(j > i, jnp.where(j <= i + w, 0.0, NEG_BIG), NEG_BIG)
    bias_seq_start = jnp.where(j >= w, bias, NEG_BIG)
    lo_out = lax.broadcasted_iota(jnp.int32, (w, 2 * head_dim), 1) < head_dim

    ones_half = [jnp.where(lo if half == 0 else ~lo, 1.0, 0.0).astype(BF16) for half in range(2)]
    v_ext = [[jnp.concatenate([v_ext[kv][half], ones_half[half]], axis=1) for half in range(2)]
             for kv in range(kv_heads)]

    inst = [(t, kv) for t in range(blocks) for kv in range(kv_heads)]
    band = lambda x, t: x[t * w:(t + 2) * w]
    q2 = [jnp.concatenate([q_ref[t * w:(t + 1) * w, (kv * pairs + p) * LANES:(kv * pairs + p + 1) * LANES]
                           for p in range(pairs)], axis=0) for t, kv in inst]
    scores = [[_dot_nt(q_, band(k_ext[kv][half], t)) for half in range(2)]
              for q_, (t, kv) in zip(q2, inst)]
    bias_first = jnp.where(n > 0, bias, bias_seq_start)
    for (t, kv), sc in zip(inst, scores):
        vband = jnp.concatenate([band(v_ext[kv][0], t), band(v_ext[kv][1], t)], axis=0)
        for p in range(pairs):
            exps, sink_terms = [], []
            for half in range(2):
                sink = sink_ref[kv * group + 2 * p + half]
                s = sc[half][p * w:(p + 1) * w] + (bias_first if t == 0 else bias)
                m = jnp.maximum(jnp.max(s, axis=-1, keepdims=True), sink)
                exps.append(jnp.exp(s - m).astype(BF16))
                sink_terms.append(jnp.exp(sink - m))
            o = _dot(jnp.concatenate(exps, axis=1), vband)
            den = o[:, LANES:] + jnp.where(lo_out, sink_terms[0], sink_terms[1])
            cs = slice((kv * pairs + p) * LANES, (kv * pairs + p + 1) * LANES)
            o_ref[t * w:(t + 1) * w, cs] = (o[:, :LANES] / den).astype(o_ref.dtype)


def _swa_prompt(sinks, q, kv, *, batch, kv_heads, head_dim):
    t, dq = q.shape
    seq = t // batch
    assert seq % WINDOW == 0
    group = dq // (kv_heads * head_dim)
    assert 2 * head_dim == LANES and kv_heads == 2 and group % 2 == 0
    blocks = SWA_BLOCKS_PER_STEP if (seq // WINDOW) % SWA_BLOCKS_PER_STEP == 0 else 1
    nb = seq // (blocks * WINDOW)
    rows = blocks * WINDOW
    return pl.pallas_call(
        functools.partial(_swa_prompt_kernel, kv_heads=kv_heads, group=group, head_dim=head_dim),
        grid=(batch, nb),
        in_specs=[pl.BlockSpec(memory_space=pltpu.SMEM),
                  pl.BlockSpec((rows, dq), lambda b, n: (b * nb + n, 0)),
                  pl.BlockSpec((WINDOW, kv.shape[1]),
                               lambda b, n: ((b * nb + n) * blocks - jnp.minimum(n, 1), 0)),
                  pl.BlockSpec((rows, kv.shape[1]), lambda b, n: (b * nb + n, 0))],
        out_specs=[pl.BlockSpec((rows, dq), lambda b, n: (b * nb + n, 0)),
                   pl.BlockSpec((1, WINDOW, kv.shape[1]), lambda b, n: (b, 0, 0))],
        out_shape=[jax.ShapeDtypeStruct((t, dq), BF16),
                   jax.ShapeDtypeStruct((batch, WINDOW, kv.shape[1]), F32)],
        compiler_params=_params(("parallel", "arbitrary")),
        name="swa_prompt",
    )(sinks, q, kv, kv)


def _swa_sample_kernel(sink_ref, q_ref, kvn_ref, ck_ref, cv_ref, o_ref, nk_ref, nv_ref, *,
                       kv_heads, group, head_dim):
    bb, t, _ = q_ref.shape
    wb = ck_ref.shape[1]
    dkv = kv_heads * head_dim
    q = q_ref[...]
    kc = jnp.concatenate([ck_ref[...], kvn_ref[:, :, :dkv]], axis=1)
    vc = jnp.concatenate([cv_ref[...], kvn_ref[:, :, dkv:]], axis=1)
    nk_ref[...] = kc[:, t:, :]
    nv_ref[...] = vc[:, t:, :]
    kcb = kc.astype(BF16)
    vcb = vc.astype(BF16)
    assert t & (t - 1) == 0
    i = lax.broadcasted_iota(jnp.int32, (group * t, wb + t), 0) & (t - 1)
    j = lax.broadcasted_iota(jnp.int32, (group * t, wb + t), 1)
    diff = wb + i - j
    mask = ((diff >= 0) & (diff < WINDOW))[None]
    hs = [slice(kv * head_dim, (kv + 1) * head_dim) for kv in range(kv_heads)]
    heads = [[kv * group + g for g in range(group)] for kv in range(kv_heads)]
    qs = [jnp.concatenate([q[:, :, h * head_dim:(h + 1) * head_dim] for h in hd_], axis=1) for hd_ in heads]
    sinks = [jnp.concatenate([jnp.full((1, t, 1), sink_ref[h], F32) for h in hd_], axis=1) for hd_ in heads]
    scores = [jnp.einsum('bqd,bkd->bqk', q_, kcb[:, :, h_], preferred_element_type=F32)
              for q_, h_ in zip(qs, hs)]
    probs = [_sink_softmax(s, mask, sink).astype(BF16) for s, sink in zip(scores, sinks)]
    outs = [jnp.einsum('bqk,bkd->bqd', p, vcb[:, :, h_], preferred_element_type=F32)
            for p, h_ in zip(probs, hs)]
    o_ref[...] = jnp.concatenate([o[:, g * t:(g + 1) * t, :] for o in outs for g in range(group)],
                                 axis=2).astype(o_ref.dtype)


def _swa_sample(sinks, q, kvn, ck, cv, *, kv_heads, head_dim):
    nbatch, t, dq = q.shape
    wb = ck.shape[1]
    group = dq // (kv_heads * head_dim)
    bb = SWA_SAMPLE_GROUP if nbatch % SWA_SAMPLE_GROUP == 0 else 1
    blk = lambda a: pl.BlockSpec((bb,) + a.shape[1:], lambda i: (i, 0, 0))
    return pl.pallas_call(
        functools.partial(_swa_sample_kernel, kv_heads=kv_heads, group=group, head_dim=head_dim),
        grid=(nbatch // bb,),
        in_specs=[pl.BlockSpec(memory_space=pltpu.SMEM), blk(q), blk(kvn), blk(ck), blk(cv)],
        out_specs=[blk(q), blk(ck), blk(cv)],
        out_shape=[jax.ShapeDtypeStruct(q.shape, BF16), jax.ShapeDtypeStruct(ck.shape, F32),
                   jax.ShapeDtypeStruct(cv.shape, F32)],
        compiler_params=_params(("parallel",)),
        name="swa_sample",
    )(sinks, q, kvn, ck, cv)


def _chunk_masks(r, c):
    assert r & (r - 1) == 0 and c & (c - 1) == 0 and LANES % r == 0
    width = 2 * LANES
    ri = lax.broadcasted_iota(jnp.int32, (r, width), 0)
    cj = lax.broadcasted_iota(jnp.int32, (r, width), 1)
    ci = cj & (r - 1)
    incl, strict = ci <= ri, ci < ri
    if r != c:
        shift = c.bit_length() - 1
        same = (ri >> shift) == (ci >> shift)
        incl, strict = same & incl, same & strict
    f01 = lambda mask: jnp.where(mask, 1.0, 0.0)
    block = cj >> (r.bit_length() - 1)
    first_in_tile = lax.broadcasted_iota(jnp.int32, (r, LANES), 1) < r
    return dict(incl=incl, strict=strict, eye=f01(ri == ci), first_in_tile=first_in_tile,
                keep=[f01(block == x).astype(BF16) for x in range(width // r)],
                incl_sel=f01(incl)[:, :r].astype(BF16),
                same_sel=None if r == c else f01(same)[:, :r].astype(BF16))


def _delta_gates(gb, masks, heads, dk):
    r = gb.shape[0]
    gc_all = _dot_sel(masks["incl_sel"], gb)
    if masks["same_sel"] is None:
        gl_all = jnp.broadcast_to(gc_all[r - 1:r, :], gc_all.shape)
    else:
        gl_all = _dot_sel(masks["same_sel"], gb)
    bc = lambda a, lane: jnp.broadcast_to(a[:, lane:lane + 1], (r, dk))
    return [(bc(gc_all, h), bc(gl_all, h), bc(gb, heads + h)) for h in range(heads)]


def _split2(x):
    hi = x.astype(BF16)
    return hi, (x - hi.astype(F32)).astype(BF16)


def _rows(xs):
    return jnp.concatenate(xs, axis=0)


def _lanes(xs):
    return xs[0] if len(xs) == 1 else jnp.concatenate(xs, axis=1)


def _block_diag(blocks):
    n = len(blocks)
    z = jnp.zeros(blocks[0].shape, BF16)
    return _rows([_lanes([blocks[i] if j == i else z for j in range(n)]) for i in range(n)])


def _delta_intra(insts, masks, c):
    r, dk = insts[0][0].shape
    dv = insts[0][2].shape[1]
    m = LANES // r
    n = 2 * m
    assert dk == LANES and len(insts) % n == 0
    incl, strict, eye, keep = masks["incl"], masks["strict"], masks["eye"], masks["keep"]
    tiles = [list(range(i, i + m)) for i in range(0, len(insts), m)]
    groups = [(2 * i, 2 * i + 1) for i in range(len(tiles) // 2)]
    qs, ks, vs, gcs, gls, betas = ([i[j] for i in insts] for j in range(6))
    bf = lambda x: x.astype(BF16)

    def lane_tile(xs):
        return xs[0] if m == 1 else jnp.where(masks["first_in_tile"], xs[0], xs[1])

    def block_diag_x(x):
        return _rows([x * keep[i] for i in range(n)])

    g_cols = [_lanes([lane_tile([gcs[i] for i in tiles[t]]) for t in g]) for g in groups]
    g_rows = [jnp.transpose(_rows([gcs[i] for t in g for i in tiles[t]]))[:r, :] for g in groups]
    decays = [jnp.where(incl, jnp.exp(gc - gr), 0.0) for gc, gr in zip(g_cols, g_rows)]
    egcs = [jnp.exp(gc) for gc in gcs]
    kbs = [k * beta for k, beta in zip(ks, betas)]
    kq = [_dot_nt(_rows([_lanes([bf(kbs[i]) for i in t]), _lanes([bf(qs[i]) for i in t])]),
                  _block_diag([bf(ks[i]) for i in t])) for t in tiles]
    ms = [jnp.where(strict, _lanes([kq[t][:r] for t in g]) * decay, 0.0) for g, decay in zip(groups, decays)]
    qks = [_lanes([kq[t][r:] for t in g]) * decay for g, decay in zip(groups, decays)]
    rounds = max((c - 1).bit_length(), 1)
    ps = [-m_ for m_ in ms]
    tinvs = [None] * len(groups)
    for j in range(rounds):
        square = j + 1 < rounds
        if not square and tinvs[0] is None:
            tinvs = [eye + p for p in ps]
            break
        p_hl = [_split2(p) for p in ps]
        weights = [block_diag_x(hi) for hi, _ in p_hl]
        lhs = [_rows((list(hl) if square else []) + ([] if t is None else list(_split2(t))))
               for hl, t in zip(p_hl, tinvs)]
        res = [_dot(a, w) for a, w in zip(lhs, weights)]
        new_ps, new_ts = [], []
        for p, t, a in zip(ps, tinvs, res):
            o = 0
            if square:
                new_ps.append(a[:r] + a[r:2 * r])
                o = 2 * r
            new_ts.append(eye + p if t is None else t + (a[o:o + r] + a[o + r:o + 2 * r]))
        ps, tinvs = new_ps if square else ps, new_ts
    rhs = [jnp.concatenate([kb * egc, v * beta], axis=1).astype(BF16)
           for kb, egc, v, beta in zip(kbs, egcs, vs, betas)]
    t_hl = [_split2(t) for t in tinvs]
    sols = []
    for ti, t in enumerate(tiles):
        hi, lo = (x[:, (ti % 2) * LANES:(ti % 2 + 1) * LANES] for x in t_hl[ti // 2])
        a = _dot(_rows([hi, lo]), _block_diag([rhs[i] for i in t]))
        sols.append(a[:r] + a[r:])
    out = []
    for ti, t in enumerate(tiles):
        per_inst = []
        for slot, i in enumerate(t):
            s_ = sols[ti][:, slot * (dk + dv):(slot + 1) * (dk + dv)]
            per_inst.append((s_[:, :dk], s_[:, dk:], qs[i] * egcs[i], ks[i] * jnp.exp(gls[i] - gcs[i]),
                             jnp.exp(gls[i])))
        out.append((qks[ti // 2][:, (ti % 2) * LANES:(ti % 2 + 1) * LANES], per_inst))
    return out


def _gated_out_norm(o, w, z):
    return (_rms(o, w) * _silu(z)).astype(BF16)


def _head_cols(xb, h, heads, dk, dv):
    q = xb[:, h * dk:(h + 1) * dk]
    k = xb[:, (heads + h) * dk:(heads + h + 1) * dk]
    v = xb[:, 2 * heads * dk + h * dv:2 * heads * dk + (h + 1) * dv]
    return q, k, v


def _delta_prompt_intra_kernel(x_ref, gb_ref, kq_ref, vu_ref, kd_ref, qk_ref, gt_ref, *,
                               heads, dk, dv, chunk):
    rows = x_ref.shape[0]
    xb = x_ref[...]
    gb = gb_ref[...]
    masks = _chunk_masks(chunk, chunk)
    insts = []
    for g in range(rows // chunk):
        rs = slice(g * chunk, (g + 1) * chunk)
        gates = _delta_gates(gb[rs], masks, heads, dk)
        insts.extend(_head_cols(xb[rs], h, heads, dk, dv) + gates[h] for h in range(heads))
    intra = _delta_intra(insts, masks, chunk)

    tiles_per_chunk = len(intra) // (rows // chunk)
    for ti, (qk, per_inst) in enumerate(intra):
        g, tc = divmod(ti, tiles_per_chunk)
        qk_ref[0, g, tc] = qk.astype(BF16)
        for x, (k_cum, v_u, q_dec, k_dec, g_tot) in enumerate(per_inst):
            h = tc * len(per_inst) + x
            kq_ref[0, g, h] = _rows([k_cum, q_dec]).astype(BF16)
            vu_ref[0, g, h] = v_u
            kd_ref[0, g, h] = k_dec.astype(BF16)
            gt_ref[0, g, h] = g_tot[0:SUBLANES, :]


def _delta_prompt_state_kernel(kq_ref, vu_ref, kd_ref, qk_ref, gt_ref, z_ref, nw_ref, o_ref, s_out_ref,
                               s_ref):
    n = pl.program_id(0)
    batch, chunks, heads, c, dv = vu_ref.shape
    tiles = qk_ref.shape[2]
    m = heads // tiles

    @pl.when(n == 0)
    def _():
        s_ref[...] = jnp.zeros(s_ref.shape, F32)

    inst = [(b, h) for b in range(batch) for h in range(heads)]
    state = [s_ref[b, h] for b, h in inst]
    for j in range(chunks):
        res = [_dot(kq_ref[b, j, h], s.astype(BF16)) for (b, h), s in zip(inst, state)]
        v_new = [vu_ref[b, j, h] - r_[:c] for (b, h), r_ in zip(inst, res)]
        vn_bf = [vn.astype(BF16) for vn in v_new]
        o_intra = [_dot(qk_ref[b, j, t], _block_diag(vn_bf[(b * tiles + t) * m:(b * tiles + t + 1) * m]))
                   for b in range(batch) for t in range(tiles)]
        state = [s * gt_ref[b, j, h][0:1, :] + _dot_tn(kd_ref[b, j, h], vn)
                 for (b, h), s, vn in zip(inst, state, vn_bf)]
        for i, (b, h) in enumerate(inst):
            o = res[i][c:] + o_intra[i // m][:, (i % m) * dv:(i % m + 1) * dv]
            cs = slice(h * dv, (h + 1) * dv)
            o_ref[b, j, :, cs] = _gated_out_norm(o, nw_ref[...], z_ref[b, j, :, cs])
    for (b, h), s in zip(inst, state):
        s_ref[b, h] = s

    @pl.when(n == pl.num_programs(0) - 1)
    def _():
        s_out_ref[...] = s_ref[...]


def _delta_prompt(x, z, gb, norm_w, *, batch, heads, dk, dv):
    t, cdim = x.shape
    seq = t // batch
    c = DN_CHUNK if seq % DN_CHUNK == 0 else seq
    assert c % SUBLANES == 0 and dk == dv == LANES
    nc = seq // c
    group = PROMPT_CHUNKS_PER_STEP if nc % PROMPT_CHUNKS_PER_STEP == 0 else 1
    rows = group * c
    steps = seq // rows
    tiles = heads * c // LANES
    blk = lambda n_: pl.BlockSpec((rows, n_), lambda b, n: (b * steps + n, 0))
    per_chunk = lambda per, r_, dt: (
        pl.BlockSpec((1, group, per, r_, LANES), lambda b, n: (b, n, 0, 0, 0)),
        jax.ShapeDtypeStruct((batch, nc, per, r_, LANES), dt))
    outs = [per_chunk(heads, 2 * c, BF16), per_chunk(heads, c, F32), per_chunk(heads, c, BF16),
            per_chunk(tiles, c, BF16), per_chunk(heads, SUBLANES, F32)]
    kq, vu, kd, qk, gt = pl.pallas_call(
        functools.partial(_delta_prompt_intra_kernel, heads=heads, dk=dk, dv=dv, chunk=c),
        grid=(batch, steps),
        in_specs=[blk(cdim), blk(gb.shape[1])],
        out_specs=[o[0] for o in outs],
        out_shape=[o[1] for o in outs],
        compiler_params=_params(("parallel", "parallel")),
        name="delta_prompt_intra",
    )(x, gb)
    per_step = STATE_CHUNKS_PER_STEP if nc % STATE_CHUNKS_PER_STEP == 0 else 1
    all_seq = lambda a: pl.BlockSpec((batch, per_step) + a.shape[2:], lambda n: (0, n) + (0,) * (a.ndim - 2))
    z4 = z.reshape(batch, nc, c, heads * dv)
    o, s_out = pl.pallas_call(
        _delta_prompt_state_kernel,
        grid=(nc // per_step,),
        in_specs=[all_seq(kq), all_seq(vu), all_seq(kd), all_seq(qk), all_seq(gt), all_seq(z4),
                  _const_spec(norm_w.shape)],
        out_specs=[all_seq(z4), _const_spec((batch, heads, dk, dv))],
        out_shape=[jax.ShapeDtypeStruct(z4.shape, BF16), jax.ShapeDtypeStruct((batch, heads, dk, dv), F32)],
        scratch_shapes=[pltpu.VMEM((batch, heads, dk, dv), F32)],
        compiler_params=_params(("arbitrary",)),
        name="delta_prompt_state",
    )(kq, vu, kd, qk, gt, z4, norm_w)
    return o.reshape(t, heads * dv), s_out


def _delta_sample_kernel(x_ref, cbuf_ref, z_ref, gb_ref, s_in_ref, cw_ref, nw_ref, o_ref, s_out_ref,
                         xc_ref, *, heads, dk, dv, conv_w):
    nb, hist, cdim = cbuf_ref.shape
    r = x_ref.shape[0]
    t = r // nb
    xc_ref[:, SUBLANES:SUBLANES + t, :] = x_ref[...].reshape(nb, t, cdim)
    xc_ref[:, SUBLANES - hist:SUBLANES, :] = cbuf_ref[...]
    acc = None
    for jj in range(conv_w):
        off = SUBLANES - (conv_w - 1) + jj
        term = xc_ref[:, off:off + t, :] * cw_ref[jj:jj + 1, :][None]
        acc = term if acc is None else acc + term
    xb = _silu(acc).reshape(r, cdim)

    masks = _chunk_masks(r, t)
    gates = _delta_gates(gb_ref[...], masks, heads, dk)
    insts = []
    for h in range(heads):
        q, k, v = _head_cols(xb, h, heads, dk, dv)
        insts.append((_unit(q) * (dk ** -0.5), _unit(k), v) + gates[h])
    intra = _delta_intra(insts, masks, t)
    rows = [slice(b * t, (b + 1) * t) for b in range(nb)]
    for h in range(heads):
        qk, ((k_cum, v_u, q_dec, k_dec, g_tot),) = intra[h]
        state = [s_in_ref[b, h] for b in range(nb)]
        res = [_dot(jnp.concatenate([k_cum[rs], q_dec[rs]], axis=0).astype(BF16), s.astype(BF16))
               for rs, s in zip(rows, state)]
        v_new = [v_u[rs] - r_[:t] for rs, r_ in zip(rows, res)]
        for b in range(nb):
            s_out_ref[b, h] = (state[b] * g_tot[b * t:b * t + 1, :]
                               + _dot_tn(k_dec[rows[b]].astype(BF16), v_new[b].astype(BF16)))
        o = (jnp.concatenate([r_[t:] for r_ in res], axis=0)
             + _dot(qk.astype(BF16), jnp.concatenate(v_new, axis=0).astype(BF16)))
        o_ref[:, h * dv:(h + 1) * dv] = _gated_out_norm(o, nw_ref[...], z_ref[:, h * dv:(h + 1) * dv])


def _delta_sample(x, cbuf, z, gb, s0, conv_w, norm_w, *, heads, dk, dv):
    rows, cdim = x.shape
    nbatch, hist, _ = cbuf.shape
    t = rows // nbatch
    assert t == SUBLANES and hist == conv_w.shape[0] - 1 and hist <= t and dk == dv
    nb = SAMPLE_GROUP if nbatch % SAMPLE_GROUP == 0 else 1
    r = nb * t
    blk = lambda n_: pl.BlockSpec((r, n_), lambda i: (i, 0))
    sblk = pl.BlockSpec((nb, heads, dk, dv), lambda i: (i, 0, 0, 0))
    return pl.pallas_call(
        functools.partial(_delta_sample_kernel, heads=heads, dk=dk, dv=dv, conv_w=conv_w.shape[0]),
        grid=(nbatch // nb,),
        in_specs=[blk(cdim), pl.BlockSpec((nb, hist, cdim), lambda i: (i, 0, 0)), blk(z.shape[1]),
                  blk(gb.shape[1]), sblk, _const_spec(conv_w.shape), _const_spec(norm_w.shape)],
        out_specs=[blk(heads * dv), sblk],
        out_shape=[jax.ShapeDtypeStruct((rows, heads * dv), BF16), jax.ShapeDtypeStruct(s0.shape, F32)],
        scratch_shapes=[pltpu.VMEM((nb, 2 * SUBLANES, cdim), F32)],
        compiler_params=_params(("parallel",)),
        name="delta_sample",
    )(x, cbuf, z, gb, s0, conv_w, norm_w)


def _layer(xp, xs, ck, cv, cconv, sdelta, lw, dims):
    (f1_pre, f1_post, f1_up, f1_down, mix_pre, mix_post, w_in, sinks, conv_w, a_log, dt_bias,
     out_norm, w_a, w_b, w_o, f2_pre, f2_post, f2_up, f2_down) = lw
    batch, seq, d = xp.shape
    dbatch, dseq, _ = xs.shape
    kvh, hd = dims["kv_heads"], dims["head_dim"]
    hb, dk, dv = dims["b_heads"], dims["dk"], dims["dv"]
    a_heads = w_a.shape[0] // hd
    cdim = conv_w.shape[1]
    d_ff = f1_down.shape[0]
    bf = lambda a: a.astype(BF16)
    row = lambda a: a.reshape(1, -1).astype(F32)

    sizes = (a_heads * hd, kvh * hd, kvh * hd, cdim, hb * dv, hb, hb, d, d)
    offs = [0]
    for s_ in sizes:
        offs.append(offs[-1] + s_)
    w_proj = bf(w_in[:, :offs[5] + LANES])
    w_gates = bf(w_in[:, offs[7]:offs[9]])
    proj_offs = (offs[0], offs[1], offs[3], offs[4], offs[5])
    alog = jnp.pad(row(a_log), ((0, 0), (0, LANES - hb)))
    dtb = jnp.pad(row(dt_bias), ((0, 0), (0, LANES - hb)))

    def ffn(xa, xb, pre, post, up, down):
        return _ffn(xa, xb, row(pre), row(post), bf(up), bf(down))

    def inproj(x, **conv):
        return _inproj(x, row(mix_pre), w_proj, alog, dtb, q_scale=hd ** -0.5, n_heads_b=hb,
                       offs=proj_offs, **conv)

    def merge(group_a, group_b):
        return _merge(group_a, group_b, row(mix_pre), row(mix_post), w_gates, bf(w_a), bf(w_b), bf(w_o))

    sinks = sinks.astype(F32)
    norm_w = row(out_norm)
    conv_w = conv_w.astype(F32)
    hist = conv_w.shape[0] - 1

    x, x_s = ffn(xp.reshape(batch * seq, d), xs.reshape(dbatch * dseq, d), f1_pre, f1_post, f1_up, f1_down)
    q, kv, xb, z, gb, tail = inproj(x, conv_w=conv_w, seq_rows=seq, dk=dk)
    oa, kv_last = _swa_prompt(sinks, q, kv, batch=batch, kv_heads=kvh, head_dim=hd)
    ob, p_state = _delta_prompt(xb, z, gb, norm_w, batch=batch, heads=hb, dk=dk, dv=dv)
    prompt_mix = (x, oa, ob)
    kv_last = kv_last.reshape(batch, WINDOW, 2, kvh, hd)
    p_k, p_v = kv_last[:, :, 0], kv_last[:, :, 1]
    p_conv = tail.reshape(batch, -1, SUBLANES, cdim)[:, -1, SUBLANES - hist:]

    x = x_s
    q, kv, xb, z, gb = inproj(x)
    wb = ck.shape[1]
    oa, s_k, s_v = _swa_sample(sinks, q.reshape(dbatch, dseq, -1), kv.reshape(dbatch, dseq, -1),
                               ck.reshape(dbatch, wb, kvh * hd), cv.reshape(dbatch, wb, kvh * hd),
                               kv_heads=kvh, head_dim=hd)
    ob, s_state = _delta_sample(xb, cconv, z, gb, sdelta, conv_w, norm_w, heads=hb, dk=dk, dv=dv)
    x_p, x_s = merge(prompt_mix, (x, oa.reshape(dbatch * dseq, -1), ob))
    yp, ys = ffn(x_p, x_s, f2_pre, f2_post, f2_up, f2_down)
    yp, ys = yp.reshape(batch, seq, d), ys.reshape(dbatch, dseq, d)
    s_k = s_k.reshape(dbatch, wb, kvh, hd)
    s_v = s_v.reshape(dbatch, wb, kvh, hd)
    s_conv = xb.reshape(dbatch, dseq, cdim)[:, dseq - hist:]
    return yp, ys, (p_k, p_v, p_conv, p_state), (s_k, s_v, s_conv, s_state)


def kernel(x_prompt, x_sample, cache_swa_k, cache_swa_v, state_conv, state_delta, ffn1_norm_pre, ffn1_norm_post, ffn1_w_up, ffn1_w_down, mix_norm_pre, mix_norm_post, w_in, attn_sinks, conv_w, dn_a_log, dn_dt_bias, dn_out_norm, w_branch_a, w_branch_b, w_out, ffn2_norm_pre, ffn2_norm_post, ffn2_w_up, ffn2_w_down):
    weights = (ffn1_norm_pre, ffn1_norm_post, ffn1_w_up, ffn1_w_down, mix_norm_pre, mix_norm_post,
               w_in, attn_sinks, conv_w, dn_a_log, dn_dt_bias, dn_out_norm, w_branch_a, w_branch_b,
               w_out, ffn2_norm_pre, ffn2_norm_post, ffn2_w_up, ffn2_w_down)
    depth = w_in.shape[0]
    dims = dict(kv_heads=cache_swa_k.shape[3], head_dim=cache_swa_k.shape[4],
                b_heads=state_delta.shape[2], dk=state_delta.shape[3], dv=state_delta.shape[4])
    yp, ys = x_prompt, x_sample
    p_out, s_out = [], []
    for l in range(depth):
        yp, ys, p_new, s_new = _layer(yp, ys, cache_swa_k[l], cache_swa_v[l], state_conv[l],
                                      state_delta[l], tuple(w[l] for w in weights), dims)
        p_out.append(p_new)
        s_out.append(s_new)
    stack = lambda outs, i: jnp.stack([o[i] for o in outs])
    return (yp, ys,
            stack(p_out, 0), stack(p_out, 1), stack(p_out, 2), stack(p_out, 3),
            stack(s_out, 0), stack(s_out, 1), stack(s_out, 2), stack(s_out, 3))
```

```python
import functools

import jax
import jax.numpy as jnp
from jax import lax
from jax.experimental import pallas as pl
from jax.experimental.pallas import tpu as pltpu

WINDOW = 128
DN_CHUNK = 64
EPS = 1e-6
F32 = jnp.float32
BF16 = jnp.bfloat16
NEG_BIG = -1e30
LANES = 128
SUBLANES = 8
VMEM_LIMIT_BYTES = 56 * 1024 * 1024
FF_CHUNK = 256
MXU_COLS = 256
CONV_ROW_BLOCK = 128
SAMPLE_GROUP = 16
SWA_SAMPLE_GROUP = 32
SWA_BLOCKS_PER_STEP = 4
STATE_CHUNKS_PER_STEP = 4
PROMPT_CHUNKS_PER_STEP = 16


def _rms(x, w):
    return x * lax.rsqrt(jnp.mean(x * x, axis=-1, keepdims=True) + EPS) * w


def _silu(x):
    return x * jax.nn.sigmoid(x)


def _unit(x):
    return x * lax.rsqrt(jnp.sum(x * x, axis=-1, keepdims=True) + EPS)


def _softplus(x):
    return jnp.maximum(x, 0.0) + jnp.log1p(jnp.exp(-jnp.abs(x)))


def _dot(a, b):
    return jnp.dot(a, b, preferred_element_type=F32)


def _dot_nt(a, b):
    return lax.dot_general(a, b, (((1,), (1,)), ((), ())), preferred_element_type=F32)


def _dot_tn(a, b):
    return lax.dot_general(a, b, (((0,), (0,)), ((), ())), preferred_element_type=F32)


def _split3(b):
    hi = b.astype(BF16)
    r = b - hi.astype(F32)
    mid = r.astype(BF16)
    lo = (r - mid.astype(F32)).astype(BF16)
    return hi, mid, lo


def _dot_sel(sel_bf16, b):
    hi, mid, lo = _split3(b)
    return _dot(sel_bf16, hi) + _dot(sel_bf16, mid) + _dot(sel_bf16, lo)


def _ffn_tile(x_ref, pre_ref, post_ref, wup_ref, wd_ref, o_ref):
    x = x_ref[...]
    xn = _rms(x, pre_ref[...]).astype(BF16)
    d_ff = wd_ref.shape[0]
    acc = jnp.zeros(x.shape, F32)
    for c in range(d_ff // FF_CHUNK):
        sl = slice(c * FF_CHUNK, (c + 1) * FF_CHUNK)
        g = _dot(xn, wup_ref[:, sl])
        u = _dot(xn, wup_ref[:, d_ff + c * FF_CHUNK:d_ff + (c + 1) * FF_CHUNK])
        acc = acc + _dot((_silu(g) * u).astype(BF16), wd_ref[sl, :])
    o_ref[...] = x + 0.5 * _rms(acc, post_ref[...])


def _ffn_kernel(xa_ref, xb_ref, pre_ref, post_ref, wup_ref, wd_ref, oa_ref, ob_ref, *, tiles_a):
    i = pl.program_id(0)

    @pl.when(i < tiles_a)
    def _():
        _ffn_tile(xa_ref, pre_ref, post_ref, wup_ref, wd_ref, oa_ref)

    @pl.when(i >= tiles_a)
    def _():
        _ffn_tile(xb_ref, pre_ref, post_ref, wup_ref, wd_ref, ob_ref)


def _const_spec(shape):
    nd = len(shape)
    return pl.BlockSpec(shape, lambda *_: (0,) * nd)


def _weight_spec(shape):
    nd = len(shape)
    return pl.BlockSpec(shape, lambda *_: (0,) * nd, pipeline_mode=pl.Buffered(1))


def _row_tile(t, largest=512):
    for tm in (1024, 512, 256, 128, 64, 32, 16, 8):
        if tm <= largest and t % tm == 0:
            return tm
    raise ValueError(f"token count {t} is not a multiple of 8")


def _params(sem):
    return pltpu.CompilerParams(dimension_semantics=sem, vmem_limit_bytes=VMEM_LIMIT_BYTES)


def _ffn(xa, xb, pre, post, wup, wd):
    (ta, d), tb = xa.shape, xb.shape[0]
    f = wd.shape[0]
    assert f % FF_CHUNK == 0 and wup.shape == (d, 2 * f)
    tm = min(_row_tile(ta), _row_tile(tb))
    na, nb = ta // tm, tb // tm
    row_a = pl.BlockSpec((tm, d), lambda i: (jnp.minimum(i, na - 1), 0))
    row_b = pl.BlockSpec((tm, d), lambda i: (jnp.maximum(i - na, 0), 0))
    return pl.pallas_call(
        functools.partial(_ffn_kernel, tiles_a=na),
        grid=(na + nb,),
        in_specs=[row_a, row_b, _const_spec((1, d)), _const_spec((1, d)),
                  _weight_spec((d, 2 * f)), _weight_spec((f, d))],
        out_specs=[row_a, row_b],
        out_shape=[jax.ShapeDtypeStruct((ta, d), F32), jax.ShapeDtypeStruct((tb, d), F32)],
        compiler_params=_params(("arbitrary",)),
        name="ffn",
    )(xa, xb, pre, post, wup, wd)


def _conv_silu_unit(xc_ref, cv_ref, cw_ref, r0, rows, cb, heads, dk):
    taps = cw_ref.shape[0]
    rs = slice(r0, r0 + rows)
    cs = slice(cb * dk, (cb + 1) * dk)
    acc = None
    for jj in range(taps):
        off = r0 + SUBLANES - (taps - 1) + jj
        term = xc_ref[off:off + rows, cs] * cw_ref[jj:jj + 1, cs]
        acc = term if acc is None else acc + term
    y = _silu(acc)
    if cb < 2 * heads:
        y = _unit(y) * (dk ** -0.5) if cb < heads else _unit(y)
    cv_ref[rs, cs] = y


def _inproj_kernel(x_ref, pre_ref, w_ref, alog_ref, dtb_ref, *rest, q_scale, n_heads_b, offs, conv):
    tm = x_ref.shape[0]
    if conv is None:
        q_ref, kv_ref, b_ref, z_ref, gb_ref, h_ref = rest
        sub = tm
    else:
        cw_ref, q_ref, kv_ref, b_ref, z_ref, gb_ref, tail_ref, h_ref, xc_ref = rest
        seq_rows, dk = conv
        sub = min(tm, CONV_ROW_BLOCK)

        @pl.when((pl.program_id(0) * tm) % seq_rows == 0)
        def _():
            xc_ref[0:SUBLANES, :] = jnp.zeros((SUBLANES, xc_ref.shape[1]), F32)

    h_ref[...] = _rms(x_ref[...], pre_ref[...]).astype(BF16)

    def project(r0, i, c0, width):
        rs = slice(r0, r0 + sub)
        cs = slice(c0, c0 + width)
        y = _dot(h_ref[rs, :], w_ref[:, offs[i] + c0:offs[i] + c0 + width])
        if i == 0:
            q_ref[rs, cs] = (y * q_scale).astype(BF16)
        elif i == 1:
            kv_ref[rs, cs] = y
        elif i == 2 and conv is None:
            b_ref[rs, cs] = y
        elif i == 2:
            xc_ref[SUBLANES + r0:SUBLANES + r0 + sub, cs] = y
        elif i == 3:
            z_ref[rs, cs] = y
        else:
            g = -jnp.exp(alog_ref[...]) * _softplus(y + dtb_ref[...])
            lane = lax.broadcasted_iota(jnp.int32, y.shape, 1)
            gb_ref[rs, :] = jnp.where(lane < n_heads_b, g, jax.nn.sigmoid(y))

    widths = [offs[i + 1] - offs[i] for i in range(4)] + [LANES]
    pieces = [(i, c0, min(MXU_COLS, widths[i] - c0))
              for i in (2, 0, 1, 3, 4) for c0 in range(0, widths[i], MXU_COLS)]
    for r0 in range(0, tm + (sub if conv is not None else 0), sub):
        mm = [functools.partial(project, r0, *p) for p in pieces] if r0 < tm else []
        cv = []
        if conv is not None and r0 > 0:
            cv = [functools.partial(_conv_silu_unit, xc_ref, b_ref, cw_ref, r0 - sub, sub, cb, n_heads_b, dk)
                  for cb in range(xc_ref.shape[1] // dk)]
        for k in range(max(len(mm), len(cv))):
            if k < len(mm):
                mm[k]()
            if k < len(cv):
                cv[k]()
    if conv is not None:
        tail_ref[0] = xc_ref[tm:tm + SUBLANES, :]
        xc_ref[0:SUBLANES, :] = xc_ref[tm:tm + SUBLANES, :]


def _inproj(x, pre, w, alog, dtb, *, q_scale, n_heads_b, offs, conv_w=None, seq_rows=None, dk=None):
    t, d = x.shape
    tm = _row_tile(t, largest=1024)
    assert all(o % LANES == 0 for o in offs[:5]) and offs[4] + LANES <= w.shape[1]
    row = lambda n: pl.BlockSpec((tm, n), lambda i: (i, 0))
    cdim = offs[3] - offs[2]
    outs = [(offs[1] - offs[0], BF16), (offs[2] - offs[1], F32), (cdim, F32), (offs[4] - offs[3], F32),
            (LANES, F32)]
    in_specs = [row(d), _const_spec((1, d)), _weight_spec(w.shape),
                _const_spec(alog.shape), _const_spec(dtb.shape)]
    out_specs = [row(n) for n, _ in outs]
    out_shape = [jax.ShapeDtypeStruct((t, n), dt) for n, dt in outs]
    args, scratch, conv = [x, pre, w, alog, dtb], [pltpu.VMEM((tm, d), BF16)], None
    if conv_w is not None:
        assert seq_rows % tm == 0 and conv_w.shape[0] - 1 <= SUBLANES
        in_specs.append(_const_spec(conv_w.shape))
        args.append(conv_w)
        out_specs.append(pl.BlockSpec((1, SUBLANES, cdim), lambda i: (i, 0, 0)))
        out_shape.append(jax.ShapeDtypeStruct((t // tm, SUBLANES, cdim), F32))
        scratch.append(pltpu.VMEM((tm + SUBLANES, cdim), F32))
        conv = (seq_rows, dk)
    return pl.pallas_call(
        functools.partial(_inproj_kernel, q_scale=q_scale, n_heads_b=n_heads_b, offs=offs, conv=conv),
        grid=(t // tm,),
        in_specs=in_specs,
        out_specs=out_specs,
        out_shape=out_shape,
        scratch_shapes=scratch,
        compiler_params=_params(("arbitrary",)),
        name="inproj",
    )(*args)


def _merge_tile(x_ref, oa_ref, ob_ref, pre_ref, post_ref, wg_ref, wa_ref, wb_ref, wo_ref, o_ref):
    x = x_ref[...]
    h = _rms(x, pre_ref[...]).astype(BF16)
    oa, ob = oa_ref[...], ob_ref[...]
    m = jnp.zeros(x.shape, F32)
    d = wo_ref.shape[0]
    for c0 in range(0, d, MXU_COLS):
        cs = slice(c0, c0 + MXU_COLS)
        merged = (jax.nn.sigmoid(_dot(h, wg_ref[:, cs])) * _dot(oa, wa_ref[:, cs])
                  + jax.nn.sigmoid(_dot(h, wg_ref[:, d + c0:d + c0 + MXU_COLS])) * _dot(ob, wb_ref[:, cs]))
        m = m + _dot(merged.astype(BF16), wo_ref[cs, :])
    o_ref[...] = x + _rms(m, post_ref[...])


def _merge_kernel(xa_ref, oaa_ref, oba_ref, xb_ref, oab_ref, obb_ref, *rest, tiles_a):
    *shared, oa_ref, ob_ref = rest
    i = pl.program_id(0)

    @pl.when(i < tiles_a)
    def _():
        _merge_tile(xa_ref, oaa_ref, oba_ref, *shared, oa_ref)

    @pl.when(i >= tiles_a)
    def _():
        _merge_tile(xb_ref, oab_ref, obb_ref, *shared, ob_ref)


def _merge(group_a, group_b, pre, post, wg, wa, wb, wo):
    (ta, d), tb = group_a[0].shape, group_b[0].shape[0]
    tm = min(_row_tile(ta, largest=1024), _row_tile(tb, largest=1024))
    na, nb = ta // tm, tb // tm
    rows_a = [pl.BlockSpec((tm, a.shape[1]), lambda i: (jnp.minimum(i, na - 1), 0)) for a in group_a]
    rows_b = [pl.BlockSpec((tm, a.shape[1]), lambda i: (jnp.maximum(i - na, 0), 0)) for a in group_b]
    weights = (wg, wa, wb, wo)
    return pl.pallas_call(
        functools.partial(_merge_kernel, tiles_a=na),
        grid=(na + nb,),
        in_specs=rows_a + rows_b + [_const_spec((1, d)), _const_spec((1, d))]
        + [_weight_spec(w.shape) for w in weights],
        out_specs=[rows_a[0], rows_b[0]],
        out_shape=[jax.ShapeDtypeStruct((ta, d), F32), jax.ShapeDtypeStruct((tb, d), F32)],
        compiler_params=_params(("arbitrary",)),
        name="merge",
    )(*group_a, *group_b, pre, post, *weights)


def _sink_softmax(s, mask, sink):
    s = jnp.where(mask, s, NEG_BIG)
    m = jnp.maximum(jnp.max(s, axis=-1, keepdims=True), sink)
    e = jnp.exp(s - m)
    return e / (jnp.sum(e, axis=-1, keepdims=True) + jnp.exp(sink - m))


def _swa_prompt_kernel(sink_ref, q_ref, kvp_ref, kvc_ref, o_ref, kv_last_ref, *, kv_heads, group,
                       head_dim):
    n = pl.program_id(1)
    w = WINDOW
    blocks = q_ref.shape[0] // w
    dkv = kv_heads * head_dim
    pairs = group // 2
    kv_all = jnp.concatenate([kvp_ref[...], kvc_ref[...]], axis=0)

    @pl.when(n == pl.num_programs(1) - 1)
    def _():
        kv_last_ref[0] = kvc_ref[(blocks - 1) * w:blocks * w, :]
    lo = lax.broadcasted_iota(jnp.int32, (kv_all.shape[0], dkv), 1) < head_dim

    def lane_halves(x):
        x_rot = pltpu.roll(x, head_dim, axis=1)
        pick = lambda kv, half: jnp.where(lo if half == 0 else ~lo, x if kv == half else x_rot, 0.0)
        return [[pick(kv, half).astype(BF16) for half in range(2)] for kv in range(kv_heads)]

    k_ext = lane_halves(kv_all[:, :dkv])
    v_ext = lane_halves(kv_all[:, dkv:])

    i = lax.broadcasted_iota(jnp.int32, (w, 2 * w), 0)
    j = lax.broadcasted_iota(jnp.int32, (w, 2 * w), 1)
    bias = jnp.where(j > i, jnp.where(j <= i + w, 0.0, NEG_BIG), NEG_BIG)
    bias_seq_start = jnp.where(j >= w, bias, NEG_BIG)
    lo_out = lax.broadcasted_iota(jnp.int32, (w, 2 * head_dim), 1) < head_dim

    ones_half = [jnp.where(lo if half == 0 else ~lo, 1.0, 0.0).astype(BF16) for half in range(2)]
    v_ext = [[jnp.concatenate([v_ext[kv][half], ones_half[half]], axis=1) for half in range(2)]
             for kv in range(kv_heads)]

    inst = [(t, kv) for t in range(blocks) for kv in range(kv_heads)]
    band = lambda x, t: x[t * w:(t + 2) * w]
    q2 = [jnp.concatenate([q_ref[t * w:(t + 1) * w, (kv * pairs + p) * LANES:(kv * pairs + p + 1) * LANES]
                           for p in range(pairs)], axis=0) for t, kv in inst]
    scores = [[_dot_nt(q_, band(k_ext[kv][half], t)) for half in range(2)]
              for q_, (t, kv) in zip(q2, inst)]
    bias_first = jnp.where(n > 0, bias, bias_seq_start)
    for (t, kv), sc in zip(inst, scores):
        vband = jnp.concatenate([band(v_ext[kv][0], t), band(v_ext[kv][1], t)], axis=0)
        for p in range(pairs):
            exps, sink_terms = [], []
            for half in range(2):
                sink = sink_ref[kv * group + 2 * p + half]
                s = sc[half][p * w:(p + 1) * w] + (bias_first if t == 0 else bias)
                m = jnp.maximum(jnp.max(s, axis=-1, keepdims=True), sink)
                exps.append(jnp.exp(s - m).astype(BF16))
                sink_terms.append(jnp.exp(sink - m))
            o = _dot(jnp.concatenate(exps, axis=1), vband)
            den = o[:, LANES:] + jnp.where(lo_out, sink_terms[0], sink_terms[1])
            cs = slice((kv * pairs + p) * LANES, (kv * pairs + p + 1) * LANES)
            o_ref[t * w:(t + 1) * w, cs] = (o[:, :LANES] / den).astype(o_ref.dtype)


def _swa_prompt(sinks, q, kv, *, batch, kv_heads, head_dim):
    t, dq = q.shape
    seq = t // batch
    assert seq % WINDOW == 0
    group = dq // (kv_heads * head_dim)
    assert 2 * head_dim == LANES and kv_heads == 2 and group % 2 == 0
    blocks = SWA_BLOCKS_PER_STEP if (seq // WINDOW) % SWA_BLOCKS_PER_STEP == 0 else 1
    nb = seq // (blocks * WINDOW)
    rows = blocks * WINDOW
    return pl.pallas_call(
        functools.partial(_swa_prompt_kernel, kv_heads=kv_heads, group=group, head_dim=head_dim),
        grid=(batch, nb),
        in_specs=[pl.BlockSpec(memory_space=pltpu.SMEM),
                  pl.BlockSpec((rows, dq), lambda b, n: (b * nb + n, 0)),
                  pl.BlockSpec((WINDOW, kv.shape[1]),
                               lambda b, n: ((b * nb + n) * blocks - jnp.minimum(n, 1), 0)),
                  pl.BlockSpec((rows, kv.shape[1]), lambda b, n: (b * nb + n, 0))],
        out_specs=[pl.BlockSpec((rows, dq), lambda b, n: (b * nb + n, 0)),
                   pl.BlockSpec((1, WINDOW, kv.shape[1]), lambda b, n: (b, 0, 0))],
        out_shape=[jax.ShapeDtypeStruct((t, dq), BF16),
                   jax.ShapeDtypeStruct((batch, WINDOW, kv.shape[1]), F32)],
        compiler_params=_params(("parallel", "arbitrary")),
        name="swa_prompt",
    )(sinks, q, kv, kv)


def _swa_sample_kernel(sink_ref, q_ref, kvn_ref, ck_ref, cv_ref, o_ref, nk_ref, nv_ref, *,
                       kv_heads, group, head_dim):
    bb, t, _ = q_ref.shape
    wb = ck_ref.shape[1]
    dkv = kv_heads * head_dim
    q = q_ref[...]
    kc = jnp.concatenate([ck_ref[...], kvn_ref[:, :, :dkv]], axis=1)
    vc = jnp.concatenate([cv_ref[...], kvn_ref[:, :, dkv:]], axis=1)
    nk_ref[...] = kc[:, t:, :]
    nv_ref[...] = vc[:, t:, :]
    kcb = kc.astype(BF16)
    vcb = vc.astype(BF16)
    assert t & (t - 1) == 0
    i = lax.broadcasted_iota(jnp.int32, (group * t, wb + t), 0) & (t - 1)
    j = lax.broadcasted_iota(jnp.int32, (group * t, wb + t), 1)
    diff = wb + i - j
    mask = ((diff >= 0) & (diff < WINDOW))[None]
    hs = [slice(kv * head_dim, (kv + 1) * head_dim) for kv in range(kv_heads)]
    heads = [[kv * group + g for g in range(group)] for kv in range(kv_heads)]
    qs = [jnp.concatenate([q[:, :, h * head_dim:(h + 1) * head_dim] for h in hd_], axis=1) for hd_ in heads]
    sinks = [jnp.concatenate([jnp.full((1, t, 1), sink_ref[h], F32) for h in hd_], axis=1) for hd_ in heads]
    scores = [jnp.einsum('bqd,bkd->bqk', q_, kcb[:, :, h_], preferred_element_type=F32)
              for q_, h_ in zip(qs, hs)]
    probs = [_sink_softmax(s, mask, sink).astype(BF16) for s, sink in zip(scores, sinks)]
    outs = [jnp.einsum('bqk,bkd->bqd', p, vcb[:, :, h_], preferred_element_type=F32)
            for p, h_ in zip(probs, hs)]
    o_ref[...] = jnp.concatenate([o[:, g * t:(g + 1) * t, :] for o in outs for g in range(group)],
                                 axis=2).astype(o_ref.dtype)


def _swa_sample(sinks, q, kvn, ck, cv, *, kv_heads, head_dim):
    nbatch, t, dq = q.shape
    wb = ck.shape[1]
    group = dq // (kv_heads * head_dim)
    bb = SWA_SAMPLE_GROUP if nbatch % SWA_SAMPLE_GROUP == 0 else 1
    blk = lambda a: pl.BlockSpec((bb,) + a.shape[1:], lambda i: (i, 0, 0))
    return pl.pallas_call(
        functools.partial(_swa_sample_kernel, kv_heads=kv_heads, group=group, head_dim=head_dim),
        grid=(nbatch // bb,),
        in_specs=[pl.BlockSpec(memory_space=pltpu.SMEM), blk(q), blk(kvn), blk(ck), blk(cv)],
        out_specs=[blk(q), blk(ck), blk(cv)],
        out_shape=[jax.ShapeDtypeStruct(q.shape, BF16), jax.ShapeDtypeStruct(ck.shape, F32),
                   jax.ShapeDtypeStruct(cv.shape, F32)],
        compiler_params=_params(("parallel",)),
        name="swa_sample",
    )(sinks, q, kvn, ck, cv)


def _chunk_masks(r, c):
    assert r & (r - 1) == 0 and c & (c - 1) == 0 and LANES % r == 0
    width = 2 * LANES
    ri = lax.broadcasted_iota(jnp.int32, (r, width), 0)
    cj = lax.broadcasted_iota(jnp.int32, (r, width), 1)
    ci = cj & (r - 1)
    incl, strict = ci <= ri, ci < ri
    if r != c:
        shift = c.bit_length() - 1
        same = (ri >> shift) == (ci >> shift)
        incl, strict = same & incl, same & strict
    f01 = lambda mask: jnp.where(mask, 1.0, 0.0)
    block = cj >> (r.bit_length() - 1)
    first_in_tile = lax.broadcasted_iota(jnp.int32, (r, LANES), 1) < r
    return dict(incl=incl, strict=strict, eye=f01(ri == ci), first_in_tile=first_in_tile,
                keep=[f01(block == x).astype(BF16) for x in range(width // r)],
                incl_sel=f01(incl)[:, :r].astype(BF16),
                same_sel=None if r == c else f01(same)[:, :r].astype(BF16))


def _delta_gates(gb, masks, heads, dk):
    r = gb.shape[0]
    gc_all = _dot_sel(masks["incl_sel"], gb)
    if masks["same_sel"] is None:
        gl_all = jnp.broadcast_to(gc_all[r - 1:r, :], gc_all.shape)
    else:
        gl_all = _dot_sel(masks["same_sel"], gb)
    bc = lambda a, lane: jnp.broadcast_to(a[:, lane:lane + 1], (r, dk))
    return [(bc(gc_all, h), bc(gl_all, h), bc(gb, heads + h)) for h in range(heads)]


def _split2(x):
    hi = x.astype(BF16)
    return hi, (x - hi.astype(F32)).astype(BF16)


def _rows(xs):
    return jnp.concatenate(xs, axis=0)


def _lanes(xs):
    return xs[0] if len(xs) == 1 else jnp.concatenate(xs, axis=1)


def _block_diag(blocks):
    n = len(blocks)
    z = jnp.zeros(blocks[0].shape, BF16)
    return _rows([_lanes([blocks[i] if j == i else z for j in range(n)]) for i in range(n)])


def _delta_intra(insts, masks, c):
    r, dk = insts[0][0].shape
    dv = insts[0][2].shape[1]
    m = LANES // r
    n = 2 * m
    assert dk == LANES and len(insts) % n == 0
    incl, strict, eye, keep = masks["incl"], masks["strict"], masks["eye"], masks["keep"]
    tiles = [list(range(i, i + m)) for i in range(0, len(insts), m)]
    groups = [(2 * i, 2 * i + 1) for i in range(len(tiles) // 2)]
    qs, ks, vs, gcs, gls, betas = ([i[j] for i in insts] for j in range(6))
    bf = lambda x: x.astype(BF16)

    def lane_tile(xs):
        return xs[0] if m == 1 else jnp.where(masks["first_in_tile"], xs[0], xs[1])

    def block_diag_x(x):
        return _rows([x * keep[i] for i in range(n)])

    g_cols = [_lanes([lane_tile([gcs[i] for i in tiles[t]]) for t in g]) for g in groups]
    g_rows = [jnp.transpose(_rows([gcs[i] for t in g for i in tiles[t]]))[:r, :] for g in groups]
    decays = [jnp.where(incl, jnp.exp(gc - gr), 0.0) for gc, gr in zip(g_cols, g_rows)]
    egcs = [jnp.exp(gc) for gc in gcs]
    kbs = [k * beta for k, beta in zip(ks, betas)]
    kq = [_dot_nt(_rows([_lanes([bf(kbs[i]) for i in t]), _lanes([bf(qs[i]) for i in t])]),
                  _block_diag([bf(ks[i]) for i in t])) for t in tiles]
    ms = [jnp.where(strict, _lanes([kq[t][:r] for t in g]) * decay, 0.0) for g, decay in zip(groups, decays)]
    qks = [_lanes([kq[t][r:] for t in g]) * decay for g, decay in zip(groups, decays)]
    rounds = max((c - 1).bit_length(), 1)
    ps = [-m_ for m_ in ms]
    tinvs = [None] * len(groups)
    for j in range(rounds):
        square = j + 1 < rounds
        if not square and tinvs[0] is None:
            tinvs = [eye + p for p in ps]
            break
        p_hl = [_split2(p) for p in ps]
        weights = [block_diag_x(hi) for hi, _ in p_hl]
        lhs = [_rows((list(hl) if square else []) + ([] if t is None else list(_split2(t))))
               for hl, t in zip(p_hl, tinvs)]
        res = [_dot(a, w) for a, w in zip(lhs, weights)]
        new_ps, new_ts = [], []
        for p, t, a in zip(ps, tinvs, res):
            o = 0
            if square:
                new_ps.append(a[:r] + a[r:2 * r])
                o = 2 * r
            new_ts.append(eye + p if t is None else t + (a[o:o + r] + a[o + r:o + 2 * r]))
        ps, tinvs = new_ps if square else ps, new_ts
    rhs = [jnp.concatenate([kb * egc, v * beta], axis=1).astype(BF16)
           for kb, egc, v, beta in zip(kbs, egcs, vs, betas)]
    t_hl = [_split2(t) for t in tinvs]
    sols = []
    for ti, t in enumerate(tiles):
        hi, lo = (x[:, (ti % 2) * LANES:(ti % 2 + 1) * LANES] for x in t_hl[ti // 2])
        a = _dot(_rows([hi, lo]), _block_diag([rhs[i] for i in t]))
        sols.append(a[:r] + a[r:])
    out = []
    for ti, t in enumerate(tiles):
        per_inst = []
        for slot, i in enumerate(t):
            s_ = sols[ti][:, slot * (dk + dv):(slot + 1) * (dk + dv)]
            per_inst.append((s_[:, :dk], s_[:, dk:], qs[i] * egcs[i], ks[i] * jnp.exp(gls[i] - gcs[i]),
                             jnp.exp(gls[i])))
        out.append((qks[ti // 2][:, (ti % 2) * LANES:(ti % 2 + 1) * LANES], per_inst))
    return out


def _gated_out_norm(o, w, z):
    return (_rms(o, w) * _silu(z)).astype(BF16)


def _head_cols(xb, h, heads, dk, dv):
    q = xb[:, h * dk:(h + 1) * dk]
    k = xb[:, (heads + h) * dk:(heads + h + 1) * dk]
    v = xb[:, 2 * heads * dk + h * dv:2 * heads * dk + (h + 1) * dv]
    return q, k, v


def _delta_prompt_intra_kernel(x_ref, gb_ref, kq_ref, vu_ref, kd_ref, qk_ref, gt_ref, *,
                               heads, dk, dv, chunk):
    rows = x_ref.shape[0]
    xb = x_ref[...]
    gb = gb_ref[...]
    masks = _chunk_masks(chunk, chunk)
    insts = []
    for g in range(rows // chunk):
        rs = slice(g * chunk, (g + 1) * chunk)
        gates = _delta_gates(gb[rs], masks, heads, dk)
        insts.extend(_head_cols(xb[rs], h, heads, dk, dv) + gates[h] for h in range(heads))
    intra = _delta_intra(insts, masks, chunk)

    tiles_per_chunk = len(intra) // (rows // chunk)
    for ti, (qk, per_inst) in enumerate(intra):
        g, tc = divmod(ti, tiles_per_chunk)
        qk_ref[0, g, tc] = qk.astype(BF16)
        for x, (k_cum, v_u, q_dec, k_dec, g_tot) in enumerate(per_inst):
            h = tc * len(per_inst) + x
            kq_ref[0, g, h] = _rows([k_cum, q_dec]).astype(BF16)
            vu_ref[0, g, h] = v_u
            kd_ref[0, g, h] = k_dec.astype(BF16)
            gt_ref[0, g, h] = g_tot[0:SUBLANES, :]


def _delta_prompt_state_kernel(kq_ref, vu_ref, kd_ref, qk_ref, gt_ref, z_ref, nw_ref, o_ref, s_out_ref,
                               s_ref):
    n = pl.program_id(0)
    batch, chunks, heads, c, dv = vu_ref.shape
    tiles = qk_ref.shape[2]
    m = heads // tiles

    @pl.when(n == 0)
    def _():
        s_ref[...] = jnp.zeros(s_ref.shape, F32)

    inst = [(b, h) for b in range(batch) for h in range(heads)]
    state = [s_ref[b, h] for b, h in inst]
    for j in range(chunks):
        res = [_dot(kq_ref[b, j, h], s.astype(BF16)) for (b, h), s in zip(inst, state)]
        v_new = [vu_ref[b, j, h] - r_[:c] for (b, h), r_ in zip(inst, res)]
        vn_bf = [vn.astype(BF16) for vn in v_new]
        o_intra = [_dot(qk_ref[b, j, t], _block_diag(vn_bf[(b * tiles + t) * m:(b * tiles + t + 1) * m]))
                   for b in range(batch) for t in range(tiles)]
        state = [s * gt_ref[b, j, h][0:1, :] + _dot_tn(kd_ref[b, j, h], vn)
                 for (b, h), s, vn in zip(inst, state, vn_bf)]
        for i, (b, h) in enumerate(inst):
            o = res[i][c:] + o_intra[i // m][:, (i % m) * dv:(i % m + 1) * dv]
            cs = slice(h * dv, (h + 1) * dv)
            o_ref[b, j, :, cs] = _gated_out_norm(o, nw_ref[...], z_ref[b, j, :, cs])
    for (b, h), s in zip(inst, state):
        s_ref[b, h] = s

    @pl.when(n == pl.num_programs(0) - 1)
    def _():
        s_out_ref[...] = s_ref[...]


def _delta_prompt(x, z, gb, norm_w, *, batch, heads, dk, dv):
    t, cdim = x.shape
    seq = t // batch
    c = DN_CHUNK if seq % DN_CHUNK == 0 else seq
    assert c % SUBLANES == 0 and dk == dv == LANES
    nc = seq // c
    group = PROMPT_CHUNKS_PER_STEP if nc % PROMPT_CHUNKS_PER_STEP == 0 else 1
    rows = group * c
    steps = seq // rows
    tiles = heads * c // LANES
    blk = lambda n_: pl.BlockSpec((rows, n_), lambda b, n: (b * steps + n, 0))
    per_chunk = lambda per, r_, dt: (
        pl.BlockSpec((1, group, per, r_, LANES), lambda b, n: (b, n, 0, 0, 0)),
        jax.ShapeDtypeStruct((batch, nc, per, r_, LANES), dt))
    outs = [per_chunk(heads, 2 * c, BF16), per_chunk(heads, c, F32), per_chunk(heads, c, BF16),
            per_chunk(tiles, c, BF16), per_chunk(heads, SUBLANES, F32)]
    kq, vu, kd, qk, gt = pl.pallas_call(
        functools.partial(_delta_prompt_intra_kernel, heads=heads, dk=dk, dv=dv, chunk=c),
        grid=(batch, steps),
        in_specs=[blk(cdim), blk(gb.shape[1])],
        out_specs=[o[0] for o in outs],
        out_shape=[o[1] for o in outs],
        compiler_params=_params(("parallel", "parallel")),
        name="delta_prompt_intra",
    )(x, gb)
    per_step = STATE_CHUNKS_PER_STEP if nc % STATE_CHUNKS_PER_STEP == 0 else 1
    all_seq = lambda a: pl.BlockSpec((batch, per_step) + a.shape[2:], lambda n: (0, n) + (0,) * (a.ndim - 2))
    z4 = z.reshape(batch, nc, c, heads * dv)
    o, s_out = pl.pallas_call(
        _delta_prompt_state_kernel,
        grid=(nc // per_step,),
        in_specs=[all_seq(kq), all_seq(vu), all_seq(kd), all_seq(qk), all_seq(gt), all_seq(z4),
                  _const_spec(norm_w.shape)],
        out_specs=[all_seq(z4), _const_spec((batch, heads, dk, dv))],
        out_shape=[jax.ShapeDtypeStruct(z4.shape, BF16), jax.ShapeDtypeStruct((batch, heads, dk, dv), F32)],
        scratch_shapes=[pltpu.VMEM((batch, heads, dk, dv), F32)],
        compiler_params=_params(("arbitrary",)),
        name="delta_prompt_state",
    )(kq, vu, kd, qk, gt, z4, norm_w)
    return o.reshape(t, heads * dv), s_out


def _delta_sample_kernel(x_ref, cbuf_ref, z_ref, gb_ref, s_in_ref, cw_ref, nw_ref, o_ref, s_out_ref,
                         xc_ref, *, heads, dk, dv, conv_w):
    nb, hist, cdim = cbuf_ref.shape
    r = x_ref.shape[0]
    t = r // nb
    xc_ref[:, SUBLANES:SUBLANES + t, :] = x_ref[...].reshape(nb, t, cdim)
    xc_ref[:, SUBLANES - hist:SUBLANES, :] = cbuf_ref[...]
    acc = None
    for jj in range(conv_w):
        off = SUBLANES - (conv_w - 1) + jj
        term = xc_ref[:, off:off + t, :] * cw_ref[jj:jj + 1, :][None]
        acc = term if acc is None else acc + term
    xb = _silu(acc).reshape(r, cdim)

    masks = _chunk_masks(r, t)
    gates = _delta_gates(gb_ref[...], masks, heads, dk)
    insts = []
    for h in range(heads):
        q, k, v = _head_cols(xb, h, heads, dk, dv)
        insts.append((_unit(q) * (dk ** -0.5), _unit(k), v) + gates[h])
    intra = _delta_intra(insts, masks, t)
    rows = [slice(b * t, (b + 1) * t) for b in range(nb)]
    for h in range(heads):
        qk, ((k_cum, v_u, q_dec, k_dec, g_tot),) = intra[h]
        state = [s_in_ref[b, h] for b in range(nb)]
        res = [_dot(jnp.concatenate([k_cum[rs], q_dec[rs]], axis=0).astype(BF16), s.astype(BF16))
               for rs, s in zip(rows, state)]
        v_new = [v_u[rs] - r_[:t] for rs, r_ in zip(rows, res)]
        for b in range(nb):
            s_out_ref[b, h] = (state[b] * g_tot[b * t:b * t + 1, :]
                               + _dot_tn(k_dec[rows[b]].astype(BF16), v_new[b].astype(BF16)))
        o = (jnp.concatenate([r_[t:] for r_ in res], axis=0)
             + _dot(qk.astype(BF16), jnp.concatenate(v_new, axis=0).astype(BF16)))
        o_ref[:, h * dv:(h + 1) * dv] = _gated_out_norm(o, nw_ref[...], z_ref[:, h * dv:(h + 1) * dv])


def _delta_sample(x, cbuf, z, gb, s0, conv_w, norm_w, *, heads, dk, dv):
    rows, cdim = x.shape
    nbatch, hist, _ = cbuf.shape
    t = rows // nbatch
    assert t == SUBLANES and hist == conv_w.shape[0] - 1 and hist <= t and dk == dv
    nb = SAMPLE_GROUP if nbatch % SAMPLE_GROUP == 0 else 1
    r = nb * t
    blk = lambda n_: pl.BlockSpec((r, n_), lambda i: (i, 0))
    sblk = pl.BlockSpec((nb, heads, dk, dv), lambda i: (i, 0, 0, 0))
    return pl.pallas_call(
        functools.partial(_delta_sample_kernel, heads=heads, dk=dk, dv=dv, conv_w=conv_w.shape[0]),
        grid=(nbatch // nb,),
        in_specs=[blk(cdim), pl.BlockSpec((nb, hist, cdim), lambda i: (i, 0, 0)), blk(z.shape[1]),
                  blk(gb.shape[1]), sblk, _const_spec(conv_w.shape), _const_spec(norm_w.shape)],
        out_specs=[blk(heads * dv), sblk],
        out_shape=[jax.ShapeDtypeStruct((rows, heads * dv), BF16), jax.ShapeDtypeStruct(s0.shape, F32)],
        scratch_shapes=[pltpu.VMEM((nb, 2 * SUBLANES, cdim), F32)],
        compiler_params=_params(("parallel",)),
        name="delta_sample",
    )(x, cbuf, z, gb, s0, conv_w, norm_w)


def _layer(xp, xs, ck, cv, cconv, sdelta, lw, dims):
    (f1_pre, f1_post, f1_up, f1_down, mix_pre, mix_post, w_in, sinks, conv_w, a_log, dt_bias,
     out_norm, w_a, w_b, w_o, f2_pre, f2_post, f2_up, f2_down) = lw
    batch, seq, d = xp.shape
    dbatch, dseq, _ = xs.shape
    kvh, hd = dims["kv_heads"], dims["head_dim"]
    hb, dk, dv = dims["b_heads"], dims["dk"], dims["dv"]
    a_heads = w_a.shape[0] // hd
    cdim = conv_w.shape[1]
    d_ff = f1_down.shape[0]
    bf = lambda a: a.astype(BF16)
    row = lambda a: a.reshape(1, -1).astype(F32)

    sizes = (a_heads * hd, kvh * hd, kvh * hd, cdim, hb * dv, hb, hb, d, d)
    offs = [0]
    for s_ in sizes:
        offs.append(offs[-1] + s_)
    w_proj = bf(w_in[:, :offs[5] + LANES])
    w_gates = bf(w_in[:, offs[7]:offs[9]])
    proj_offs = (offs[0], offs[1], offs[3], offs[4], offs[5])
    alog = jnp.pad(row(a_log), ((0, 0), (0, LANES - hb)))
    dtb = jnp.pad(row(dt_bias), ((0, 0), (0, LANES - hb)))

    def ffn(xa, xb, pre, post, up, down):
        return _ffn(xa, xb, row(pre), row(post), bf(up), bf(down))

    def inproj(x, **conv):
        return _inproj(x, row(mix_pre), w_proj, alog, dtb, q_scale=hd ** -0.5, n_heads_b=hb,
                       offs=proj_offs, **conv)

    def merge(group_a, group_b):
        return _merge(group_a, group_b, row(mix_pre), row(mix_post), w_gates, bf(w_a), bf(w_b), bf(w_o))

    sinks = sinks.astype(F32)
    norm_w = row(out_norm)
    conv_w = conv_w.astype(F32)
    hist = conv_w.shape[0] - 1

    x, x_s = ffn(xp.reshape(batch * seq, d), xs.reshape(dbatch * dseq, d), f1_pre, f1_post, f1_up, f1_down)
    q, kv, xb, z, gb, tail = inproj(x, conv_w=conv_w, seq_rows=seq, dk=dk)
    oa, kv_last = _swa_prompt(sinks, q, kv, batch=batch, kv_heads=kvh, head_dim=hd)
    ob, p_state = _delta_prompt(xb, z, gb, norm_w, batch=batch, heads=hb, dk=dk, dv=dv)
    prompt_mix = (x, oa, ob)
    kv_last = kv_last.reshape(batch, WINDOW, 2, kvh, hd)
    p_k, p_v = kv_last[:, :, 0], kv_last[:, :, 1]
    p_conv = tail.reshape(batch, -1, SUBLANES, cdim)[:, -1, SUBLANES - hist:]

    x = x_s
    q, kv, xb, z, gb = inproj(x)
    wb = ck.shape[1]
    oa, s_k, s_v = _swa_sample(sinks, q.reshape(dbatch, dseq, -1), kv.reshape(dbatch, dseq, -1),
                               ck.reshape(dbatch, wb, kvh * hd), cv.reshape(dbatch, wb, kvh * hd),
                               kv_heads=kvh, head_dim=hd)
    ob, s_state = _delta_sample(xb, cconv, z, gb, sdelta, conv_w, norm_w, heads=hb, dk=dk, dv=dv)
    x_p, x_s = merge(prompt_mix, (x, oa.reshape(dbatch * dseq, -1), ob))
    yp, ys = ffn(x_p, x_s, f2_pre, f2_post, f2_up, f2_down)
    yp, ys = yp.reshape(batch, seq, d), ys.reshape(dbatch, dseq, d)
    s_k = s_k.reshape(dbatch, wb, kvh, hd)
    s_v = s_v.reshape(dbatch, wb, kvh, hd)
    s_conv = xb.reshape(dbatch, dseq, cdim)[:, dseq - hist:]
    return yp, ys, (p_k, p_v, p_conv, p_state), (s_k, s_v, s_conv, s_state)


def kernel(x_prompt, x_sample, cache_swa_k, cache_swa_v, state_conv, state_delta, ffn1_norm_pre, ffn1_norm_post, ffn1_w_up, ffn1_w_down, mix_norm_pre, mix_norm_post, w_in, attn_sinks, conv_w, dn_a_log, dn_dt_bias, dn_out_norm, w_branch_a, w_branch_b, w_out, ffn2_norm_pre, ffn2_norm_post, ffn2_w_up, ffn2_w_down):
    weights = (ffn1_norm_pre, ffn1_norm_post, ffn1_w_up, ffn1_w_down, mix_norm_pre, mix_norm_post,
               w_in, attn_sinks, conv_w, dn_a_log, dn_dt_bias, dn_out_norm, w_branch_a, w_branch_b,
               w_out, ffn2_norm_pre, ffn2_norm_post, ffn2_w_up, ffn2_w_down)
    depth = w_in.shape[0]
    dims = dict(kv_heads=cache_swa_k.shape[3], head_dim=cache_swa_k.shape[4],
                b_heads=state_delta.shape[2], dk=state_delta.shape[3], dv=state_delta.shape[4])
    yp, ys = x_prompt, x_sample
    p_out, s_out = [], []
    for l in range(depth):
        yp, ys, p_new, s_new = _layer(yp, ys, cache_swa_k[l], cache_swa_v[l], state_conv[l],
                                      state_delta[l], tuple(w[l] for w in weights), dims)
        p_out.append(p_new)
        s_out.append(s_new)
    stack = lambda outs, i: jnp.stack([o[i] for o in outs])
    return (yp, ys,
            stack(p_out, 0), stack(p_out, 1), stack(p_out, 2), stack(p_out, 3),
            stack(s_out, 0), stack(s_out, 1), stack(s_out, 2), stack(s_out, 3))
```

```python
import functools

import jax
import jax.numpy as jnp
from jax import lax
from jax.experimental import pallas as pl
from jax.experimental.pallas import tpu as pltpu

WINDOW = 128
DN_CHUNK = 64
EPS = 1e-6
F32 = jnp.float32
BF16 = jnp.bfloat16
NEG_BIG = -1e30
LANES = 128
SUBLANES = 8
VMEM_LIMIT_BYTES = 56 * 1024 * 1024
FF_CHUNK = 256
MXU_COLS = 256
CONV_ROW_BLOCK = 128
SAMPLE_GROUP = 16
SWA_SAMPLE_GROUP = 32
SWA_BLOCKS_PER_STEP = 4
STATE_CHUNKS_PER_STEP = 4
PROMPT_CHUNKS_PER_STEP = 16


def _rms(x, w):
    return x * lax.rsqrt(jnp.mean(x * x, axis=-1, keepdims=True) + EPS) * w


def _silu(x):
    return x * jax.nn.sigmoid(x)


def _unit(x):
    return x * lax.rsqrt(jnp.sum(x * x, axis=-1, keepdims=True) + EPS)


def _softplus(x):
    return jnp.maximum(x, 0.0) + jnp.log1p(jnp.exp(-jnp.abs(x)))


def _dot(a, b):
    return jnp.dot(a, b, preferred_element_type=F32)


def _dot_nt(a, b):
    return lax.dot_general(a, b, (((1,), (1,)), ((), ())), preferred_element_type=F32)


def _dot_tn(a, b):
    return lax.dot_general(a, b, (((0,), (0,)), ((), ())), preferred_element_type=F32)


def _split3(b):
    hi = b.astype(BF16)
    r = b - hi.astype(F32)
    mid = r.astype(BF16)
    lo = (r - mid.astype(F32)).astype(BF16)
    return hi, mid, lo


def _dot_sel(sel_bf16, b):
    hi, mid, lo = _split3(b)
    return _dot(sel_bf16, hi) + _dot(sel_bf16, mid) + _dot(sel_bf16, lo)


def _ffn_tile(x_ref, pre_ref, post_ref, wup_ref, wd_ref, o_ref):
    x = x_ref[...]
    xn = _rms(x, pre_ref[...]).astype(BF16)
    d_ff = wd_ref.shape[0]
    acc = jnp.zeros(x.shape, F32)
    chunks = d_ff // FF_CHUNK
    for c0 in range(0, chunks, 2):
        acts = []
        for c in range(c0, min(c0 + 2, chunks)):
            g = _dot(xn, wup_ref[:, c * FF_CHUNK:(c + 1) * FF_CHUNK])
            u = _dot(xn, wup_ref[:, d_ff + c * FF_CHUNK:d_ff + (c + 1) * FF_CHUNK])
            acts.append((_silu(g) * u).astype(BF16))
        rows = slice(c0 * FF_CHUNK, (c0 + len(acts)) * FF_CHUNK)
        acc = acc + _dot(_lanes(acts), wd_ref[rows, :])
    o_ref[...] = x + 0.5 * _rms(acc, post_ref[...])


def _ffn_kernel(xa_ref, xb_ref, pre_ref, post_ref, wup_ref, wd_ref, oa_ref, ob_ref, *, tiles_a):
    i = pl.program_id(0)

    @pl.when(i < tiles_a)
    def _():
        _ffn_tile(xa_ref, pre_ref, post_ref, wup_ref, wd_ref, oa_ref)

    @pl.when(i >= tiles_a)
    def _():
        _ffn_tile(xb_ref, pre_ref, post_ref, wup_ref, wd_ref, ob_ref)


def _const_spec(shape):
    nd = len(shape)
    return pl.BlockSpec(shape, lambda *_: (0,) * nd)


def _weight_spec(shape):
    nd = len(shape)
    return pl.BlockSpec(shape, lambda *_: (0,) * nd, pipeline_mode=pl.Buffered(1))


def _row_tile(t, largest=512):
    for tm in (1024, 512, 256, 128, 64, 32, 16, 8):
        if tm <= largest and t % tm == 0:
            return tm
    raise ValueError(f"token count {t} is not a multiple of 8")


def _params(sem):
    return pltpu.CompilerParams(dimension_semantics=sem, vmem_limit_bytes=VMEM_LIMIT_BYTES)


def _ffn(xa, xb, pre, post, wup, wd):
    (ta, d), tb = xa.shape, xb.shape[0]
    f = wd.shape[0]
    assert f % FF_CHUNK == 0 and wup.shape == (d, 2 * f)
    tm = min(_row_tile(ta), _row_tile(tb))
    na, nb = ta // tm, tb // tm
    row_a = pl.BlockSpec((tm, d), lambda i: (jnp.minimum(i, na - 1), 0))
    row_b = pl.BlockSpec((tm, d), lambda i: (jnp.maximum(i - na, 0), 0))
    return pl.pallas_call(
        functools.partial(_ffn_kernel, tiles_a=na),
        grid=(na + nb,),
        in_specs=[row_a, row_b, _const_spec((1, d)), _const_spec((1, d)),
                  _weight_spec((d, 2 * f)), _weight_spec((f, d))],
        out_specs=[row_a, row_b],
        out_shape=[jax.ShapeDtypeStruct((ta, d), F32), jax.ShapeDtypeStruct((tb, d), F32)],
        compiler_params=_params(("arbitrary",)),
        name="ffn",
    )(xa, xb, pre, post, wup, wd)


def _conv_silu_unit(xc_ref, cv_ref, cw_ref, r0, rows, cb, heads, dk):
    taps = cw_ref.shape[0]
    rs = slice(r0, r0 + rows)
    cs = slice(cb * dk, (cb + 1) * dk)
    acc = None
    for jj in range(taps):
        off = r0 + SUBLANES - (taps - 1) + jj
        term = xc_ref[off:off + rows, cs] * cw_ref[jj:jj + 1, cs]
        acc = term if acc is None else acc + term
    y = _silu(acc)
    if cb < 2 * heads:
        y = _unit(y) * (dk ** -0.5) if cb < heads else _unit(y)
    cv_ref[rs, cs] = y


def _inproj_kernel(x_ref, pre_ref, w_ref, alog_ref, dtb_ref, *rest, q_scale, n_heads_b, offs, conv):
    tm = x_ref.shape[0]
    if conv is None:
        q_ref, kv_ref, b_ref, z_ref, gb_ref, h_ref = rest
        sub = tm
    else:
        cw_ref, q_ref, kv_ref, b_ref, z_ref, gb_ref, tail_ref, h_ref, xc_ref = rest
        seq_rows, dk = conv
        sub = min(tm, CONV_ROW_BLOCK)

        @pl.when((pl.program_id(0) * tm) % seq_rows == 0)
        def _():
            xc_ref[0:SUBLANES, :] = jnp.zeros((SUBLANES, xc_ref.shape[1]), F32)

    h_ref[...] = _rms(x_ref[...], pre_ref[...]).astype(BF16)

    def project(r0, i, c0, width):
        rs = slice(r0, r0 + sub)
        cs = slice(c0, c0 + width)
        y = _dot(h_ref[rs, :], w_ref[:, offs[i] + c0:offs[i] + c0 + width])
        if i == 0:
            q_ref[rs, cs] = (y * q_scale).astype(BF16)
        elif i == 1:
            kv_ref[rs, cs] = y
        elif i == 2 and conv is None:
            b_ref[rs, cs] = y
        elif i == 2:
            xc_ref[SUBLANES + r0:SUBLANES + r0 + sub, cs] = y
        elif i == 3:
            z_ref[rs, cs] = y
        else:
            g = -jnp.exp(alog_ref[...]) * _softplus(y + dtb_ref[...])
            lane = lax.broadcasted_iota(jnp.int32, y.shape, 1)
            gb_ref[rs, :] = jnp.where(lane < n_heads_b, g, jax.nn.sigmoid(y))

    widths = [offs[i + 1] - offs[i] for i in range(4)] + [LANES]
    pieces = [(i, c0, min(MXU_COLS, widths[i] - c0))
              for i in (2, 0, 1, 3, 4) for c0 in range(0, widths[i], MXU_COLS)]
    for r0 in range(0, tm + (sub if conv is not None else 0), sub):
        mm = [functools.partial(project, r0, *p) for p in pieces] if r0 < tm else []
        cv = []
        if conv is not None and r0 > 0:
            cv = [functools.partial(_conv_silu_unit, xc_ref, b_ref, cw_ref, r0 - sub, sub, cb, n_heads_b, dk)
                  for cb in range(xc_ref.shape[1] // dk)]
        for k in range(max(len(mm), len(cv))):
            if k < len(mm):
                mm[k]()
            if k < len(cv):
                cv[k]()
    if conv is not None:
        tail_ref[0] = xc_ref[tm:tm + SUBLANES, :]
        xc_ref[0:SUBLANES, :] = xc_ref[tm:tm + SUBLANES, :]


def _inproj(x, pre, w, alog, dtb, *, q_scale, n_heads_b, offs, conv_w=None, seq_rows=None, dk=None):
    t, d = x.shape
    tm = _row_tile(t if conv_w is None else seq_rows, largest=1024)
    assert all(o % LANES == 0 for o in offs[:5]) and offs[4] + LANES <= w.shape[1]
    row = lambda n: pl.BlockSpec((tm, n), lambda i: (i, 0))
    cdim = offs[3] - offs[2]
    outs = [(offs[1] - offs[0], BF16), (offs[2] - offs[1], F32), (cdim, F32), (offs[4] - offs[3], F32),
            (LANES, F32)]
    in_specs = [row(d), _const_spec((1, d)), _weight_spec(w.shape),
                _const_spec(alog.shape), _const_spec(dtb.shape)]
    out_specs = [row(n) for n, _ in outs]
    out_shape = [jax.ShapeDtypeStruct((t, n), dt) for n, dt in outs]
    args, scratch, conv = [x, pre, w, alog, dtb], [pltpu.VMEM((tm, d), BF16)], None
    if conv_w is not None:
        assert seq_rows % tm == 0 and conv_w.shape[0] - 1 <= SUBLANES
        in_specs.append(_const_spec(conv_w.shape))
        args.append(conv_w)
        out_specs.append(pl.BlockSpec((1, SUBLANES, cdim), lambda i: (i, 0, 0)))
        out_shape.append(jax.ShapeDtypeStruct((t // tm, SUBLANES, cdim), F32))
        scratch.append(pltpu.VMEM((tm + SUBLANES, cdim), F32))
        conv = (seq_rows, dk)
    return pl.pallas_call(
        functools.partial(_inproj_kernel, q_scale=q_scale, n_heads_b=n_heads_b, offs=offs, conv=conv),
        grid=(t // tm,),
        in_specs=in_specs,
        out_specs=out_specs,
        out_shape=out_shape,
        scratch_shapes=scratch,
        compiler_params=_params(("arbitrary",)),
        name="inproj",
    )(*args)


def _merge_tile(x_ref, oa_ref, ob_ref, pre_ref, post_ref, wg_ref, wa_ref, wb_ref, wo_ref, o_ref):
    x = x_ref[...]
    h = _rms(x, pre_ref[...]).astype(BF16)
    oa, ob = oa_ref[...], ob_ref[...]
    m = jnp.zeros(x.shape, F32)
    d = wo_ref.shape[0]
    for c0 in range(0, d, MXU_COLS):
        cs = slice(c0, c0 + MXU_COLS)
        merged = (jax.nn.sigmoid(_dot(h, wg_ref[:, cs])) * _dot(oa, wa_ref[:, cs])
                  + jax.nn.sigmoid(_dot(h, wg_ref[:, d + c0:d + c0 + MXU_COLS])) * _dot(ob, wb_ref[:, cs]))
        m = m + _dot(merged.astype(BF16), wo_ref[cs, :])
    o_ref[...] = x + _rms(m, post_ref[...])


def _merge_kernel(xa_ref, oaa_ref, oba_ref, xb_ref, oab_ref, obb_ref, *rest, tiles_a):
    *shared, oa_ref, ob_ref = rest
    i = pl.program_id(0)

    @pl.when(i < tiles_a)
    def _():
        _merge_tile(xa_ref, oaa_ref, oba_ref, *shared, oa_ref)

    @pl.when(i >= tiles_a)
    def _():
        _merge_tile(xb_ref, oab_ref, obb_ref, *shared, ob_ref)


def _merge(group_a, group_b, pre, post, wg, wa, wb, wo):
    (ta, d), tb = group_a[0].shape, group_b[0].shape[0]
    tm = min(_row_tile(ta, largest=1024), _row_tile(tb, largest=1024))
    na, nb = ta // tm, tb // tm
    rows_a = [pl.BlockSpec((tm, a.shape[1]), lambda i: (jnp.minimum(i, na - 1), 0)) for a in group_a]
    rows_b = [pl.BlockSpec((tm, a.shape[1]), lambda i: (jnp.maximum(i - na, 0), 0)) for a in group_b]
    weights = (wg, wa, wb, wo)
    return pl.pallas_call(
        functools.partial(_merge_kernel, tiles_a=na),
        grid=(na + nb,),
        in_specs=rows_a + rows_b + [_const_spec((1, d)), _const_spec((1, d))]
        + [_weight_spec(w.shape) for w in weights],
        out_specs=[rows_a[0], rows_b[0]],
        out_shape=[jax.ShapeDtypeStruct((ta, d), F32), jax.ShapeDtypeStruct((tb, d), F32)],
        compiler_params=_params(("arbitrary",)),
        name="merge",
    )(*group_a, *group_b, pre, post, *weights)


def _sink_softmax(s, mask, sink):
    s = jnp.where(mask, s, NEG_BIG)
    m = jnp.maximum(jnp.max(s, axis=-1, keepdims=True), sink)
    e = jnp.exp(s - m)
    return e / (jnp.sum(e, axis=-1, keepdims=True) + jnp.exp(sink - m))


def _swa_prompt_kernel(sink_ref, q_ref, kvp_ref, kvc_ref, o_ref, kv_last_ref, *, kv_heads, group,
                       head_dim):
    n = pl.program_id(1)
    w = WINDOW
    blocks = q_ref.shape[0] // w
    dkv = kv_heads * head_dim
    pairs = group // 2
    kv_all = jnp.concatenate([kvp_ref[...], kvc_ref[...]], axis=0)

    @pl.when(n == pl.num_programs(1) - 1)
    def _():
        kv_last_ref[0] = kvc_ref[(blocks - 1) * w:blocks * w, :]
    lo = lax.broadcasted_iota(jnp.int32, (kv_all.shape[0], dkv), 1) < head_dim

    def lane_halves(x):
        x_rot = pltpu.roll(x, head_dim, axis=1)
        pick = lambda kv, half: jnp.where(lo if half == 0 else ~lo, x if kv == half else x_rot, 0.0)
        return [[pick(kv, half).astype(BF16) for half in range(2)] for kv in range(kv_heads)]

    k_ext = lane_halves(kv_all[:, :dkv])
    v_ext = lane_halves(kv_all[:, dkv:])

    i = lax.broadcasted_iota(jnp.int32, (w, 2 * w), 0)
    j = lax.broadcasted_iota(jnp.int32, (w, 2 * w), 1)
    bias = jnp.where(j > i, jnp.where(j <= i + w, 0.0, NEG_BIG), NEG_BIG)
    bias_seq_start = jnp.where(j >= w, bias, NEG_BIG)
    lo_out = lax.broadcasted_iota(jnp.int32, (w, 2 * head_dim), 1) < head_dim

    ones_half = [jnp.where(lo if half == 0 else ~lo, 1.0, 0.0).astype(BF16) for half in range(2)]
    v_ext = [[jnp.concatenate([v_ext[kv][half], ones_half[half]], axis=1) for half in range(2)]
             for kv in range(kv_heads)]

    inst = [(t, kv) for t in range(blocks) for kv in range(kv_heads)]
    band = lambda x, t: x[t * w:(t + 2) * w]
    q2 = [jnp.concatenate([q_ref[t * w:(t + 1) * w, (kv * pairs + p) * LANES:(kv * pairs + p + 1) * LANES]
                           for p in range(pairs)], axis=0) for t, kv in inst]
    scores = [[_dot_nt(q_, band(k_ext[kv][half], t)) for half in range(2)]
              for q_, (t, kv) in zip(q2, inst)]
    bias_first = jnp.where(n > 0, bias, bias_seq_start)
    for (t, kv), sc in zip(inst, scores):
        vband = jnp.concatenate([band(v_ext[kv][0], t), band(v_ext[kv][1], t)], axis=0)
        for p in range(pairs):
            exps, sink_terms = [], []
            for half in range(2):
                sink = sink_ref[kv * group + 2 * p + half]
                s = sc[half][p * w:(p + 1) * w] + (bias_first if t == 0 else bias)
                m = jnp.maximum(jnp.max(s, axis=-1, keepdims=True), sink)
                exps.append(jnp.exp(s - m).astype(BF16))
                sink_terms.append(jnp.exp(sink - m))
            o = _dot(jnp.concatenate(exps, axis=1), vband)
            den = o[:, LANES:] + jnp.where(lo_out, sink_terms[0], sink_terms[1])
            cs = slice((kv * pairs + p) * LANES, (kv * pairs + p + 1) * LANES)
            o_ref[t * w:(t + 1) * w, cs] = (o[:, :LANES] / den).astype(o_ref.dtype)


def _swa_prompt(sinks, q, kv, *, batch, kv_heads, head_dim):
    t, dq = q.shape
    seq = t // batch
    assert seq % WINDOW == 0
    group = dq // (kv_heads * head_dim)
    assert 2 * head_dim == LANES and kv_heads == 2 and group % 2 == 0
    blocks = SWA_BLOCKS_PER_STEP if (seq // WINDOW) % SWA_BLOCKS_PER_STEP == 0 else 1
    nb = seq // (blocks * WINDOW)
    rows = blocks * WINDOW
    return pl.pallas_call(
        functools.partial(_swa_prompt_kernel, kv_heads=kv_heads, group=group, head_dim=head_dim),
        grid=(batch, nb),
        in_specs=[pl.BlockSpec(memory_space=pltpu.SMEM),
                  pl.BlockSpec((rows, dq), lambda b, n: (b * nb + n, 0)),
                  pl.BlockSpec((WINDOW, kv.shape[1]),
                               lambda b, n: ((b * nb + n) * blocks - jnp.minimum(n, 1), 0)),
                  pl.BlockSpec((rows, kv.shape[1]), lambda b, n: (b * nb + n, 0))],
        out_specs=[pl.BlockSpec((rows, dq), lambda b, n: (b * nb + n, 0)),
                   pl.BlockSpec((1, WINDOW, kv.shape[1]), lambda b, n: (b, 0, 0))],
        out_shape=[jax.ShapeDtypeStruct((t, dq), BF16),
                   jax.ShapeDtypeStruct((batch, WINDOW, kv.shape[1]), F32)],
        compiler_params=_params(("parallel", "arbitrary")),
        name="swa_prompt",
    )(sinks, q, kv, kv)


def _swa_sample_kernel(sink_ref, q_ref, kvn_ref, ck_ref, cv_ref, o_ref, nk_ref, nv_ref, *,
                       kv_heads, group, head_dim):
    bb, t, _ = q_ref.shape
    wb = ck_ref.shape[1]
    dkv = kv_heads * head_dim
    q = q_ref[...]
    kc = jnp.concatenate([ck_ref[...], kvn_ref[:, :, :dkv]], axis=1)
    vc = jnp.concatenate([cv_ref[...], kvn_ref[:, :, dkv:]], axis=1)
    nk_ref[...] = kc[:, t:, :]
    nv_ref[...] = vc[:, t:, :]
    kcb = kc.astype(BF16)
    vcb = vc.astype(BF16)
    assert t & (t - 1) == 0
    i = lax.broadcasted_iota(jnp.int32, (group * t, wb + t), 0) & (t - 1)
    j = lax.broadcasted_iota(jnp.int32, (group * t, wb + t), 1)
    diff = wb + i - j
    mask = ((diff >= 0) & (diff < WINDOW))[None]
    hs = [slice(kv * head_dim, (kv + 1) * head_dim) for kv in range(kv_heads)]
    heads = [[kv * group + g for g in range(group)] for kv in range(kv_heads)]
    qs = [jnp.concatenate([q[:, :, h * head_dim:(h + 1) * head_dim] for h in hd_], axis=1) for hd_ in heads]
    sinks = [jnp.concatenate([jnp.full((1, t, 1), sink_ref[h], F32) for h in hd_], axis=1) for hd_ in heads]
    scores = [jnp.einsum('bqd,bkd->bqk', q_, kcb[:, :, h_], preferred_element_type=F32)
              for q_, h_ in zip(qs, hs)]
    probs = [_sink_softmax(s, mask, sink).astype(BF16) for s, sink in zip(scores, sinks)]
    outs = [jnp.einsum('bqk,bkd->bqd', p, vcb[:, :, h_], preferred_element_type=F32)
            for p, h_ in zip(probs, hs)]
    o_ref[...] = jnp.concatenate([o[:, g * t:(g + 1) * t, :] for o in outs for g in range(group)],
                                 axis=2).astype(o_ref.dtype)


def _swa_sample(sinks, q, kvn, ck, cv, *, kv_heads, head_dim):
    nbatch, t, dq = q.shape
    wb = ck.shape[1]
    group = dq // (kv_heads * head_dim)
    bb = SWA_SAMPLE_GROUP if nbatch % SWA_SAMPLE_GROUP == 0 else 1
    blk = lambda a: pl.BlockSpec((bb,) + a.shape[1:], lambda i: (i, 0, 0))
    return pl.pallas_call(
        functools.partial(_swa_sample_kernel, kv_heads=kv_heads, group=group, head_dim=head_dim),
        grid=(nbatch // bb,),
        in_specs=[pl.BlockSpec(memory_space=pltpu.SMEM), blk(q), blk(kvn), blk(ck), blk(cv)],
        out_specs=[blk(q), blk(ck), blk(cv)],
        out_shape=[jax.ShapeDtypeStruct(q.shape, BF16), jax.ShapeDtypeStruct(ck.shape, F32),
                   jax.ShapeDtypeStruct(cv.shape, F32)],
        compiler_params=_params(("parallel",)),
        name="swa_sample",
    )(sinks, q, kvn, ck, cv)


def _chunk_masks(r, c):
    assert r & (r - 1) == 0 and c & (c - 1) == 0 and LANES % r == 0
    width = 2 * LANES
    ri = lax.broadcasted_iota(jnp.int32, (r, width), 0)
    cj = lax.broadcasted_iota(jnp.int32, (r, width), 1)
    ci = cj & (r - 1)
    incl, strict = ci <= ri, ci < ri
    if r != c:
        shift = c.bit_length() - 1
        same = (ri >> shift) == (ci >> shift)
        incl, strict = same & incl, same & strict
    f01 = lambda mask: jnp.where(mask, 1.0, 0.0)
    block = cj >> (r.bit_length() - 1)
    first_in_tile = lax.broadcasted_iota(jnp.int32, (r, LANES), 1) < r
    return dict(incl=incl, strict=strict, eye=f01(ri == ci), first_in_tile=first_in_tile,
                keep=[f01(block == x).astype(BF16) for x in range(width // r)],
                incl_sel=f01(incl)[:, :r].astype(BF16),
                same_sel=None if r == c else f01(same)[:, :r].astype(BF16))


def _delta_gates(gb, masks, heads, dk):
    r = gb.shape[0]
    gc_all = _dot_sel(masks["incl_sel"], gb)
    if masks["same_sel"] is None:
        gl_all = jnp.broadcast_to(gc_all[r - 1:r, :], gc_all.shape)
    else:
        gl_all = _dot_sel(masks["same_sel"], gb)
    bc = lambda a, lane: jnp.broadcast_to(a[:, lane:lane + 1], (r, dk))
    return [(bc(gc_all, h), bc(gl_all, h), bc(gb, heads + h)) for h in range(heads)]


def _split2(x):
    hi = x.astype(BF16)
    return hi, (x - hi.astype(F32)).astype(BF16)


def _rows(xs):
    return jnp.concatenate(xs, axis=0)


def _lanes(xs):
    return xs[0] if len(xs) == 1 else jnp.concatenate(xs, axis=1)


def _block_diag(blocks):
    n = len(blocks)
    z = jnp.zeros(blocks[0].shape, BF16)
    return _rows([_lanes([blocks[i] if j == i else z for j in range(n)]) for i in range(n)])


def _delta_intra(insts, masks, c):
    r, dk = insts[0][0].shape
    dv = insts[0][2].shape[1]
    m = LANES // r
    n = 2 * m
    assert dk == LANES and len(insts) % n == 0
    incl, strict, eye, keep = masks["incl"], masks["strict"], masks["eye"], masks["keep"]
    tiles = [list(range(i, i + m)) for i in range(0, len(insts), m)]
    groups = [(2 * i, 2 * i + 1) for i in range(len(tiles) // 2)]
    qs, ks, vs, gcs, gls, betas = ([i[j] for i in insts] for j in range(6))
    bf = lambda x: x.astype(BF16)

    def lane_tile(xs):
        return xs[0] if m == 1 else jnp.where(masks["first_in_tile"], xs[0], xs[1])

    def block_diag_x(x):
        return _rows([x * keep[i] for i in range(n)])

    g_cols = [_lanes([lane_tile([gcs[i] for i in tiles[t]]) for t in g]) for g in groups]
    g_rows = [jnp.transpose(_rows([gcs[i] for t in g for i in tiles[t]]))[:r, :] for g in groups]
    decays = [jnp.where(incl, jnp.exp(gc - gr), 0.0) for gc, gr in zip(g_cols, g_rows)]
    egcs = [jnp.exp(gc) for gc in gcs]
    kbs = [k * beta for k, beta in zip(ks, betas)]
    kq = [_dot_nt(_rows([_lanes([bf(kbs[i]) for i in t]), _lanes([bf(qs[i]) for i in t])]),
                  _block_diag([bf(ks[i]) for i in t])) for t in tiles]
    ms = [jnp.where(strict, _lanes([kq[t][:r] for t in g]) * decay, 0.0) for g, decay in zip(groups, decays)]
    qks = [_lanes([kq[t][r:] for t in g]) * decay for g, decay in zip(groups, decays)]
    rounds = max((c - 1).bit_length(), 1)
    ps = [-m_ for m_ in ms]
    tinvs = [None] * len(groups)
    for j in range(rounds):
        square = j + 1 < rounds
        if not square and tinvs[0] is None:
            tinvs = [eye + p for p in ps]
            break
        p_hl = [_split2(p) for p in ps]
        weights = [block_diag_x(hi) for hi, _ in p_hl]
        lhs = [_rows((list(hl) if square else []) + ([] if t is None else list(_split2(t))))
               for hl, t in zip(p_hl, tinvs)]
        res = [_dot(a, w) for a, w in zip(lhs, weights)]
        new_ps, new_ts = [], []
        for p, t, a in zip(ps, tinvs, res):
            o = 0
            if square:
                new_ps.append(a[:r] + a[r:2 * r])
                o = 2 * r
            new_ts.append(eye + p if t is None else t + (a[o:o + r] + a[o + r:o + 2 * r]))
        ps, tinvs = new_ps if square else ps, new_ts
    rhs = [jnp.concatenate([kb * egc, v * beta], axis=1).astype(BF16)
           for kb, egc, v, beta in zip(kbs, egcs, vs, betas)]
    t_hl = [_split2(t) for t in tinvs]
    sols = []
    for ti, t in enumerate(tiles):
        hi, lo = (x[:, (ti % 2) * LANES:(ti % 2 + 1) * LANES] for x in t_hl[ti // 2])
        a = _dot(_rows([hi, lo]), _block_diag([rhs[i] for i in t]))
        sols.append(a[:r] + a[r:])
    out = []
    for ti, t in enumerate(tiles):
        per_inst = []
        for slot, i in enumerate(t):
            s_ = sols[ti][:, slot * (dk + dv):(slot + 1) * (dk + dv)]
            per_inst.append((s_[:, :dk], s_[:, dk:], qs[i] * egcs[i], ks[i] * jnp.exp(gls[i] - gcs[i]),
                             jnp.exp(gls[i])))
        out.append((qks[ti // 2][:, (ti % 2) * LANES:(ti % 2 + 1) * LANES], per_inst))
    return out


def _gated_out_norm(o, w, z):
    return (_rms(o, w) * _silu(z)).astype(BF16)


def _head_cols(xb, h, heads, dk, dv):
    q = xb[:, h * dk:(h + 1) * dk]
    k = xb[:, (heads + h) * dk:(heads + h + 1) * dk]
    v = xb[:, 2 * heads * dk + h * dv:2 * heads * dk + (h + 1) * dv]
    return q, k, v


def _delta_prompt_intra_kernel(x_ref, gb_ref, kq_ref, vu_ref, kd_ref, qk_ref, gt_ref, *,
                               heads, dk, dv, chunk):
    rows = x_ref.shape[0]
    xb = x_ref[...]
    gb = gb_ref[...]
    masks = _chunk_masks(chunk, chunk)
    insts = []
    for g in range(rows // chunk):
        rs = slice(g * chunk, (g + 1) * chunk)
        gates = _delta_gates(gb[rs], masks, heads, dk)
        insts.extend(_head_cols(xb[rs], h, heads, dk, dv) + gates[h] for h in range(heads))
    intra = _delta_intra(insts, masks, chunk)

    tiles_per_chunk = len(intra) // (rows // chunk)
    for ti, (qk, per_inst) in enumerate(intra):
        g, tc = divmod(ti, tiles_per_chunk)
        qk_ref[0, g, tc] = qk.astype(BF16)
        for x, (k_cum, v_u, q_dec, k_dec, g_tot) in enumerate(per_inst):
            h = tc * len(per_inst) + x
            kq_ref[0, g, h] = _rows([k_cum, q_dec]).astype(BF16)
            vu_ref[0, g, h] = v_u
            kd_ref[0, g, h] = k_dec.astype(BF16)
            gt_ref[0, g, h] = g_tot[0:SUBLANES, :]


def _delta_prompt_state_kernel(kq_ref, vu_ref, kd_ref, qk_ref, gt_ref, z_ref, nw_ref, o_ref, s_out_ref,
                               s_ref):
    n = pl.program_id(0)
    batch, chunks, heads, c, dv = vu_ref.shape
    tiles = qk_ref.shape[2]
    m = heads // tiles

    @pl.when(n == 0)
    def _():
        s_ref[...] = jnp.zeros(s_ref.shape, F32)

    inst = [(b, h) for b in range(batch) for h in range(heads)]
    state = [s_ref[b, h] for b, h in inst]
    for j in range(chunks):
        res = [_dot(kq_ref[b, j, h], s.astype(BF16)) for (b, h), s in zip(inst, state)]
        v_new = [vu_ref[b, j, h] - r_[:c] for (b, h), r_ in zip(inst, res)]
        vn_bf = [vn.astype(BF16) for vn in v_new]
        o_intra = [_dot(qk_ref[b, j, t], _block_diag(vn_bf[(b * tiles + t) * m:(b * tiles + t + 1) * m]))
                   for b in range(batch) for t in range(tiles)]
        state = [s * gt_ref[b, j, h][0:1, :] + _dot_tn(kd_ref[b, j, h], vn)
                 for (b, h), s, vn in zip(inst, state, vn_bf)]
        for i, (b, h) in enumerate(inst):
            o = res[i][c:] + o_intra[i // m][:, (i % m) * dv:(i % m + 1) * dv]
            cs = slice(h * dv, (h + 1) * dv)
            o_ref[b, j, :, cs] = _gated_out_norm(o, nw_ref[...], z_ref[b, j, :, cs])
    for (b, h), s in zip(inst, state):
        s_ref[b, h] = s

    @pl.when(n == pl.num_programs(0) - 1)
    def _():
        s_out_ref[...] = s_ref[...]


def _delta_prompt(x, z, gb, norm_w, *, batch, heads, dk, dv):
    t, cdim = x.shape
    seq = t // batch
    c = DN_CHUNK if seq % DN_CHUNK == 0 else seq
    assert c % SUBLANES == 0 and dk == dv == LANES
    nc = seq // c
    group = PROMPT_CHUNKS_PER_STEP if nc % PROMPT_CHUNKS_PER_STEP == 0 else 1
    rows = group * c
    steps = seq // rows
    tiles = heads * c // LANES
    blk = lambda n_: pl.BlockSpec((rows, n_), lambda b, n: (b * steps + n, 0))
    per_chunk = lambda per, r_, dt: (
        pl.BlockSpec((1, group, per, r_, LANES), lambda b, n: (b, n, 0, 0, 0)),
        jax.ShapeDtypeStruct((batch, nc, per, r_, LANES), dt))
    outs = [per_chunk(heads, 2 * c, BF16), per_chunk(heads, c, F32), per_chunk(heads, c, BF16),
            per_chunk(tiles, c, BF16), per_chunk(heads, SUBLANES, F32)]
    kq, vu, kd, qk, gt = pl.pallas_call(
        functools.partial(_delta_prompt_intra_kernel, heads=heads, dk=dk, dv=dv, chunk=c),
        grid=(batch, steps),
        in_specs=[blk(cdim), blk(gb.shape[1])],
        out_specs=[o[0] for o in outs],
        out_shape=[o[1] for o in outs],
        compiler_params=_params(("parallel", "parallel")),
        name="delta_prompt_intra",
    )(x, gb)
    per_step = STATE_CHUNKS_PER_STEP if nc % STATE_CHUNKS_PER_STEP == 0 else 1
    all_seq = lambda a: pl.BlockSpec((batch, per_step) + a.shape[2:], lambda n: (0, n) + (0,) * (a.ndim - 2))
    z4 = z.reshape(batch, nc, c, heads * dv)
    o, s_out = pl.pallas_call(
        _delta_prompt_state_kernel,
        grid=(nc // per_step,),
        in_specs=[all_seq(kq), all_seq(vu), all_seq(kd), all_seq(qk), all_seq(gt), all_seq(z4),
                  _const_spec(norm_w.shape)],
        out_specs=[all_seq(z4), _const_spec((batch, heads, dk, dv))],
        out_shape=[jax.ShapeDtypeStruct(z4.shape, BF16), jax.ShapeDtypeStruct((batch, heads, dk, dv), F32)],
        scratch_shapes=[pltpu.VMEM((batch, heads, dk, dv), F32)],
        compiler_params=_params(("arbitrary",)),
        name="delta_prompt_state",
    )(kq, vu, kd, qk, gt, z4, norm_w)
    return o.reshape(t, heads * dv), s_out


def _delta_sample_kernel(x_ref, cbuf_ref, z_ref, gb_ref, s_in_ref, cw_ref, nw_ref, o_ref, s_out_ref,
                         xc_ref, *, heads, dk, dv, conv_w):
    nb, hist, cdim = cbuf_ref.shape
    r = x_ref.shape[0]
    t = r // nb
    xc_ref[:, SUBLANES:SUBLANES + t, :] = x_ref[...].reshape(nb, t, cdim)
    xc_ref[:, SUBLANES - hist:SUBLANES, :] = cbuf_ref[...]
    acc = None
    for jj in range(conv_w):
        off = SUBLANES - (conv_w - 1) + jj
        term = xc_ref[:, off:off + t, :] * cw_ref[jj:jj + 1, :][None]
        acc = term if acc is None else acc + term
    xb = _silu(acc).reshape(r, cdim)

    masks = _chunk_masks(r, t)
    gates = _delta_gates(gb_ref[...], masks, heads, dk)
    insts = []
    for h in range(heads):
        q, k, v = _head_cols(xb, h, heads, dk, dv)
        insts.append((_unit(q) * (dk ** -0.5), _unit(k), v) + gates[h])
    intra = _delta_intra(insts, masks, t)
    rows = [slice(b * t, (b + 1) * t) for b in range(nb)]
    for h in range(heads):
        qk, ((k_cum, v_u, q_dec, k_dec, g_tot),) = intra[h]
        state = [s_in_ref[b, h] for b in range(nb)]
        res = [_dot(jnp.concatenate([k_cum[rs], q_dec[rs]], axis=0).astype(BF16), s.astype(BF16))
               for rs, s in zip(rows, state)]
        v_new = [v_u[rs] - r_[:t] for rs, r_ in zip(rows, res)]
        for b in range(nb):
            s_out_ref[b, h] = (state[b] * g_tot[b * t:b * t + 1, :]
                               + _dot_tn(k_dec[rows[b]].astype(BF16), v_new[b].astype(BF16)))
        o = (jnp.concatenate([r_[t:] for r_ in res], axis=0)
             + _dot(qk.astype(BF16), jnp.concatenate(v_new, axis=0).astype(BF16)))
        o_ref[:, h * dv:(h + 1) * dv] = _gated_out_norm(o, nw_ref[...], z_ref[:, h * dv:(h + 1) * dv])


def _delta_sample(x, cbuf, z, gb, s0, conv_w, norm_w, *, heads, dk, dv):
    rows, cdim = x.shape
    nbatch, hist, _ = cbuf.shape
    t = rows // nbatch
    assert t == SUBLANES and hist == conv_w.shape[0] - 1 and hist <= t and dk == dv
    nb = SAMPLE_GROUP if nbatch % SAMPLE_GROUP == 0 else 1
    r = nb * t
    blk = lambda n_: pl.BlockSpec((r, n_), lambda i: (i, 0))
    sblk = pl.BlockSpec((nb, heads, dk, dv), lambda i: (i, 0, 0, 0))
    return pl.pallas_call(
        functools.partial(_delta_sample_kernel, heads=heads, dk=dk, dv=dv, conv_w=conv_w.shape[0]),
        grid=(nbatch // nb,),
        in_specs=[blk(cdim), pl.BlockSpec((nb, hist, cdim), lambda i: (i, 0, 0)), blk(z.shape[1]),
                  blk(gb.shape[1]), sblk, _const_spec(conv_w.shape), _const_spec(norm_w.shape)],
        out_specs=[blk(heads * dv), sblk],
        out_shape=[jax.ShapeDtypeStruct((rows, heads * dv), BF16), jax.ShapeDtypeStruct(s0.shape, F32)],
        scratch_shapes=[pltpu.VMEM((nb, 2 * SUBLANES, cdim), F32)],
        compiler_params=_params(("parallel",)),
        name="delta_sample",
    )(x, cbuf, z, gb, s0, conv_w, norm_w)


def _layer(xp, xs, ck, cv, cconv, sdelta, lw, dims):
    (f1_pre, f1_post, f1_up, f1_down, mix_pre, mix_post, w_in, sinks, conv_w, a_log, dt_bias,
     out_norm, w_a, w_b, w_o, f2_pre, f2_post, f2_up, f2_down) = lw
    batch, seq, d = xp.shape
    dbatch, dseq, _ = xs.shape
    kvh, hd = dims["kv_heads"], dims["head_dim"]
    hb, dk, dv = dims["b_heads"], dims["dk"], dims["dv"]
    a_heads = w_a.shape[0] // hd
    cdim = conv_w.shape[1]
    d_ff = f1_down.shape[0]
    bf = lambda a: a.astype(BF16)
    row = lambda a: a.reshape(1, -1).astype(F32)

    sizes = (a_heads * hd, kvh * hd, kvh * hd, cdim, hb * dv, hb, hb, d, d)
    offs = [0]
    for s_ in sizes:
        offs.append(offs[-1] + s_)
    w_proj = bf(w_in[:, :offs[5] + LANES])
    w_gates = bf(w_in[:, offs[7]:offs[9]])
    proj_offs = (offs[0], offs[1], offs[3], offs[4], offs[5])
    alog = jnp.pad(row(a_log), ((0, 0), (0, LANES - hb)))
    dtb = jnp.pad(row(dt_bias), ((0, 0), (0, LANES - hb)))

    def ffn(xa, xb, pre, post, up, down):
        return _ffn(xa, xb, row(pre), row(post), bf(up), bf(down))

    def inproj(x, **conv):
        return _inproj(x, row(mix_pre), w_proj, alog, dtb, q_scale=hd ** -0.5, n_heads_b=hb,
                       offs=proj_offs, **conv)

    def merge(group_a, group_b):
        return _merge(group_a, group_b, row(mix_pre), row(mix_post), w_gates, bf(w_a), bf(w_b), bf(w_o))

    sinks = sinks.astype(F32)
    norm_w = row(out_norm)
    conv_w = conv_w.astype(F32)
    hist = conv_w.shape[0] - 1

    x, x_s = ffn(xp.reshape(batch * seq, d), xs.reshape(dbatch * dseq, d), f1_pre, f1_post, f1_up, f1_down)
    q, kv, xb, z, gb, tail = inproj(x, conv_w=conv_w, seq_rows=seq, dk=dk)
    oa, kv_last = _swa_prompt(sinks, q, kv, batch=batch, kv_heads=kvh, head_dim=hd)
    ob, p_state = _delta_prompt(xb, z, gb, norm_w, batch=batch, heads=hb, dk=dk, dv=dv)
    prompt_mix = (x, oa, ob)
    kv_last = kv_last.reshape(batch, WINDOW, 2, kvh, hd)
    p_k, p_v = kv_last[:, :, 0], kv_last[:, :, 1]
    p_conv = tail.reshape(batch, -1, SUBLANES, cdim)[:, -1, SUBLANES - hist:]

    x = x_s
    q, kv, xb, z, gb = inproj(x)
    wb = ck.shape[1]
    oa, s_k, s_v = _swa_sample(sinks, q.reshape(dbatch, dseq, -1), kv.reshape(dbatch, dseq, -1),
                               ck.reshape(dbatch, wb, kvh * hd), cv.reshape(dbatch, wb, kvh * hd),
                               kv_heads=kvh, head_dim=hd)
    ob, s_state = _delta_sample(xb, cconv, z, gb, sdelta, conv_w, norm_w, heads=hb, dk=dk, dv=dv)
    x_p, x_s = merge(prompt_mix, (x, oa.reshape(dbatch * dseq, -1), ob))
    yp, ys = ffn(x_p, x_s, f2_pre, f2_post, f2_up, f2_down)
    yp, ys = yp.reshape(batch, seq, d), ys.reshape(dbatch, dseq, d)
    s_k = s_k.reshape(dbatch, wb, kvh, hd)
    s_v = s_v.reshape(dbatch, wb, kvh, hd)
    s_conv = xb.reshape(dbatch, dseq, cdim)[:, dseq - hist:]
    return yp, ys, (p_k, p_v, p_conv, p_state), (s_k, s_v, s_conv, s_state)


def kernel(x_prompt, x_sample, cache_swa_k, cache_swa_v, state_conv, state_delta, ffn1_norm_pre, ffn1_norm_post, ffn1_w_up, ffn1_w_down, mix_norm_pre, mix_norm_post, w_in, attn_sinks, conv_w, dn_a_log, dn_dt_bias, dn_out_norm, w_branch_a, w_branch_b, w_out, ffn2_norm_pre, ffn2_norm_post, ffn2_w_up, ffn2_w_down):
    weights = (ffn1_norm_pre, ffn1_norm_post, ffn1_w_up, ffn1_w_down, mix_norm_pre, mix_norm_post,
               w_in, attn_sinks, conv_w, dn_a_log, dn_dt_bias, dn_out_norm, w_branch_a, w_branch_b,
               w_out, ffn2_norm_pre, ffn2_norm_post, ffn2_w_up, ffn2_w_down)
    depth = w_in.shape[0]
    dims = dict(kv_heads=cache_swa_k.shape[3], head_dim=cache_swa_k.shape[4],
                b_heads=state_delta.shape[2], dk=state_delta.shape[3], dv=state_delta.shape[4])
    yp, ys = x_prompt, x_sample
    p_out, s_out = [], []
    for l in range(depth):
        yp, ys, p_new, s_new = _layer(yp, ys, cache_swa_k[l], cache_swa_v[l], state_conv[l],
                                      state_delta[l], tuple(w[l] for w in weights), dims)
        p_out.append(p_new)
        s_out.append(s_new)
    stack = lambda outs, i: jnp.stack([o[i] for o in outs])
    return (yp, ys,
            stack(p_out, 0), stack(p_out, 1), stack(p_out, 2), stack(p_out, 3),
            stack(s_out, 0), stack(s_out, 1), stack(s_out, 2), stack(s_out, 3))
```
